```python
import jax, jax.numpy as jnp
from jax import lax
import numpy as np

D_MODEL = 1024
BATCH = 32
SEQ = 2048
DEPTH = 4

N_A_LAYERS = DEPTH // 2
N_B_LAYERS = DEPTH - N_A_LAYERS
N_DENSE_LAYERS = (DEPTH + 1) // 2
N_MOE_LAYERS = DEPTH // 2

A_HEADS = 8
A_HEAD_DIM = D_MODEL // A_HEADS
A_WIDTH = A_HEADS * A_HEAD_DIM
A_CHUNK = 32

B_HEADS = 16
B_KV_HEADS = 4
B_GROUP = B_HEADS // B_KV_HEADS
B_HEAD_DIM = D_MODEL // B_HEADS
CMP_BLOCK = 32
CMP_STRIDE = 16
CMP_HIDDEN = 256
SEL_BLOCK = 32
N_SELECT = 8
WINDOW = 512
Q_BLOCK = 64
N_BRANCH = 3

D_FF = 2816
N_EXPERTS = 8
TOP_K = 2
D_FF_EXPERT = 3584

NORM_EPS = 1e-6
SEL_BIG = 1e9
TINY = 1e-30

kernel_name = "yoco_hgrn2_nsa_moe_trunk"


def rms_norm(x, g):
    xf = x.astype(jnp.float32)
    y = xf * lax.rsqrt(jnp.mean(xf * xf, axis=-1, keepdims=True) + NORM_EPS)
    return (y * g.astype(jnp.float32)).astype(x.dtype)


def masked_softmax(s, mask):
    s = jnp.where(mask, s, -jnp.inf)
    m = jnp.max(s, axis=-1, keepdims=True)
    m = jnp.where(jnp.isfinite(m), m, 0.0)
    e = jnp.exp(s - m)
    return e / jnp.maximum(jnp.sum(e, axis=-1, keepdims=True), TINY)


def alibi_slopes(n):
    return np.array([2.0 ** (-8.0 * (h + 1) / n) for h in range(n)], dtype=np.float32)


def hgrn2_lower_bounds(lb_logits):
    lb = jnp.cumsum(jax.nn.softmax(lb_logits.astype(jnp.float32), axis=0), axis=0)
    return lb - lb[0:1]


def hgrn2_chunk_scan(q, k, v, logf):
    B, H, S, dk = q.shape
    dv = v.shape[-1]
    nc = S // A_CHUNK

    def to_chunks(a):
        return a.reshape(B, H, nc, A_CHUNK, a.shape[-1]).transpose(2, 0, 1, 3, 4)

    causal = jnp.tril(jnp.ones((A_CHUNK, A_CHUNK), dtype=bool))[:, :, None]

    def step(state, inp):
        qc, kc, vc, lf = inp
        cum = jnp.cumsum(lf, axis=2)
        o_inter = jnp.einsum('bhtk,bhkv->bhtv', qc * jnp.exp(cum), state)
        diff = cum[:, :, :, None, :] - cum[:, :, None, :, :]
        decay = jnp.exp(jnp.where(causal, diff, -jnp.inf))
        scores = jnp.einsum('bhtk,bhsk,bhtsk->bhts', qc, kc, decay)
        o = o_inter + jnp.einsum('bhts,bhsv->bhtv', scores, vc)
        last = cum[:, :, -1:, :]
        new_state = (jnp.exp(last[:, :, 0, :])[..., None] * state
                     + jnp.einsum('bhsk,bhsv->bhkv', kc * jnp.exp(last - cum), vc))
        return new_state, o

    state0 = jnp.zeros((B, H, dk, dv), jnp.float32)
    _, o = lax.scan(step, state0, (to_chunks(q), to_chunks(k), to_chunks(v), to_chunks(logf)))
    return o.transpose(1, 2, 0, 3, 4).reshape(B, H, S, dv)


def hgrn2_mixer(xn, w_in, w_out, g_norm, lb):
    B, S, _ = xn.shape
    q, f, i, g = jnp.split(xn @ w_in, 4, axis=-1)
    f32 = f.astype(jnp.float32)
    logf = jnp.logaddexp(jnp.log(lb), jnp.log1p(-lb) + jax.nn.log_sigmoid(f32))
    key = (1.0 - lb) * jax.nn.sigmoid(-f32)

    def heads(a):
        return a.reshape(B, S, A_HEADS, A_HEAD_DIM).transpose(0, 2, 1, 3).astype(jnp.float32)

    o = hgrn2_chunk_scan(heads(jax.nn.silu(q)), heads(key), heads(i), heads(logf))
    o = rms_norm(o.transpose(0, 2, 1, 3), g_norm).reshape(B, S, A_WIDTH).astype(xn.dtype)
    return (o * jax.nn.silu(g)) @ w_out


def compress_blocks(a, pos, w1, w2):
    B, S = a.shape[:2]
    n_cmp = (S - CMP_BLOCK) // CMP_STRIDE + 1
    idx = np.arange(n_cmp)[:, None] * CMP_STRIDE + np.arange(CMP_BLOCK)[None, :]
    blocks = a[:, idx] + pos[None, None, :, None, :]
    blocks = blocks.transpose(0, 3, 1, 2, 4).reshape(B, B_KV_HEADS, n_cmp, CMP_BLOCK * B_HEAD_DIM)
    return jax.nn.silu(blocks @ w1) @ w2


def nsa_shared_kv(h, kv_norm, kv_w, cmp_pos_k, cmp_w1_k, cmp_w2_k, cmp_pos_v, cmp_w1_v, cmp_w2_v, k_norm):
    B, S, _ = h.shape
    hn = rms_norm(h, kv_norm)
    kv = (hn @ kv_w).reshape(B, S, 6, B_KV_HEADS, B_HEAD_DIM)
    k_c, v_c, k_s, v_s, k_w, v_w = (kv[:, :, 0], kv[:, :, 1], kv[:, :, 2], kv[:, :, 3], kv[:, :, 4], kv[:, :, 5])
    kc = rms_norm(compress_blocks(k_c, cmp_pos_k, cmp_w1_k, cmp_w2_k), k_norm[0])
    vc = compress_blocks(v_c, cmp_pos_v, cmp_w1_v, cmp_w2_v)
    n_sel = S // SEL_BLOCK
    ks = rms_norm(k_s, k_norm[1]).transpose(0, 2, 1, 3).reshape(B, B_KV_HEADS, n_sel, SEL_BLOCK * B_HEAD_DIM)
    vs = v_s.transpose(0, 2, 1, 3).reshape(B, B_KV_HEADS, n_sel, SEL_BLOCK * B_HEAD_DIM)
    pad = ((0, 0), (0, 0), (WINDOW, 0), (0, 0))
    kw = jnp.pad(rms_norm(k_w, k_norm[2]).transpose(0, 2, 1, 3), pad)
    vw = jnp.pad(v_w.transpose(0, 2, 1, 3), pad)
    return kc, vc, ks, vs, kw, vw


def nsa_mixer(xn, w_in, w_out, q_norm, kc, vc, ks, vs, kw, vw):
    B, S, _ = xn.shape
    dh = B_HEAD_DIM
    scale = dh ** -0.5
    n_cmp = kc.shape[2]
    n_sel = S // SEL_BLOCK
    n_pick = min(N_SELECT, n_sel)
    n_qb = S // Q_BLOCK
    c_start = np.arange(n_cmp) * CMP_STRIDE
    c_end = jnp.asarray(c_start + CMP_BLOCK - 1)
    j_sel = np.arange(n_sel)
    overlap = jnp.asarray(((c_start[:, None] < (j_sel[None, :] + 1) * SEL_BLOCK)
                           & (c_start[:, None] + CMP_BLOCK > j_sel[None, :] * SEL_BLOCK)).astype(np.float32))
    j_sel = jnp.asarray(j_sel)
    slopes = jnp.asarray(alibi_slopes(B_HEADS).reshape(B_KV_HEADS, B_GROUP))[None, None, :, :, None]
    gather = jax.vmap(jax.vmap(lambda a, i: a[i]))

    proj = xn @ w_in
    q = rms_norm(proj[..., :B_HEADS * dh].reshape(B, S, B_HEADS, dh), q_norm)
    gates = jax.nn.sigmoid(proj[..., B_HEADS * dh:].astype(jnp.float32)).astype(xn.dtype)
    q_blocks = jnp.moveaxis(q.reshape(B, n_qb, Q_BLOCK, B_KV_HEADS, B_GROUP, dh), 1, 0)
    g_blocks = jnp.moveaxis(gates.reshape(B, n_qb, Q_BLOCK, B_KV_HEADS, B_GROUP, N_BRANCH), 1, 0)

    def block(args):
        qb, gb, blk = args
        t = blk * Q_BLOCK + jnp.arange(Q_BLOCK)
        dist_c = t[:, None] - c_end[None, :]
        s_c = (jnp.einsum('bqghd,bgcd->bqghc', qb, kc, preferred_element_type=jnp.float32) * scale
               - slopes * dist_c.astype(jnp.float32)[None, :, None, None, :])
        p_c = masked_softmax(s_c, (dist_c >= 0)[None, :, None, None, :])
        o_c = jnp.einsum('bqghc,bgcd->bqghd', p_c.astype(vc.dtype), vc)
        imp = jnp.einsum('bqghc,cj->bqgj', p_c, overlap)
        valid = j_sel[None, :] * SEL_BLOCK <= t[:, None]
        cur = t // SEL_BLOCK
        forced = valid & ((j_sel[None, :] == 0) | (j_sel[None, :] == cur[:, None]) | (j_sel[None, :] == cur[:, None] - 1))
        score = jnp.where(forced[None, :, None, :], SEL_BIG, jnp.where(valid[None, :, None, :], imp, -SEL_BIG))
        vals, idx = lax.top_k(score, n_pick)
        sel_ok = vals > -0.5 * SEL_BIG
        idx_bg = idx.transpose(0, 2, 1, 3).reshape(B, B_KV_HEADS, Q_BLOCK * n_pick)
        k_sel = gather(ks, idx_bg).reshape(B, B_KV_HEADS, Q_BLOCK, n_pick * SEL_BLOCK, dh)
        v_sel = gather(vs, idx_bg).reshape(B, B_KV_HEADS, Q_BLOCK, n_pick * SEL_BLOCK, dh)
        pos_sel = idx[..., None] * SEL_BLOCK + jnp.arange(SEL_BLOCK)
        ok_sel = (sel_ok[..., None] & (pos_sel <= t[None, :, None, None, None])).reshape(B, Q_BLOCK, B_KV_HEADS, -1)
        dist_s = (t[None, :, None, None] - pos_sel.reshape(B, Q_BLOCK, B_KV_HEADS, -1)).astype(jnp.float32)
        s_s = (jnp.einsum('bqghd,bgqkd->bqghk', qb, k_sel, preferred_element_type=jnp.float32) * scale
               - slopes * dist_s[:, :, :, None, :])
        p_s = masked_softmax(s_s, ok_sel[:, :, :, None, :])
        o_s = jnp.einsum('bqghk,bgqkd->bqghd', p_s.astype(v_sel.dtype), v_sel)
        k_win = lax.dynamic_slice_in_dim(kw, blk * Q_BLOCK, WINDOW + Q_BLOCK, axis=2)
        v_win = lax.dynamic_slice_in_dim(vw, blk * Q_BLOCK, WINDOW + Q_BLOCK, axis=2)
        pos_w = blk * Q_BLOCK - WINDOW + jnp.arange(WINDOW + Q_BLOCK)
        dist_w = t[:, None] - pos_w[None, :]
        mask_w = (pos_w[None, :] >= 0) & (dist_w >= 0) & (dist_w < WINDOW)
        s_w = (jnp.einsum('bqghd,bgkd->bqghk', qb, k_win, preferred_element_type=jnp.float32) * scale
               - slopes * dist_w.astype(jnp.float32)[None, :, None, None, :])
        p_w = masked_softmax(s_w, mask_w[None, :, None, None, :])
        o_w = jnp.einsum('bqghk,bgkd->bqghd', p_w.astype(v_win.dtype), v_win)
        return gb[..., 0:1] * o_c + gb[..., 1:2] * o_s + gb[..., 2:3] * o_w

    o = lax.map(block, (q_blocks, g_blocks, jnp.arange(n_qb)))
    o = jnp.moveaxis(o, 0, 1).reshape(B, S, B_HEADS * dh)
    return o @ w_out


def swiglu(xn, w_gu, w_down):
    gate, up = jnp.split(xn @ w_gu, 2, axis=-1)
    return (jax.nn.silu(gate) * up) @ w_down


def moe_swiglu(xn, w_router, w_gu, w_down):
    logits = (xn @ w_router).astype(jnp.float32)
    vals, idx = lax.top_k(logits, TOP_K)
    wts = jax.nn.softmax(vals, axis=-1)
    combine = jnp.sum(jax.nn.one_hot(idx, N_EXPERTS, dtype=jnp.float32) * wts[..., None], axis=-2).astype(xn.dtype)
    out = jnp.zeros_like(xn)
    for e in range(N_EXPERTS):
        out = out + combine[..., e:e + 1] * swiglu(xn, w_gu[e], w_down[e])
    return out


def _normal(key, shape, scale):
    return jax.random.normal(key, shape, jnp.float32) * scale


def setup_inputs(seed: int = 0) -> dict:
    key = jax.random.key(seed)
    ks = jax.random.split(key, 25)
    dh = B_HEAD_DIM
    kv_cols = 6 * B_KV_HEADS * dh
    nsa_cols = B_HEADS * dh + N_BRANCH * B_HEADS
    return {
        "x": _normal(ks[0], (BATCH, SEQ, D_MODEL), 1.0),
        "norm_mix": 1.0 + _normal(ks[1], (DEPTH, D_MODEL), 0.02),
        "norm_ffn": 1.0 + _normal(ks[2], (DEPTH, D_MODEL), 0.02),
        "hgrn_w_in": _normal(ks[3], (N_A_LAYERS, D_MODEL, 4 * A_WIDTH), D_MODEL ** -0.5),
        "hgrn_w_out": _normal(ks[4], (N_A_LAYERS, A_WIDTH, D_MODEL), A_WIDTH ** -0.5),
        "hgrn_g_norm": 1.0 + _normal(ks[5], (N_A_LAYERS, A_HEAD_DIM), 0.02),
        "hgrn_lb_logits": _normal(ks[6], (N_A_LAYERS, A_WIDTH), 0.5),
        "kv_norm": 1.0 + _normal(ks[7], (D_MODEL,), 0.02),
        "kv_w": _normal(ks[8], (D_MODEL, kv_cols), D_MODEL ** -0.5),
        "cmp_pos_k": _normal(ks[9], (CMP_BLOCK, dh), 0.1),
        "cmp_w1_k": _normal(ks[10], (CMP_BLOCK * dh, CMP_HIDDEN), (CMP_BLOCK * dh) ** -0.5),
        "cmp_w2_k": _normal(ks[11], (CMP_HIDDEN, dh), CMP_HIDDEN ** -0.5),
        "cmp_pos_v": _normal(ks[12], (CMP_BLOCK, dh), 0.1),
        "cmp_w1_v": _normal(ks[13], (CMP_BLOCK * dh, CMP_HIDDEN), (CMP_BLOCK * dh) ** -0.5),
        "cmp_w2_v": _normal(ks[14], (CMP_HIDDEN, dh), CMP_HIDDEN ** -0.5),
        "k_norm": 1.0 + _normal(ks[15], (N_BRANCH, dh), 0.02),
        "nsa_w_in": _normal(ks[16], (N_B_LAYERS, D_MODEL, nsa_cols), D_MODEL ** -0.5),
        "nsa_w_out": _normal(ks[17], (N_B_LAYERS, B_HEADS * dh, D_MODEL), (B_HEADS * dh) ** -0.5),
        "nsa_q_norm": 1.0 + _normal(ks[18], (N_B_LAYERS, dh), 0.02),
        "ffn_w_gu": _normal(ks[19], (N_DENSE_LAYERS, D_MODEL, 2 * D_FF), D_MODEL ** -0.5),
        "ffn_w_down": _normal(ks[20], (N_DENSE_LAYERS, D_FF, D_MODEL), D_FF ** -0.5),
        "moe_router": _normal(ks[21], (N_MOE_LAYERS, D_MODEL, N_EXPERTS), D_MODEL ** -0.5),
        "moe_w_gu": _normal(ks[22], (N_MOE_LAYERS, N_EXPERTS, D_MODEL, 2 * D_FF_EXPERT), D_MODEL ** -0.5),
        "moe_w_down": _normal(ks[23], (N_MOE_LAYERS, N_EXPERTS, D_FF_EXPERT, D_MODEL), D_FF_EXPERT ** -0.5),
    }


def reference(x, norm_mix, norm_ffn, hgrn_w_in, hgrn_w_out, hgrn_g_norm, hgrn_lb_logits,
              kv_norm, kv_w, cmp_pos_k, cmp_w1_k, cmp_w2_k, cmp_pos_v, cmp_w1_v, cmp_w2_v, k_norm,
              nsa_w_in, nsa_w_out, nsa_q_norm, ffn_w_gu, ffn_w_down, moe_router, moe_w_gu, moe_w_down):
    lower_bounds = hgrn2_lower_bounds(hgrn_lb_logits)
    h = x
    kc = vc = ks = vs = kw = vw = None
    for layer in range(DEPTH):
        xn = rms_norm(h, norm_mix[layer])
        if layer < N_A_LAYERS:
            h = h + hgrn2_mixer(xn, hgrn_w_in[layer], hgrn_w_out[layer], hgrn_g_norm[layer], lower_bounds[layer])
        else:
            if layer == N_A_LAYERS:
                kc, vc, ks, vs, kw, vw = nsa_shared_kv(h, kv_norm, kv_w, cmp_pos_k, cmp_w1_k, cmp_w2_k,
                                                       cmp_pos_v, cmp_w1_v, cmp_w2_v, k_norm)
            j = layer - N_A_LAYERS
            h = h + nsa_mixer(xn, nsa_w_in[j], nsa_w_out[j], nsa_q_norm[j], kc, vc, ks, vs, kw, vw)
        xn = rms_norm(h, norm_ffn[layer])
        if layer % 2 == 0:
            h = h + swiglu(xn, ffn_w_gu[layer // 2], ffn_w_down[layer // 2])
        else:
            h = h + moe_swiglu(xn, moe_router[layer // 2], moe_w_gu[layer // 2], moe_w_down[layer // 2])
    return h
```

```python
import functools

import numpy as np
import jax
import jax.numpy as jnp
from jax import lax
from jax.experimental import pallas as pl
from jax.experimental.pallas import tpu as pltpu

F32 = jnp.float32
BF16 = jnp.bfloat16

NORM_EPS = 1e-6
VMEM_LIMIT_BYTES = 56 * 1024 * 1024

A_HEADS = 8
A_HEAD_DIM = 128
SCAN_ROWS = 128


def _cparams(n_axes):
    return pltpu.CompilerParams(dimension_semantics=("arbitrary",) * n_axes,
                                vmem_limit_bytes=VMEM_LIMIT_BYTES)


def _dot(a, b):
    return jnp.dot(a, b, preferred_element_type=F32)


def _dot_nt(a, b):
    return lax.dot_general(a, b, (((1,), (1,)), ((), ())), preferred_element_type=F32)


def _dot_tn(a, b):
    return lax.dot_general(a, b, (((0,), (0,)), ((), ())), preferred_element_type=F32)


def _sigmoid(x):
    return 1.0 / (1.0 + jnp.exp(-x))


def _rms(x, g):
    return x * lax.rsqrt(jnp.mean(x * x, axis=-1, keepdims=True) + NORM_EPS) * g


def _split_bf16(x):
    hi = x.astype(BF16)
    lo = (x - hi.astype(F32)).astype(BF16)
    return hi, lo


def _hgrn_constants():
    n = SCAN_ROWS
    t = np.arange(n)
    r = np.arange(n)[None, :]
    sums, masks = [], []
    m = n // 2
    while m >= 1:
        grp = t // (2 * m)
        mid = grp * 2 * m + m
        upper = (t % (2 * m)) >= m
        sums.append(np.where(upper[:, None], (r >= mid[:, None]) & (r <= t[:, None]),
                             (r > t[:, None]) & (r < mid[:, None])))
        masks.append((grp[:, None] == grp[None, :]) & upper[:, None] & (~upper[None, :]))
        m //= 2
    masks.append(np.eye(n, dtype=bool))
    sums.append(r <= t[:, None])
    sums.append(r > t[:, None])
    w = np.concatenate(sums, 0).astype(np.float32)
    return np.concatenate([w, w], 1), np.stack(masks).astype(np.float32)


def _hgrn_kernel(x_ref, gn_ref, win_ref, lb_ref, ghn_ref, wout_ref, wsum_ref, msk_ref,
                 out_ref, xn_scr, st_scr, *, tm):
    sblk = pl.program_id(1)
    h = pl.program_id(2)
    n = SCAN_ROWS
    n_lvl = msk_ref.shape[0] - 1

    @pl.when(h == 0)
    def _():
        xn_scr[...] = _rms(x_ref[...], gn_ref[...]).astype(BF16)

    @pl.when(sblk == 0)
    def _():
        st_scr[h] = jnp.zeros((A_HEAD_DIM, A_HEAD_DIM), F32)

    proj = _dot(xn_scr[...], win_ref[...])
    q = proj[:, 0:128]
    f = proj[:, 128:256]
    v = proj[:, 256:384]
    g = proj[:, 384:512]
    lb = lb_ref[0:1, :]
    log_lb = lb_ref[1:2, :]
    log_1m_lb = lb_ref[2:3, :]
    qs = q * _sigmoid(q)
    log_sig = jnp.minimum(f, 0.0) - jnp.log1p(jnp.exp(-jnp.abs(f)))
    b = log_1m_lb + log_sig
    logf = jnp.maximum(log_lb, b) + jnp.log1p(jnp.exp(-jnp.abs(log_lb - b)))
    key = (1.0 - lb) * _sigmoid(-f)

    st = st_scr[h]
    outs = []
    for r in range(tm // n):
        rows = slice(r * n, (r + 1) * n)
        hi, lo = _split_bf16(logf[rows])
        e = jnp.exp(_dot(wsum_ref[...], jnp.concatenate([hi, lo], axis=0)))
        qb = qs[rows]
        kb = key[rows]
        vb = v[rows].astype(BF16)
        s = msk_ref[n_lvl] * _dot_nt(qb.astype(BF16), kb.astype(BF16))
        for l in range(n_lvl):
            el = e[l * n:(l + 1) * n]
            s = s + msk_ref[l] * _dot_nt((qb * el).astype(BF16), (kb * el).astype(BF16))
        e_prefix = e[n_lvl * n:(n_lvl + 1) * n]
        e_suffix = e[(n_lvl + 1) * n:(n_lvl + 2) * n]
        o = _dot(s.astype(BF16), vb) + _dot_nt((qb * e_prefix).astype(BF16), st.astype(BF16))
        st = st * e_prefix[n - 1:n, :] + _dot_tn(vb, (kb * e_suffix).astype(BF16))
        outs.append(o)
    st_scr[h] = st
    o = jnp.concatenate(outs, axis=0)

    og = (_rms(o, ghn_ref[...]) * (g * _sigmoid(g))).astype(BF16)
    contrib = _dot(og, wout_ref[...])

    @pl.when(h == 0)
    def _():
        out_ref[...] = x_ref[...] + contrib

    @pl.when(h > 0)
    def _():
        out_ref[...] += contrib


def hgrn_mixer(h2d, seq, g_norm_in, w_in, w_out, g_head, lb, *, tm=512):
    t_total, d = h2d.shape
    n_sblk = seq // tm
    hd = A_HEAD_DIM
    w_perm = w_in.reshape(d, 4, A_HEADS, hd).transpose(0, 2, 1, 3).reshape(d, 4 * A_HEADS * hd).astype(BF16)
    lb_rows = jnp.zeros((8, A_HEADS * hd), F32)
    lb_rows = lb_rows.at[0].set(lb).at[1].set(jnp.log(lb)).at[2].set(jnp.log1p(-lb))
    wsum, masks = _hgrn_constants()
    n_blk = wsum.shape[0]
    grid = (t_total // seq, n_sblk, A_HEADS)
    row = lambda b, s, h: (b * n_sblk + s, 0)
    const2 = lambda b, s, h: (0, 0)
    return pl.pallas_call(
        functools.partial(_hgrn_kernel, tm=tm),
        grid=grid,
        in_specs=[
            pl.BlockSpec((tm, d), row),
            pl.BlockSpec((1, d), const2),
            pl.BlockSpec((d, 4 * hd), lambda b, s, h: (0, h)),
            pl.BlockSpec((8, hd), lambda b, s, h: (0, h)),
            pl.BlockSpec((1, hd), const2),
            pl.BlockSpec((hd, d), lambda b, s, h: (h, 0)),
            pl.BlockSpec((n_blk, 2 * SCAN_ROWS), const2),
            pl.BlockSpec(masks.shape, lambda b, s, h: (0, 0, 0)),
        ],
        out_specs=pl.BlockSpec((tm, d), row),
        out_shape=jax.ShapeDtypeStruct((t_total, d), F32),
        scratch_shapes=[pltpu.VMEM((tm, d), BF16), pltpu.VMEM((A_HEADS, hd, hd), F32)],
        compiler_params=_cparams(3),
        name="hgrn_mixer",
    )(h2d, g_norm_in.reshape(1, d), w_perm, lb_rows, g_head.reshape(1, hd), w_out.astype(BF16),
      jnp.asarray(wsum, BF16), jnp.asarray(masks, F32))


def _resident(block_shape, index_map):
    return pl.BlockSpec(block_shape, index_map, pipeline_mode=pl.Buffered(1))


def _ffn_kernel(x_ref, gn_ref, wg_ref, wu_ref, wd_ref, out_ref, *, n_chunks):
    x = x_ref[...]
    xn = _rms(x, gn_ref[...]).astype(BF16)
    width = wg_ref.shape[1] // n_chunks
    acc = x
    for j in range(n_chunks):
        cols = slice(j * width, (j + 1) * width)
        gate = _dot(xn, wg_ref[:, cols])
        up = _dot(xn, wu_ref[:, cols])
        act = (gate * _sigmoid(gate) * up).astype(BF16)
        acc = acc + _dot(act, wd_ref[cols, :])
    out_ref[...] = acc


def ffn_swiglu(h2d, g_norm, w_gu, w_down, *, tm=512, n_chunks=2):
    t_total, d = h2d.shape
    ff = w_down.shape[0]
    w_gu = w_gu.astype(BF16)
    const2 = lambda i: (0, 0)
    return pl.pallas_call(
        functools.partial(_ffn_kernel, n_chunks=n_chunks),
        grid=(t_total // tm,),
        in_specs=[
            pl.BlockSpec((tm, d), lambda i: (i, 0)),
            _resident((1, d), const2),
            _resident((d, ff), const2),
            _resident((d, ff), lambda i: (0, 1)),
            _resident((ff, d), const2),
        ],
        out_specs=pl.BlockSpec((tm, d), lambda i: (i, 0)),
        out_shape=jax.ShapeDtypeStruct((t_total, d), F32),
        compiler_params=_cparams(1),
        name="ffn_swiglu",
    )(h2d, g_norm.reshape(1, d), w_gu, w_gu, w_down.astype(BF16))


N_EXPERTS = 8
LANES = 128
NEG = -1e30


def _router_kernel(x_ref, gn_ref, wh_ref, wl_ref, comb_ref, xn_ref):
    xn = _rms(x_ref[...], gn_ref[...])
    xn_ref[...] = xn.astype(BF16)
    xh, xl = _split_bf16(xn)
    logits = _dot(xh, wh_ref[...]) + (_dot(xl, wh_ref[...]) + _dot(xh, wl_ref[...]))
    lane = lax.broadcasted_iota(jnp.int32, logits.shape, 1).astype(F32)
    logits = jnp.where(lane < N_EXPERTS, logits, NEG)
    m1 = jnp.max(logits, axis=-1, keepdims=True)
    i1 = jnp.min(jnp.where(logits == m1, lane, float(LANES)), axis=-1, keepdims=True)
    rest = jnp.where(lane == i1, NEG, logits)
    m2 = jnp.max(rest, axis=-1, keepdims=True)
    i2 = jnp.min(jnp.where(rest == m2, lane, float(LANES)), axis=-1, keepdims=True)
    e2 = jnp.exp(m2 - m1)
    denom = 1.0 + e2
    comb_ref[...] = jnp.where(lane == i1, 1.0 / denom, jnp.where(lane == i2, e2 / denom, 0.0))


def moe_route(h2d, g_norm, w_router, *, tm=512):
    t_total, d = h2d.shape
    w_pad = jnp.zeros((d, LANES), F32).at[:, :N_EXPERTS].set(w_router)
    w_hi = w_pad.astype(BF16)
    w_lo = (w_pad - w_hi.astype(F32)).astype(BF16)
    const2 = lambda i: (0, 0)
    return pl.pallas_call(
        _router_kernel,
        grid=(t_total // tm,),
        in_specs=[
            pl.BlockSpec((tm, d), lambda i: (i, 0)),
            _resident((1, d), const2),
            _resident((d, LANES), const2),
            _resident((d, LANES), const2),
        ],
        out_specs=[pl.BlockSpec((tm, LANES), lambda i: (i, 0)), pl.BlockSpec((tm, d), lambda i: (i, 0))],
        out_shape=[jax.ShapeDtypeStruct((t_total, LANES), F32), jax.ShapeDtypeStruct((t_total, d), BF16)],
        compiler_params=_cparams(1),
        name="moe_router",
    )(h2d, g_norm.reshape(1, d), w_hi, w_lo)


def _moe_dense_kernel(h_ref, xn_ref, comb_ref, wg_ref, wu_ref, wd_ref, out_ref):
    e = pl.program_id(1)
    j = pl.program_id(2)

    @pl.when((e == 0) & (j == 0))
    def _():
        out_ref[...] = h_ref[...]

    comb = comb_ref[...]
    lane = lax.broadcasted_iota(jnp.int32, comb.shape, 1)
    cw = jnp.sum(jnp.where(lane == e, comb, 0.0), axis=-1, keepdims=True)
    xn = xn_ref[...]
    gate = _dot(xn, wg_ref[0])
    up = _dot(xn, wu_ref[0])
    act = (gate * _sigmoid(gate) * up).astype(BF16)
    out_ref[...] += cw * _dot(act, wd_ref[0])


def moe_dense(h2d, xn, comb, w_gu, w_down, *, tm=512, n_chunks=4):
    t_total, d = h2d.shape
    ff = w_down.shape[1]
    tf = ff // n_chunks
    w_gu = w_gu.astype(BF16)
    row = lambda i, e, j: (i, 0)
    return pl.pallas_call(
        _moe_dense_kernel,
        grid=(t_total // tm, N_EXPERTS, n_chunks),
        in_specs=[
            pl.BlockSpec((tm, d), row),
            pl.BlockSpec((tm, d), row),
            pl.BlockSpec((tm, LANES), row),
            pl.BlockSpec((1, d, tf), lambda i, e, j: (e, 0, j)),
            pl.BlockSpec((1, d, tf), lambda i, e, j: (e, 0, j + n_chunks)),
            pl.BlockSpec((1, tf, d), lambda i, e, j: (e, j, 0)),
        ],
        out_specs=pl.BlockSpec((tm, d), row),
        out_shape=jax.ShapeDtypeStruct((t_total, d), F32),
        compiler_params=_cparams(3),
        name="moe_dense",
    )(h2d, xn, comb, w_gu, w_gu, w_down.astype(BF16))


B_HEADS = 16
B_KV_HEADS = 4
B_GROUP = 4
B_HEAD_DIM = 64
CMP_STRIDE = 16
CMP_BLOCK = 32
SEL_BLOCK = 32
N_SELECT = 8
WINDOW = 512
SEL_BIG = 1e9
TINY = 1e-30
N_KV_KINDS = 6
ONES_LANE = B_HEAD_DIM


def _group_rms_rows(x, gain_col, n_groups):
    r, c = x.shape
    x3 = x.reshape(n_groups, r // n_groups, c)
    y = x3 * lax.rsqrt(jnp.mean(x3 * x3, axis=1, keepdims=True) + NORM_EPS)
    return y.reshape(r, c) * gain_col


def _ones_lane(shape):
    lane = lax.broadcasted_iota(jnp.int32, shape, len(shape) - 1)
    return jnp.where(lane % LANES == ONES_LANE, 1.0, 0.0)


def _kv_proj_kernel(x_ref, gn_ref, wkt_ref, kgain_ref, wv_ref, wc_ref,
                    kst_ref, kwt_ref, vs_ref, vw_ref, kc_ref, vc_ref):
    g = B_KV_HEADS
    hn = _rms(x_ref[...], gn_ref[...]).astype(BF16)
    kt = _dot_nt(wkt_ref[...], hn)
    kt = _group_rms_rows(kt, kgain_ref[...], 2 * g)
    half = g * B_HEAD_DIM
    tm = kt.shape[1]
    kst_ref[0] = kt[:half].reshape(g, B_HEAD_DIM, tm).astype(BF16)
    kwt_ref[0] = kt[half:].reshape(g, B_HEAD_DIM, tm).astype(BF16)
    v = _dot(hn, wv_ref[...])
    v = (v + _ones_lane(v.shape)).astype(BF16)
    for j in range(g):
        vs_ref[0, j] = v[:, j * LANES:(j + 1) * LANES]
        vw_ref[0, j] = v[:, (g + j) * LANES:(g + j + 1) * LANES]
    c = _dot(hn, wc_ref[...])
    kc_ref[...] = c[:, :half].astype(BF16)
    vc_ref[...] = c[:, half:].astype(BF16)


def nsa_kv_proj(h2d, bsz, seq, kv_norm, kv_w, k_norm, *, tm=512):
    t_total, d = h2d.shape
    g, dh = B_KV_HEADS, B_HEAD_DIM
    n_sblk = seq // tm
    w = kv_w.reshape(d, N_KV_KINDS, g, dh)
    wkt = jnp.concatenate([w[:, 2].reshape(d, g * dh), w[:, 4].reshape(d, g * dh)], axis=1).T.astype(BF16)
    kgain = jnp.concatenate([jnp.tile(k_norm[1], g), jnp.tile(k_norm[2], g)]).reshape(2 * g * dh, 1)
    wv = jnp.stack([w[:, 3], w[:, 5]], axis=1)
    wv = jnp.pad(wv, ((0, 0), (0, 0), (0, 0), (0, LANES - dh))).reshape(d, 2 * g * LANES).astype(BF16)
    wc = jnp.concatenate([w[:, 0].reshape(d, g * dh), w[:, 1].reshape(d, g * dh)], axis=1).astype(BF16)
    const2 = lambda b, s: (0, 0)
    row = lambda b, s: (b * n_sblk + s, 0)
    kt_spec = pl.BlockSpec((1, g, dh, tm), lambda b, s: (b, 0, 0, s))
    v_spec = pl.BlockSpec((1, g, tm, LANES), lambda b, s: (b, 0, s, 0))
    kt_shape = jax.ShapeDtypeStruct((bsz, g, dh, seq), BF16)
    v_shape = jax.ShapeDtypeStruct((bsz, g, seq, LANES), BF16)
    c_shape = jax.ShapeDtypeStruct((t_total, g * dh), BF16)
    return pl.pallas_call(
        _kv_proj_kernel,
        grid=(bsz, n_sblk),
        in_specs=[
            pl.BlockSpec((tm, d), row),
            _resident((1, d), const2),
            _resident((2 * g * dh, d), const2),
            _resident((2 * g * dh, 1), const2),
            _resident((d, 2 * g * LANES), const2),
            _resident((d, 2 * g * dh), const2),
        ],
        out_specs=[kt_spec, kt_spec, v_spec, v_spec,
                   pl.BlockSpec((tm, g * dh), row), pl.BlockSpec((tm, g * dh), row)],
        out_shape=[kt_shape, kt_shape, v_shape, v_shape, c_shape, c_shape],
        compiler_params=_cparams(2),
        name="nsa_kv_proj",
    )(h2d, kv_norm.reshape(1, d), wkt, kgain, wv, wc)


def _compress_kernel(ak_ref, av_ref, posk_ref, posv_ref, w1k_ref, w1v_ref, w2kt_ref, w2v_ref, kgain_ref,
                     kct_ref, vca_ref, *, n_half):
    half_w = ak_ref.shape[1]

    def hidden(a_ref, pos_ref, w1_ref):
        a = a_ref[...]
        top = _dot(a, w1_ref[:half_w, :])
        bot = _dot(a, w1_ref[half_w:, :])
        bias = _dot(pos_ref[...], w1_ref[...])[0:1, :]
        pre = top + pltpu.roll(bot, shift=a.shape[0] - 1, axis=0) + bias
        return (pre * _sigmoid(pre)).astype(BF16)

    act_k = hidden(ak_ref, posk_ref, w1k_ref)
    kct = _dot_nt(w2kt_ref[...], act_k)
    kct = _group_rms_rows(kct, kgain_ref[...], 1).astype(BF16)
    for j in range(kct.shape[1] // n_half):
        kct_ref[j] = kct[:, j * n_half:(j + 1) * n_half]
    act_v = hidden(av_ref, posv_ref, w1v_ref)
    vca = _dot(act_v, w2v_ref[...])
    vca_ref[...] = (vca + _ones_lane(vca.shape)).astype(BF16)


def nsa_compress(kc_raw, vc_raw, bsz, seq, pos_k, w1_k, w2_k, pos_v, w1_v, w2_v, k_gain, *, groups_per_step=4):
    g, dh = B_KV_HEADS, B_HEAD_DIM
    n_half = seq // CMP_STRIDE
    feat = CMP_STRIDE * dh

    def to_half_blocks(a):
        a = a.reshape(bsz, n_half, CMP_STRIDE, g, dh).transpose(0, 3, 1, 2, 4)
        return a.reshape(bsz * g * n_half, feat)

    def pos_rows(pos):
        return jnp.zeros((8, 2 * feat), F32).at[0].set(pos.reshape(-1)).astype(BF16)

    hid = w1_k.shape[1]
    w2v = jnp.pad(w2_v, ((0, 0), (0, LANES - dh))).astype(BF16)
    rows = groups_per_step * n_half
    const2 = lambda i: (0, 0)
    return pl.pallas_call(
        functools.partial(_compress_kernel, n_half=n_half),
        grid=(bsz * g // groups_per_step,),
        in_specs=[
            pl.BlockSpec((rows, feat), lambda i: (i, 0)),
            pl.BlockSpec((rows, feat), lambda i: (i, 0)),
            _resident((8, 2 * feat), const2),
            _resident((8, 2 * feat), const2),
            _resident((2 * feat, hid), const2),
            _resident((2 * feat, hid), const2),
            _resident((dh, hid), const2),
            _resident((hid, LANES), const2),
            _resident((dh, 1), const2),
        ],
        out_specs=[pl.BlockSpec((groups_per_step, dh, n_half), lambda i: (i, 0, 0)),
                   pl.BlockSpec((rows, LANES), lambda i: (i, 0))],
        out_shape=[jax.ShapeDtypeStruct((bsz * g, dh, n_half), BF16),
                   jax.ShapeDtypeStruct((bsz * g * n_half, LANES), BF16)],
        compiler_params=_cparams(1),
        name="nsa_compress",
    )(to_half_blocks(kc_raw), to_half_blocks(vc_raw), pos_rows(pos_k), pos_rows(pos_v),
      w1_k.astype(BF16), w1_v.astype(BF16), w2_k.T.astype(BF16), w2v, k_gain.reshape(dh, 1))


def _q_proj_kernel(x_ref, gn_ref, w_ref, bd_ref, qgain_ref, q_ref, gate_ref):
    d_q = q_ref.shape[1]
    xn = _rms(x_ref[...], gn_ref[...]).astype(BF16)
    proj = _dot(xn, w_ref[...])
    q = proj[:, :d_q]
    sq_hi, sq_lo = _split_bf16(q * q)
    ms = (_dot(sq_hi, bd_ref[...]) + _dot(sq_lo, bd_ref[...])) * (1.0 / B_HEAD_DIM)
    q_ref[...] = (q * lax.rsqrt(ms + NORM_EPS) * qgain_ref[...]).astype(BF16)
    gate_ref[...] = _sigmoid(proj[:, d_q:])


def nsa_q_proj(h2d, g_norm, w_in, q_norm, *, tm=512):
    t_total, d = h2d.shape
    g, hpg, dh = B_KV_HEADS, B_GROUP, B_HEAD_DIM
    d_q = B_HEADS * dh
    wg = w_in[:, d_q:].reshape(d, g, hpg * 3)
    wg = jnp.pad(wg, ((0, 0), (0, 0), (0, LANES - hpg * 3))).reshape(d, g * LANES)
    w = jnp.concatenate([w_in[:, :d_q], wg], axis=1).astype(BF16)
    head_of = np.arange(d_q) // dh
    bd = jnp.asarray(head_of[:, None] == head_of[None, :], BF16)
    qgain = (jnp.tile(q_norm, B_HEADS) * (dh ** -0.5)).reshape(1, d_q)
    const2 = lambda i: (0, 0)
    return pl.pallas_call(
        _q_proj_kernel,
        grid=(t_total // tm,),
        in_specs=[
            pl.BlockSpec((tm, d), lambda i: (i, 0)),
            _resident((1, d), const2),
            _resident((d, d_q + g * LANES), const2),
            _resident((d_q, d_q), const2),
            _resident((1, d_q), const2),
        ],
        out_specs=[pl.BlockSpec((tm, d_q), lambda i: (i, 0)), pl.BlockSpec((tm, g * LANES), lambda i: (i, 0))],
        out_shape=[jax.ShapeDtypeStruct((t_total, d_q), BF16), jax.ShapeDtypeStruct((t_total, g * LANES), F32)],
        compiler_params=_cparams(1),
        name="nsa_q_proj",
    )(h2d, g_norm.reshape(1, d), w, bd, qgain)


def _nsa_attn_kernel(q_ref, gate_ref, slope_ref, kct_ref, vca_ref, kst_ref, vs_ref, kwt_ref, vw_ref,
                     ovl_ref, esel_ref, o_ref, *, tq, tk):
    hpg, dh = B_GROUP, B_HEAD_DIM
    t0 = pl.program_id(2) * tq
    rows = hpg * tq
    qb = q_ref[...]
    q_st = jnp.concatenate([qb[:, i * dh:(i + 1) * dh] for i in range(hpg)], axis=0)
    slope = slope_ref[0]
    t_q = t0 + lax.broadcasted_iota(jnp.int32, (tq, 1), 0)
    t_st = jnp.concatenate([t_q] * hpg, axis=0)

    n_half = kct_ref.shape[2]
    c_end = lax.broadcasted_iota(jnp.int32, (1, n_half), 1) * CMP_STRIDE + (CMP_BLOCK - 1)
    dist = t_st - c_end
    seen = dist >= 0
    s = jnp.where(seen, _dot(q_st, kct_ref[0]) - slope * dist.astype(F32), NEG)
    m = jnp.max(s, axis=-1, keepdims=True)
    e = jnp.where(seen, jnp.exp(s - m), 0.0)
    p = e / jnp.maximum(jnp.sum(e, axis=-1, keepdims=True), TINY)
    o_cmp = _dot(p.astype(BF16), vca_ref[...])[:, :dh]

    p_grp = p[0:tq]
    for i in range(1, hpg):
        p_grp = p_grp + p[i * tq:(i + 1) * tq]
    p_hi, p_lo = _split_bf16(p_grp)
    imp = _dot(p_hi, ovl_ref[...]) + _dot(p_lo, ovl_ref[...])
    blk = lax.broadcasted_iota(jnp.int32, (tq, LANES), 1)
    valid = blk * SEL_BLOCK <= t_q
    cur = t_q // SEL_BLOCK
    forced = valid & ((blk == 0) | (blk == cur) | (blk == cur - 1))
    work = jnp.where(forced, SEL_BIG, jnp.where(valid, imp, -SEL_BIG))
    blk_f = blk.astype(F32)
    chosen = jnp.zeros((tq, LANES), F32)
    for _ in range(N_SELECT):
        top = jnp.max(work, axis=-1, keepdims=True)
        first = jnp.min(jnp.where(work == top, blk_f, float(LANES)), axis=-1, keepdims=True)
        hit = blk_f == first
        chosen = jnp.where(hit & (top > -0.5 * SEL_BIG), 1.0, chosen)
        work = jnp.where(hit, -3e38, work)
    chosen = chosen.astype(BF16)

    def sel_step(kt, carry):
        m_run, acc = carry
        k0 = pl.multiple_of(kt * tk, tk)
        pos = k0 + lax.broadcasted_iota(jnp.int32, (1, tk), 1)
        picked = _dot(chosen, esel_ref[:, pl.ds(k0, tk)])
        pen = jnp.where((picked > 0.5) & (pos <= t_q), 0.0, NEG)
        pen = jnp.concatenate([pen] * hpg, axis=0)
        sc = _dot(q_st, kst_ref[0, 0, :, pl.ds(k0, tk)]) - slope * (t_st - pos).astype(F32) + pen
        m_new = jnp.maximum(m_run, jnp.max(sc, axis=-1, keepdims=True))
        pr = jnp.exp(sc - m_new).astype(BF16)
        acc = jnp.exp(m_run - m_new) * acc + _dot(pr, vs_ref[0, 0, pl.ds(k0, tk), :])
        return m_new, acc

    n_kt = (t0 + tq + tk - 1) // tk
    _, acc = lax.fori_loop(0, n_kt, sel_step,
                           (jnp.full((rows, 1), NEG, F32), jnp.zeros((rows, LANES), F32)))
    o_sel = acc[:, :dh] / acc[:, ONES_LANE:ONES_LANE + 1]

    span = WINDOW + tq
    w0 = pl.multiple_of(jnp.maximum(t0 - WINDOW, 0), LANES)
    pos = w0 + lax.broadcasted_iota(jnp.int32, (1, span), 1)
    dist = t_st - pos
    inside = (dist >= 0) & (dist < WINDOW)
    s = jnp.where(inside, _dot(q_st, kwt_ref[0, 0, :, pl.ds(w0, span)]) - slope * dist.astype(F32), NEG)
    m = jnp.max(s, axis=-1, keepdims=True)
    pr = jnp.exp(s - m).astype(BF16)
    acc = _dot(pr, vw_ref[0, 0, pl.ds(w0, span), :])
    o_win = acc[:, :dh] / acc[:, ONES_LANE:ONES_LANE + 1]

    gate = gate_ref[...]
    outs = []
    for i in range(hpg):
        r = slice(i * tq, (i + 1) * tq)
        outs.append(gate[:, 3 * i:3 * i + 1] * o_cmp[r] + gate[:, 3 * i + 1:3 * i + 2] * o_sel[r]
                    + gate[:, 3 * i + 2:3 * i + 3] * o_win[r])
    o_ref[...] = jnp.concatenate(outs, axis=1).astype(BF16)


def nsa_attention(q, gates, kct, vca, kst, vs, kwt, vw, bsz, seq, *, tq=128, tk=512):
    g, hpg, dh = B_KV_HEADS, B_GROUP, B_HEAD_DIM
    n_q = seq // tq
    n_half = seq // CMP_STRIDE
    n_cmp = n_half - 1
    n_sel = seq // SEL_BLOCK
    slopes = np.array([2.0 ** (-8.0 * (h + 1) / B_HEADS) for h in range(B_HEADS)], dtype=np.float32)
    slope_rows = np.repeat(slopes.reshape(g, hpg), tq, axis=1).reshape(g, hpg * tq, 1)
    c_start = np.arange(n_half) * CMP_STRIDE
    j_sel = np.arange(LANES)
    ovl = ((c_start[:, None] < (j_sel[None, :] + 1) * SEL_BLOCK)
           & (c_start[:, None] + CMP_BLOCK > j_sel[None, :] * SEL_BLOCK)
           & (np.arange(n_half)[:, None] < n_cmp) & (j_sel[None, :] < n_sel))
    esel = (np.arange(seq)[None, :] // SEL_BLOCK) == j_sel[:, None]
    row = lambda b, j, i: (b * n_q + i, j)
    per_bg = lambda b, j, i: (b, j, 0, 0)
    return pl.pallas_call(
        functools.partial(_nsa_attn_kernel, tq=tq, tk=tk),
        grid=(bsz, g, n_q),
        in_specs=[
            pl.BlockSpec((tq, hpg * dh), row),
            pl.BlockSpec((tq, LANES), row),
            pl.BlockSpec((1, hpg * tq, 1), lambda b, j, i: (j, 0, 0)),
            pl.BlockSpec((1, dh, n_half), lambda b, j, i: (b * g + j, 0, 0)),
            pl.BlockSpec((n_half, LANES), lambda b, j, i: (b * g + j, 0)),
            pl.BlockSpec((1, 1, dh, seq), per_bg),
            pl.BlockSpec((1, 1, seq, LANES), per_bg),
            pl.BlockSpec((1, 1, dh, seq), per_bg),
            pl.BlockSpec((1, 1, seq, LANES), per_bg),
            _resident((n_half, LANES), lambda b, j, i: (0, 0)),
            _resident((LANES, seq), lambda b, j, i: (0, 0)),
        ],
        out_specs=pl.BlockSpec((tq, hpg * dh), row),
        out_shape=jax.ShapeDtypeStruct((bsz * seq, B_HEADS * dh), BF16),
        compiler_params=_cparams(3),
        name="nsa_attention",
    )(q, gates, jnp.asarray(slope_rows), kct, vca, kst, vs, kwt, vw, jnp.asarray(ovl, BF16), jnp.asarray(esel, BF16))


def _out_proj_kernel(h_ref, o_ref, w_ref, out_ref):
    out_ref[...] = h_ref[...] + _dot(o_ref[...], w_ref[...])


def out_proj_residual(h2d, o, w_out, *, tm=512):
    t_total, d = h2d.shape
    k = o.shape[1]
    return pl.pallas_call(
        _out_proj_kernel,
        grid=(t_total // tm,),
        in_specs=[pl.BlockSpec((tm, d), lambda i: (i, 0)), pl.BlockSpec((tm, k), lambda i: (i, 0)),
                  _resident((k, d), lambda i: (0, 0))],
        out_specs=pl.BlockSpec((tm, d), lambda i: (i, 0)),
        out_shape=jax.ShapeDtypeStruct((t_total, d), F32),
        compiler_params=_cparams(1),
        name="out_proj_residual",
    )(h2d, o, w_out.astype(BF16))


def nsa_shared_kv(h2d, bsz, seq, kv_norm, kv_w, cmp_pos_k, cmp_w1_k, cmp_w2_k, cmp_pos_v, cmp_w1_v, cmp_w2_v, k_norm):
    kst, kwt, vs, vw, kc_raw, vc_raw = nsa_kv_proj(h2d, bsz, seq, kv_norm, kv_w, k_norm)
    kct, vca = nsa_compress(kc_raw, vc_raw, bsz, seq, cmp_pos_k, cmp_w1_k, cmp_w2_k,
                            cmp_pos_v, cmp_w1_v, cmp_w2_v, k_norm[0])
    return kct, vca, kst, vs, kwt, vw


def nsa_mixer(h2d, bsz, seq, g_norm, w_in, w_out, q_norm, kv):
    q, gates = nsa_q_proj(h2d, g_norm, w_in, q_norm)
    o = nsa_attention(q, gates, *kv, bsz, seq)
    return out_proj_residual(h2d, o, w_out)


def kernel(x, norm_mix, norm_ffn, hgrn_w_in, hgrn_w_out, hgrn_g_norm, hgrn_lb_logits, kv_norm, kv_w, cmp_pos_k, cmp_w1_k, cmp_w2_k, cmp_pos_v, cmp_w1_v, cmp_w2_v, k_norm, nsa_w_in, nsa_w_out, nsa_q_norm, ffn_w_gu, ffn_w_down, moe_router, moe_w_gu, moe_w_down):
    bsz, seq, d = x.shape
    lb = jnp.cumsum(jax.nn.softmax(hgrn_lb_logits.astype(F32), axis=0), axis=0)
    lb = lb - lb[0:1]
    n_a = hgrn_w_in.shape[0]
    depth = norm_mix.shape[0]
    h = x.reshape(bsz * seq, d)
    kv = None
    for layer in range(depth):
        if layer < n_a:
            h = hgrn_mixer(h, seq, norm_mix[layer], hgrn_w_in[layer], hgrn_w_out[layer], hgrn_g_norm[layer],
                           lb[layer])
        else:
            if kv is None:
                kv = nsa_shared_kv(h, bsz, seq, kv_norm, kv_w, cmp_pos_k, cmp_w1_k, cmp_w2_k,
                                   cmp_pos_v, cmp_w1_v, cmp_w2_v, k_norm)
            j = layer - n_a
            h = nsa_mixer(h, bsz, seq, norm_mix[layer], nsa_w_in[j], nsa_w_out[j], nsa_q_norm[j], kv)
        if layer % 2 == 0:
            h = ffn_swiglu(h, norm_ffn[layer], ffn_w_gu[layer // 2], ffn_w_down[layer // 2])
        else:
            comb, xn = moe_route(h, norm_ffn[layer], moe_router[layer // 2])
            h = moe_dense(h, xn, comb, moe_w_gu[layer // 2], moe_w_down[layer // 2])
    return h.reshape(bsz, seq, d)
```

```python
import functools

import numpy as np
import jax
import jax.numpy as jnp
from jax import lax
from jax.experimental import pallas as pl
from jax.experimental.pallas import tpu as pltpu

F32 = jnp.float32
BF16 = jnp.bfloat16

NORM_EPS = 1e-6
VMEM_LIMIT_BYTES = 56 * 1024 * 1024

A_HEADS = 8
A_HEAD_DIM = 128
SCAN_ROWS = 128


def _cparams(n_axes):
    return pltpu.CompilerParams(dimension_semantics=("arbitrary",) * n_axes,
                                vmem_limit_bytes=VMEM_LIMIT_BYTES)


def _dot(a, b):
    return jnp.dot(a, b, preferred_element_type=F32)


def _dot_nt(a, b):
    return lax.dot_general(a, b, (((1,), (1,)), ((), ())), preferred_element_type=F32)


def _dot_tn(a, b):
    return lax.dot_general(a, b, (((0,), (0,)), ((), ())), preferred_element_type=F32)


def _sigmoid(x):
    return 1.0 / (1.0 + jnp.exp(-x))


def _rms(x, g):
    return x * lax.rsqrt(jnp.mean(x * x, axis=-1, keepdims=True) + NORM_EPS) * g


def _split_bf16(x):
    hi = x.astype(BF16)
    lo = (x - hi.astype(F32)).astype(BF16)
    return hi, lo


def _hgrn_constants():
    n = SCAN_ROWS
    t = np.arange(n)
    r = np.arange(n)[None, :]
    sums, masks = [], []
    m = n // 2
    while m >= 1:
        grp = t // (2 * m)
        mid = grp * 2 * m + m
        upper = (t % (2 * m)) >= m
        sums.append(np.where(upper[:, None], (r >= mid[:, None]) & (r <= t[:, None]),
                             (r > t[:, None]) & (r < mid[:, None])))
        masks.append((grp[:, None] == grp[None, :]) & upper[:, None] & (~upper[None, :]))
        m //= 2
    masks.append(np.eye(n, dtype=bool))
    sums.append(r <= t[:, None])
    sums.append(r > t[:, None])
    w = np.concatenate(sums, 0).astype(np.float32)
    return np.concatenate([w, w], 1), np.stack(masks).astype(np.float32)


def _hgrn_kernel(x_ref, gn_ref, win_ref, lb_ref, ghn_ref, wout_ref, wsum_ref, msk_ref,
                 out_ref, xn_scr, st_scr, *, tm):
    sblk = pl.program_id(1)
    h = pl.program_id(2)
    n = SCAN_ROWS
    n_lvl = msk_ref.shape[0] - 1

    @pl.when(h == 0)
    def _():
        xn_scr[...] = _rms(x_ref[...], gn_ref[...]).astype(BF16)

    @pl.when(sblk == 0)
    def _():
        st_scr[h] = jnp.zeros((A_HEAD_DIM, A_HEAD_DIM), F32)

    proj = _dot(xn_scr[...], win_ref[...])
    q = proj[:, 0:128]
    f = proj[:, 128:256]
    v = proj[:, 256:384]
    g = proj[:, 384:512]
    lb = lb_ref[0:1, :]
    log_lb = lb_ref[1:2, :]
    log_1m_lb = lb_ref[2:3, :]
    qs = q * _sigmoid(q)
    log_sig = jnp.minimum(f, 0.0) - jnp.log1p(jnp.exp(-jnp.abs(f)))
    b = log_1m_lb + log_sig
    logf = jnp.maximum(log_lb, b) + jnp.log1p(jnp.exp(-jnp.abs(log_lb - b)))
    key = (1.0 - lb) * _sigmoid(-f)

    st = st_scr[h]
    outs = []
    for r in range(tm // n):
        rows = slice(r * n, (r + 1) * n)
        hi, lo = _split_bf16(logf[rows])
        e = jnp.exp(_dot(wsum_ref[...], jnp.concatenate([hi, lo], axis=0)))
        qb = qs[rows]
        kb = key[rows]
        vb = v[rows].astype(BF16)
        s = msk_ref[n_lvl] * _dot_nt(qb.astype(BF16), kb.astype(BF16))
        for l in range(n_lvl):
            el = e[l * n:(l + 1) * n]
            s = s + msk_ref[l] * _dot_nt((qb * el).astype(BF16), (kb * el).astype(BF16))
        e_prefix = e[n_lvl * n:(n_lvl + 1) * n]
        e_suffix = e[(n_lvl + 1) * n:(n_lvl + 2) * n]
        o = _dot(s.astype(BF16), vb) + _dot_nt((qb * e_prefix).astype(BF16), st.astype(BF16))
        st = st * e_prefix[n - 1:n, :] + _dot_tn(vb, (kb * e_suffix).astype(BF16))
        outs.append(o)
    st_scr[h] = st
    o = jnp.concatenate(outs, axis=0)

    og = (_rms(o, ghn_ref[...]) * (g * _sigmoid(g))).astype(BF16)
    contrib = _dot(og, wout_ref[...])

    @pl.when(h == 0)
    def _():
        out_ref[...] = x_ref[...] + contrib

    @pl.when(h > 0)
    def _():
        out_ref[...] += contrib


def hgrn_mixer(h2d, seq, g_norm_in, w_in, w_out, g_head, lb, *, tm=512):
    t_total, d = h2d.shape
    n_sblk = seq // tm
    hd = A_HEAD_DIM
    w_perm = w_in.reshape(d, 4, A_HEADS, hd).transpose(0, 2, 1, 3).reshape(d, 4 * A_HEADS * hd).astype(BF16)
    lb_rows = jnp.zeros((8, A_HEADS * hd), F32)
    lb_rows = lb_rows.at[0].set(lb).at[1].set(jnp.log(lb)).at[2].set(jnp.log1p(-lb))
    wsum, masks = _hgrn_constants()
    n_blk = wsum.shape[0]
    grid = (t_total // seq, n_sblk, A_HEADS)
    row = lambda b, s, h: (b * n_sblk + s, 0)
    const2 = lambda b, s, h: (0, 0)
    return pl.pallas_call(
        functools.partial(_hgrn_kernel, tm=tm),
        grid=grid,
        in_specs=[
            pl.BlockSpec((tm, d), row),
            pl.BlockSpec((1, d), const2),
            pl.BlockSpec((d, 4 * hd), lambda b, s, h: (0, h)),
            pl.BlockSpec((8, hd), lambda b, s, h: (0, h)),
            pl.BlockSpec((1, hd), const2),
            pl.BlockSpec((hd, d), lambda b, s, h: (h, 0)),
            pl.BlockSpec((n_blk, 2 * SCAN_ROWS), const2),
            pl.BlockSpec(masks.shape, lambda b, s, h: (0, 0, 0)),
        ],
        out_specs=pl.BlockSpec((tm, d), row),
        out_shape=jax.ShapeDtypeStruct((t_total, d), F32),
        scratch_shapes=[pltpu.VMEM((tm, d), BF16), pltpu.VMEM((A_HEADS, hd, hd), F32)],
        compiler_params=_cparams(3),
        name="hgrn_mixer",
    )(h2d, g_norm_in.reshape(1, d), w_perm, lb_rows, g_head.reshape(1, hd), w_out.astype(BF16),
      jnp.asarray(wsum, BF16), jnp.asarray(masks, F32))


def _resident(block_shape, index_map):
    return pl.BlockSpec(block_shape, index_map, pipeline_mode=pl.Buffered(1))


def _ffn_kernel(x_ref, gn_ref, wg_ref, wu_ref, wd_ref, out_ref, *, n_chunks):
    x = x_ref[...]
    xn = _rms(x, gn_ref[...]).astype(BF16)
    width = wg_ref.shape[1] // n_chunks
    acc = x
    for j in range(n_chunks):
        cols = slice(j * width, (j + 1) * width)
        gate = _dot(xn, wg_ref[:, cols])
        up = _dot(xn, wu_ref[:, cols])
        act = (gate * _sigmoid(gate) * up).astype(BF16)
        acc = acc + _dot(act, wd_ref[cols, :])
    out_ref[...] = acc


def ffn_swiglu(h2d, g_norm, w_gu, w_down, *, tm=512, n_chunks=2):
    t_total, d = h2d.shape
    ff = w_down.shape[0]
    w_gu = w_gu.astype(BF16)
    const2 = lambda i: (0, 0)
    return pl.pallas_call(
        functools.partial(_ffn_kernel, n_chunks=n_chunks),
        grid=(t_total // tm,),
        in_specs=[
            pl.BlockSpec((tm, d), lambda i: (i, 0)),
            _resident((1, d), const2),
            _resident((d, ff), const2),
            _resident((d, ff), lambda i: (0, 1)),
            _resident((ff, d), const2),
        ],
        out_specs=pl.BlockSpec((tm, d), lambda i: (i, 0)),
        out_shape=jax.ShapeDtypeStruct((t_total, d), F32),
        compiler_params=_cparams(1),
        name="ffn_swiglu",
    )(h2d, g_norm.reshape(1, d), w_gu, w_gu, w_down.astype(BF16))


N_EXPERTS = 8
LANES = 128
NEG = -1e30


def _router_kernel(x_ref, gn_ref, wh_ref, wl_ref, sel_ref):
    xn = _rms(x_ref[...], gn_ref[...])
    xh, xl = _split_bf16(xn)
    logits = _dot(xh, wh_ref[...]) + (_dot(xl, wh_ref[...]) + _dot(xh, wl_ref[...]))
    lane = lax.broadcasted_iota(jnp.int32, logits.shape, 1).astype(F32)
    logits = jnp.where(lane < N_EXPERTS, logits, NEG)
    m1 = jnp.max(logits, axis=-1, keepdims=True)
    i1 = jnp.min(jnp.where(logits == m1, lane, float(LANES)), axis=-1, keepdims=True)
    rest = jnp.where(lane == i1, NEG, logits)
    m2 = jnp.max(rest, axis=-1, keepdims=True)
    i2 = jnp.min(jnp.where(rest == m2, lane, float(LANES)), axis=-1, keepdims=True)
    e2 = jnp.exp(m2 - m1)
    denom = 1.0 + e2
    sel_ref[...] = jnp.where(lane == 0, i1, jnp.where(lane == 1, i2, jnp.where(
        lane == 2, 1.0 / denom, jnp.where(lane == 3, e2 / denom, 0.0))))


def moe_route(h2d, g_norm, w_router, *, tm=512):
    t_total, d = h2d.shape
    w_pad = jnp.zeros((d, LANES), F32).at[:, :N_EXPERTS].set(w_router)
    w_hi = w_pad.astype(BF16)
    w_lo = (w_pad - w_hi.astype(F32)).astype(BF16)
    const2 = lambda i: (0, 0)
    return pl.pallas_call(
        _router_kernel,
        grid=(t_total // tm,),
        in_specs=[
            pl.BlockSpec((tm, d), lambda i: (i, 0)),
            _resident((1, d), const2),
            _resident((d, LANES), const2),
            _resident((d, LANES), const2),
        ],
        out_specs=pl.BlockSpec((tm, LANES), lambda i: (i, 0)),
        out_shape=jax.ShapeDtypeStruct((t_total, LANES), F32),
        compiler_params=_cparams(1),
        name="moe_router",
    )(h2d, g_norm.reshape(1, d), w_hi, w_lo)


MOE_TILE = 1024
GATHER_UNROLL = 8


def _moe_plan(sel, tm):
    t_total = sel.shape[0]
    e = sel[:, :2].astype(jnp.int32)
    w = sel[:, 2:4]
    onehot = (e[:, :, None] == jnp.arange(N_EXPERTS, dtype=jnp.int32)[None, None, :]).astype(jnp.int32).sum(1)
    csum = jnp.cumsum(onehot, axis=0)
    rank = csum - onehot
    padded = (csum[-1] + tm - 1) // tm * tm
    ends = jnp.cumsum(padded)
    pos = (ends - padded)[e] + jnp.take_along_axis(rank, e, axis=1)
    n_slots = 2 * t_total + N_EXPERTS * tm
    flat = pos.reshape(-1)
    tok_of_slot = jnp.zeros((n_slots,), jnp.int32).at[flat].set(
        jnp.repeat(jnp.arange(t_total, dtype=jnp.int32), 2), unique_indices=True)
    w_of_slot = jnp.zeros((n_slots,), F32).at[flat].set(w.reshape(-1), unique_indices=True)
    tile_start = jnp.arange(n_slots // tm, dtype=jnp.int32) * tm
    tile_expert = jnp.minimum(jnp.searchsorted(ends, tile_start, side="right"), N_EXPERTS - 1).astype(jnp.int32)
    tile_used = (tile_start < ends[-1]).astype(jnp.int32)
    return tok_of_slot, w_of_slot.reshape(n_slots, 1), tile_expert, tile_used, pos[:, 0], pos[:, 1]


def _row_gather_start(idx_smem, src_hbm, dst, sem, n_rows):
    def body(r, carry):
        pltpu.make_async_copy(src_hbm.at[pl.ds(idx_smem[r], 1), :], dst.at[pl.ds(r, 1), :], sem).start()
        return carry
    lax.fori_loop(0, n_rows, body, 0, unroll=GATHER_UNROLL)


def _row_gather_wait(dst, sem):
    pltpu.make_async_copy(dst, dst, sem).wait()


def _moe_group_kernel(te_ref, used_ref, tok_hbm, h_hbm, gn_ref, wslot_ref, wg_ref, wu_ref, wd_ref, y_ref,
                      idx_smem, xbuf, xn_scr, row_sem, idx_sem, *, tm):
    i = pl.program_id(0)
    j = pl.program_id(1)
    n_tiles = pl.num_programs(0)
    slot = i % 2

    def idx_copy(tile):
        return pltpu.make_async_copy(tok_hbm.at[pl.ds(tile * tm, tm)], idx_smem, idx_sem)

    @pl.when((i == 0) & (j == 0))
    def _():
        first = idx_copy(0)
        first.start()
        first.wait()
        _row_gather_start(idx_smem, h_hbm, xbuf.at[0], row_sem.at[0], tm)

    @pl.when(j == 0)
    def _():
        _row_gather_wait(xbuf.at[slot], row_sem.at[slot])
        xn_scr[...] = _rms(xbuf[slot], gn_ref[...]).astype(BF16)

        @pl.when(i + 1 < n_tiles)
        def _():
            idx_copy(i + 1).start()

    @pl.when((j == 1) & (i + 1 < n_tiles))
    def _():
        idx_copy(i + 1).wait()
        _row_gather_start(idx_smem, h_hbm, xbuf.at[1 - slot], row_sem.at[1 - slot], tm)

    @pl.when(used_ref[i] == 0)
    def _():
        y_ref[...] = jnp.zeros(y_ref.shape, F32)

    @pl.when(used_ref[i] != 0)
    def _():
        xn = xn_scr[...]
        gate = _dot(xn, wg_ref[0])
        up = _dot(xn, wu_ref[0])
        act = (gate * _sigmoid(gate) * up).astype(BF16)
        contrib = wslot_ref[...] * _dot(act, wd_ref[0])

        @pl.when(j == 0)
        def _():
            y_ref[...] = contrib

        @pl.when(j > 0)
        def _():
            y_ref[...] += contrib


def _moe_combine_kernel(pa_hbm, pb_hbm, y_hbm, h_ref, out_ref, ia_smem, ib_smem, buf_a, buf_b, row_sem, idx_sem,
                        *, tm):
    i = pl.program_id(0)
    ca = pltpu.make_async_copy(pa_hbm.at[pl.ds(i * tm, tm)], ia_smem, idx_sem.at[0])
    cb = pltpu.make_async_copy(pb_hbm.at[pl.ds(i * tm, tm)], ib_smem, idx_sem.at[1])
    ca.start()
    cb.start()
    ca.wait()
    _row_gather_start(ia_smem, y_hbm, buf_a, row_sem.at[0], tm)
    cb.wait()
    _row_gather_start(ib_smem, y_hbm, buf_b, row_sem.at[1], tm)
    _row_gather_wait(buf_a, row_sem.at[0])
    _row_gather_wait(buf_b, row_sem.at[1])
    out_ref[...] = h_ref[...] + (buf_a[...] + buf_b[...])


def moe_swiglu_top2(h2d, g_norm, w_router, w_gu, w_down, *, n_chunks=4):
    t_total, d = h2d.shape
    tm = MOE_TILE
    ff = w_down.shape[1]
    tf = ff // n_chunks
    sel = moe_route(h2d, g_norm, w_router)
    tok_of_slot, w_of_slot, tile_expert, tile_used, pos_a, pos_b = _moe_plan(sel, tm)
    n_slots = tok_of_slot.shape[0]
    w_gu = w_gu.astype(BF16)
    row = lambda i, j, te, used: (i, 0)
    y = pl.pallas_call(
        functools.partial(_moe_group_kernel, tm=tm),
        grid_spec=pltpu.PrefetchScalarGridSpec(
            num_scalar_prefetch=2,
            grid=(n_slots // tm, n_chunks),
            in_specs=[
                pl.BlockSpec(memory_space=pl.ANY),
                pl.BlockSpec(memory_space=pl.ANY),
                pl.BlockSpec((1, d), lambda i, j, te, used: (0, 0)),
                pl.BlockSpec((tm, 1), row),
                pl.BlockSpec((1, d, tf), lambda i, j, te, used: (te[i], 0, j)),
                pl.BlockSpec((1, d, tf), lambda i, j, te, used: (te[i], 0, j + n_chunks)),
                pl.BlockSpec((1, tf, d), lambda i, j, te, used: (te[i], j, 0)),
            ],
            out_specs=pl.BlockSpec((tm, d), row),
            scratch_shapes=[
                pltpu.SMEM((tm,), jnp.int32),
                pltpu.VMEM((2, tm, d), F32),
                pltpu.VMEM((tm, d), BF16),
                pltpu.SemaphoreType.DMA((2,)),
                pltpu.SemaphoreType.DMA(()),
            ],
        ),
        out_shape=jax.ShapeDtypeStruct((n_slots, d), F32),
        compiler_params=_cparams(2),
        name="moe_group_ffn",
    )(tile_expert, tile_used, tok_of_slot, h2d, g_norm.reshape(1, d), w_of_slot, w_gu, w_gu, w_down.astype(BF16))
    return pl.pallas_call(
        functools.partial(_moe_combine_kernel, tm=tm),
        grid=(t_total // tm,),
        in_specs=[
            pl.BlockSpec(memory_space=pl.ANY),
            pl.BlockSpec(memory_space=pl.ANY),
            pl.BlockSpec(memory_space=pl.ANY),
            pl.BlockSpec((tm, d), lambda i: (i, 0)),
        ],
        out_specs=pl.BlockSpec((tm, d), lambda i: (i, 0)),
        out_shape=jax.ShapeDtypeStruct((t_total, d), F32),
        scratch_shapes=[
            pltpu.SMEM((tm,), jnp.int32),
            pltpu.SMEM((tm,), jnp.int32),
            pltpu.VMEM((tm, d), F32),
            pltpu.VMEM((tm, d), F32),
            pltpu.SemaphoreType.DMA((2,)),
            pltpu.SemaphoreType.DMA((2,)),
        ],
        compiler_params=_cparams(1),
        name="moe_combine",
    )(pos_a, pos_b, y, h2d)


B_HEADS = 16
B_KV_HEADS = 4
B_GROUP = 4
B_HEAD_DIM = 64
CMP_STRIDE = 16
CMP_BLOCK = 32
SEL_BLOCK = 32
N_SELECT = 8
WINDOW = 512
SEL_BIG = 1e9
TINY = 1e-30
N_KV_KINDS = 6
ONES_LANE = B_HEAD_DIM


def _group_rms_rows(x, gain_col, n_groups):
    r, c = x.shape
    x3 = x.reshape(n_groups, r // n_groups, c)
    y = x3 * lax.rsqrt(jnp.mean(x3 * x3, axis=1, keepdims=True) + NORM_EPS)
    return y.reshape(r, c) * gain_col


def _ones_lane(shape):
    lane = lax.broadcasted_iota(jnp.int32, shape, len(shape) - 1)
    return jnp.where(lane % LANES == ONES_LANE, 1.0, 0.0)


def _kv_proj_kernel(x_ref, gn_ref, wkt_ref, kgain_ref, wv_ref, wc_ref,
                    kst_ref, kwt_ref, vs_ref, vw_ref, kc_ref, vc_ref):
    g = B_KV_HEADS
    hn = _rms(x_ref[...], gn_ref[...]).astype(BF16)
    kt = _dot_nt(wkt_ref[...], hn)
    kt = _group_rms_rows(kt, kgain_ref[...], 2 * g)
    half = g * B_HEAD_DIM
    tm = kt.shape[1]
    kst_ref[0] = kt[:half].reshape(g, B_HEAD_DIM, tm).astype(BF16)
    kwt_ref[0] = kt[half:].reshape(g, B_HEAD_DIM, tm).astype(BF16)
    v = _dot(hn, wv_ref[...])
    v = (v + _ones_lane(v.shape)).astype(BF16)
    for j in range(g):
        vs_ref[0, j] = v[:, j * LANES:(j + 1) * LANES]
        vw_ref[0, j] = v[:, (g + j) * LANES:(g + j + 1) * LANES]
    c = _dot(hn, wc_ref[...])
    kc_ref[...] = c[:, :half].astype(BF16)
    vc_ref[...] = c[:, half:].astype(BF16)


def nsa_kv_proj(h2d, bsz, seq, kv_norm, kv_w, k_norm, *, tm=512):
    t_total, d = h2d.shape
    g, dh = B_KV_HEADS, B_HEAD_DIM
    n_sblk = seq // tm
    w = kv_w.reshape(d, N_KV_KINDS, g, dh)
    wkt = jnp.concatenate([w[:, 2].reshape(d, g * dh), w[:, 4].reshape(d, g * dh)], axis=1).T.astype(BF16)
    kgain = jnp.concatenate([jnp.tile(k_norm[1], g), jnp.tile(k_norm[2], g)]).reshape(2 * g * dh, 1)
    wv = jnp.stack([w[:, 3], w[:, 5]], axis=1)
    wv = jnp.pad(wv, ((0, 0), (0, 0), (0, 0), (0, LANES - dh))).reshape(d, 2 * g * LANES).astype(BF16)
    wc = jnp.concatenate([w[:, 0].reshape(d, g * dh), w[:, 1].reshape(d, g * dh)], axis=1).astype(BF16)
    const2 = lambda b, s: (0, 0)
    row = lambda b, s: (b * n_sblk + s, 0)
    kt_spec = pl.BlockSpec((1, g, dh, tm), lambda b, s: (b, 0, 0, s))
    v_spec = pl.BlockSpec((1, g, tm, LANES), lambda b, s: (b, 0, s, 0))
    kt_shape = jax.ShapeDtypeStruct((bsz, g, dh, seq), BF16)
    v_shape = jax.ShapeDtypeStruct((bsz, g, seq, LANES), BF16)
    c_shape = jax.ShapeDtypeStruct((t_total, g * dh), BF16)
    return pl.pallas_call(
        _kv_proj_kernel,
        grid=(bsz, n_sblk),
        in_specs=[
            pl.BlockSpec((tm, d), row),
            _resident((1, d), const2),
            _resident((2 * g * dh, d), const2),
            _resident((2 * g * dh, 1), const2),
            _resident((d, 2 * g * LANES), const2),
            _resident((d, 2 * g * dh), const2),
        ],
        out_specs=[kt_spec, kt_spec, v_spec, v_spec,
                   pl.BlockSpec((tm, g * dh), row), pl.BlockSpec((tm, g * dh), row)],
        out_shape=[kt_shape, kt_shape, v_shape, v_shape, c_shape, c_shape],
        compiler_params=_cparams(2),
        name="nsa_kv_proj",
    )(h2d, kv_norm.reshape(1, d), wkt, kgain, wv, wc)


def _compress_kernel(ak_ref, av_ref, posk_ref, posv_ref, w1k_ref, w1v_ref, w2kt_ref, w2v_ref, kgain_ref,
                     kct_ref, vca_ref, *, n_half):
    half_w = ak_ref.shape[1]

    def hidden(a_ref, pos_ref, w1_ref):
        a = a_ref[...]
        top = _dot(a, w1_ref[:half_w, :])
        bot = _dot(a, w1_ref[half_w:, :])
        bias = _dot(pos_ref[...], w1_ref[...])[0:1, :]
        pre = top + pltpu.roll(bot, shift=a.shape[0] - 1, axis=0) + bias
        return (pre * _sigmoid(pre)).astype(BF16)

    act_k = hidden(ak_ref, posk_ref, w1k_ref)
    kct = _dot_nt(w2kt_ref[...], act_k)
    kct = _group_rms_rows(kct, kgain_ref[...], 1).astype(BF16)
    for j in range(kct.shape[1] // n_half):
        kct_ref[j] = kct[:, j * n_half:(j + 1) * n_half]
    act_v = hidden(av_ref, posv_ref, w1v_ref)
    vca = _dot(act_v, w2v_ref[...])
    vca_ref[...] = (vca + _ones_lane(vca.shape)).astype(BF16)


def nsa_compress(kc_raw, vc_raw, bsz, seq, pos_k, w1_k, w2_k, pos_v, w1_v, w2_v, k_gain, *, groups_per_step=4):
    g, dh = B_KV_HEADS, B_HEAD_DIM
    n_half = seq // CMP_STRIDE
    feat = CMP_STRIDE * dh

    def to_half_blocks(a):
        a = a.reshape(bsz, n_half, CMP_STRIDE, g, dh).transpose(0, 3, 1, 2, 4)
        return a.reshape(bsz * g * n_half, feat)

    def pos_rows(pos):
        return jnp.zeros((8, 2 * feat), F32).at[0].set(pos.reshape(-1)).astype(BF16)

    hid = w1_k.shape[1]
    w2v = jnp.pad(w2_v, ((0, 0), (0, LANES - dh))).astype(BF16)
    rows = groups_per_step * n_half
    const2 = lambda i: (0, 0)
    return pl.pallas_call(
        functools.partial(_compress_kernel, n_half=n_half),
        grid=(bsz * g // groups_per_step,),
        in_specs=[
            pl.BlockSpec((rows, feat), lambda i: (i, 0)),
            pl.BlockSpec((rows, feat), lambda i: (i, 0)),
            _resident((8, 2 * feat), const2),
            _resident((8, 2 * feat), const2),
            _resident((2 * feat, hid), const2),
            _resident((2 * feat, hid), const2),
            _resident((dh, hid), const2),
            _resident((hid, LANES), const2),
            _resident((dh, 1), const2),
        ],
        out_specs=[pl.BlockSpec((groups_per_step, dh, n_half), lambda i: (i, 0, 0)),
                   pl.BlockSpec((rows, LANES), lambda i: (i, 0))],
        out_shape=[jax.ShapeDtypeStruct((bsz * g, dh, n_half), BF16),
                   jax.ShapeDtypeStruct((bsz * g * n_half, LANES), BF16)],
        compiler_params=_cparams(1),
        name="nsa_compress",
    )(to_half_blocks(kc_raw), to_half_blocks(vc_raw), pos_rows(pos_k), pos_rows(pos_v),
      w1_k.astype(BF16), w1_v.astype(BF16), w2_k.T.astype(BF16), w2v, k_gain.reshape(dh, 1))


def _q_proj_kernel(x_ref, gn_ref, w_ref, bd_ref, qgain_ref, q_ref, gate_ref):
    d_q = q_ref.shape[1]
    xn = _rms(x_ref[...], gn_ref[...]).astype(BF16)
    proj = _dot(xn, w_ref[...])
    q = proj[:, :d_q]
    sq_hi, sq_lo = _split_bf16(q * q)
    ms = (_dot(sq_hi, bd_ref[...]) + _dot(sq_lo, bd_ref[...])) * (1.0 / B_HEAD_DIM)
    q_ref[...] = (q * lax.rsqrt(ms + NORM_EPS) * qgain_ref[...]).astype(BF16)
    gate_ref[...] = _sigmoid(proj[:, d_q:])


def nsa_q_proj(h2d, g_norm, w_in, q_norm, *, tm=512):
    t_total, d = h2d.shape
    g, hpg, dh = B_KV_HEADS, B_GROUP, B_HEAD_DIM
    d_q = B_HEADS * dh
    wg = w_in[:, d_q:].reshape(d, g, hpg * 3)
    wg = jnp.pad(wg, ((0, 0), (0, 0), (0, LANES - hpg * 3))).reshape(d, g * LANES)
    w = jnp.concatenate([w_in[:, :d_q], wg], axis=1).astype(BF16)
    head_of = np.arange(d_q) // dh
    bd = jnp.asarray(head_of[:, None] == head_of[None, :], BF16)
    qgain = (jnp.tile(q_norm, B_HEADS) * (dh ** -0.5)).reshape(1, d_q)
    const2 = lambda i: (0, 0)
    return pl.pallas_call(
        _q_proj_kernel,
        grid=(t_total // tm,),
        in_specs=[
            pl.BlockSpec((tm, d), lambda i: (i, 0)),
            _resident((1, d), const2),
            _resident((d, d_q + g * LANES), const2),
            _resident((d_q, d_q), const2),
            _resident((1, d_q), const2),
        ],
        out_specs=[pl.BlockSpec((tm, d_q), lambda i: (i, 0)), pl.BlockSpec((tm, g * LANES), lambda i: (i, 0))],
        out_shape=[jax.ShapeDtypeStruct((t_total, d_q), BF16), jax.ShapeDtypeStruct((t_total, g * LANES), F32)],
        compiler_params=_cparams(1),
        name="nsa_q_proj",
    )(h2d, g_norm.reshape(1, d), w, bd, qgain)


def _nsa_attn_kernel(q_ref, gate_ref, slope_ref, kct_ref, vca_ref, kst_ref, vs_ref, kwt_ref, vw_ref,
                     ovl_ref, esel_ref, o_ref, *, tq, tk):
    hpg, dh = B_GROUP, B_HEAD_DIM
    t0 = pl.program_id(2) * tq
    rows = hpg * tq
    qb = q_ref[...]
    q_st = jnp.concatenate([qb[:, i * dh:(i + 1) * dh] for i in range(hpg)], axis=0)
    slope = slope_ref[0]
    t_q = t0 + lax.broadcasted_iota(jnp.int32, (tq, 1), 0)
    t_st = jnp.concatenate([t_q] * hpg, axis=0)

    n_half = kct_ref.shape[2]
    c_end = lax.broadcasted_iota(jnp.int32, (1, n_half), 1) * CMP_STRIDE + (CMP_BLOCK - 1)
    dist = t_st - c_end
    seen = dist >= 0
    s = jnp.where(seen, _dot(q_st, kct_ref[0]) - slope * dist.astype(F32), NEG)
    m = jnp.max(s, axis=-1, keepdims=True)
    e = jnp.where(seen, jnp.exp(s - m), 0.0)
    p = e / jnp.maximum(jnp.sum(e, axis=-1, keepdims=True), TINY)
    o_cmp = _dot(p.astype(BF16), vca_ref[...])[:, :dh]

    p_grp = p[0:tq]
    for i in range(1, hpg):
        p_grp = p_grp + p[i * tq:(i + 1) * tq]
    p_hi, p_lo = _split_bf16(p_grp)
    imp = _dot(p_hi, ovl_ref[...]) + _dot(p_lo, ovl_ref[...])
    blk = lax.broadcasted_iota(jnp.int32, (tq, LANES), 1)
    valid = blk * SEL_BLOCK <= t_q
    cur = t_q // SEL_BLOCK
    forced = valid & ((blk == 0) | (blk == cur) | (blk == cur - 1))
    work = jnp.where(forced, SEL_BIG, jnp.where(valid, imp, -SEL_BIG))
    blk_f = blk.astype(F32)
    chosen = jnp.zeros((tq, LANES), F32)
    for _ in range(N_SELECT):
        top = jnp.max(work, axis=-1, keepdims=True)
        first = jnp.min(jnp.where(work == top, blk_f, float(LANES)), axis=-1, keepdims=True)
        hit = blk_f == first
        chosen = jnp.where(hit & (top > -0.5 * SEL_BIG), 1.0, chosen)
        work = jnp.where(hit, -3e38, work)
    chosen = chosen.astype(BF16)

    def sel_step(kt, carry):
        m_run, acc = carry
        k0 = pl.multiple_of(kt * tk, tk)
        pos = k0 + lax.broadcasted_iota(jnp.int32, (1, tk), 1)
        picked = _dot(chosen, esel_ref[:, pl.ds(k0, tk)])
        pen = jnp.where((picked > 0.5) & (pos <= t_q), 0.0, NEG)
        pen = jnp.concatenate([pen] * hpg, axis=0)
        sc = _dot(q_st, kst_ref[0, 0, :, pl.ds(k0, tk)]) - slope * (t_st - pos).astype(F32) + pen
        m_new = jnp.maximum(m_run, jnp.max(sc, axis=-1, keepdims=True))
        pr = jnp.exp(sc - m_new).astype(BF16)
        acc = jnp.exp(m_run - m_new) * acc + _dot(pr, vs_ref[0, 0, pl.ds(k0, tk), :])
        return m_new, acc

    n_kt = (t0 + tq + tk - 1) // tk
    _, acc = lax.fori_loop(0, n_kt, sel_step,
                           (jnp.full((rows, 1), NEG, F32), jnp.zeros((rows, LANES), F32)))
    o_sel = acc[:, :dh] / acc[:, ONES_LANE:ONES_LANE + 1]

    span = WINDOW + tq
    w0 = pl.multiple_of(jnp.maximum(t0 - WINDOW, 0), LANES)
    pos = w0 + lax.broadcasted_iota(jnp.int32, (1, span), 1)
    dist = t_st - pos
    inside = (dist >= 0) & (dist < WINDOW)
    s = jnp.where(inside, _dot(q_st, kwt_ref[0, 0, :, pl.ds(w0, span)]) - slope * dist.astype(F32), NEG)
    m = jnp.max(s, axis=-1, keepdims=True)
    pr = jnp.exp(s - m).astype(BF16)
    acc = _dot(pr, vw_ref[0, 0, pl.ds(w0, span), :])
    o_win = acc[:, :dh] / acc[:, ONES_LANE:ONES_LANE + 1]

    gate = gate_ref[...]
    outs = []
    for i in range(hpg):
        r = slice(i * tq, (i + 1) * tq)
        outs.append(gate[:, 3 * i:3 * i + 1] * o_cmp[r] + gate[:, 3 * i + 1:3 * i + 2] * o_sel[r]
                    + gate[:, 3 * i + 2:3 * i + 3] * o_win[r])
    o_ref[...] = jnp.concatenate(outs, axis=1).astype(BF16)


def nsa_attention(q, gates, kct, vca, kst, vs, kwt, vw, bsz, seq, *, tq=128, tk=512):
    g, hpg, dh = B_KV_HEADS, B_GROUP, B_HEAD_DIM
    n_q = seq // tq
    n_half = seq // CMP_STRIDE
    n_cmp = n_half - 1
    n_sel = seq // SEL_BLOCK
    slopes = np.array([2.0 ** (-8.0 * (h + 1) / B_HEADS) for h in range(B_HEADS)], dtype=np.float32)
    slope_rows = np.repeat(slopes.reshape(g, hpg), tq, axis=1).reshape(g, hpg * tq, 1)
    c_start = np.arange(n_half) * CMP_STRIDE
    j_sel = np.arange(LANES)
    ovl = ((c_start[:, None] < (j_sel[None, :] + 1) * SEL_BLOCK)
           & (c_start[:, None] + CMP_BLOCK > j_sel[None, :] * SEL_BLOCK)
           & (np.arange(n_half)[:, None] < n_cmp) & (j_sel[None, :] < n_sel))
    esel = (np.arange(seq)[None, :] // SEL_BLOCK) == j_sel[:, None]
    row = lambda b, j, i: (b * n_q + i, j)
    per_bg = lambda b, j, i: (b, j, 0, 0)
    return pl.pallas_call(
        functools.partial(_nsa_attn_kernel, tq=tq, tk=tk),
        grid=(bsz, g, n_q),
        in_specs=[
            pl.BlockSpec((tq, hpg * dh), row),
            pl.BlockSpec((tq, LANES), row),
            pl.BlockSpec((1, hpg * tq, 1), lambda b, j, i: (j, 0, 0)),
            pl.BlockSpec((1, dh, n_half), lambda b, j, i: (b * g + j, 0, 0)),
            pl.BlockSpec((n_half, LANES), lambda b, j, i: (b * g + j, 0)),
            pl.BlockSpec((1, 1, dh, seq), per_bg),
            pl.BlockSpec((1, 1, seq, LANES), per_bg),
            pl.BlockSpec((1, 1, dh, seq), per_bg),
            pl.BlockSpec((1, 1, seq, LANES), per_bg),
            _resident((n_half, LANES), lambda b, j, i: (0, 0)),
            _resident((LANES, seq), lambda b, j, i: (0, 0)),
        ],
        out_specs=pl.BlockSpec((tq, hpg * dh), row),
        out_shape=jax.ShapeDtypeStruct((bsz * seq, B_HEADS * dh), BF16),
        compiler_params=_cparams(3),
        name="nsa_attention",
    )(q, gates, jnp.asarray(slope_rows), kct, vca, kst, vs, kwt, vw, jnp.asarray(ovl, BF16), jnp.asarray(esel, BF16))


def _out_proj_kernel(h_ref, o_ref, w_ref, out_ref):
    out_ref[...] = h_ref[...] + _dot(o_ref[...], w_ref[...])


def out_proj_residual(h2d, o, w_out, *, tm=512):
    t_total, d = h2d.shape
    k = o.shape[1]
    return pl.pallas_call(
        _out_proj_kernel,
        grid=(t_total // tm,),
        in_specs=[pl.BlockSpec((tm, d), lambda i: (i, 0)), pl.BlockSpec((tm, k), lambda i: (i, 0)),
                  _resident((k, d), lambda i: (0, 0))],
        out_specs=pl.BlockSpec((tm, d), lambda i: (i, 0)),
        out_shape=jax.ShapeDtypeStruct((t_total, d), F32),
        compiler_params=_cparams(1),
        name="out_proj_residual",
    )(h2d, o, w_out.astype(BF16))


def nsa_shared_kv(h2d, bsz, seq, kv_norm, kv_w, cmp_pos_k, cmp_w1_k, cmp_w2_k, cmp_pos_v, cmp_w1_v, cmp_w2_v, k_norm):
    kst, kwt, vs, vw, kc_raw, vc_raw = nsa_kv_proj(h2d, bsz, seq, kv_norm, kv_w, k_norm)
    kct, vca = nsa_compress(kc_raw, vc_raw, bsz, seq, cmp_pos_k, cmp_w1_k, cmp_w2_k,
                            cmp_pos_v, cmp_w1_v, cmp_w2_v, k_norm[0])
    return kct, vca, kst, vs, kwt, vw


def nsa_mixer(h2d, bsz, seq, g_norm, w_in, w_out, q_norm, kv):
    q, gates = nsa_q_proj(h2d, g_norm, w_in, q_norm)
    o = nsa_attention(q, gates, *kv, bsz, seq)
    return out_proj_residual(h2d, o, w_out)


def kernel(x, norm_mix, norm_ffn, hgrn_w_in, hgrn_w_out, hgrn_g_norm, hgrn_lb_logits, kv_norm, kv_w, cmp_pos_k, cmp_w1_k, cmp_w2_k, cmp_pos_v, cmp_w1_v, cmp_w2_v, k_norm, nsa_w_in, nsa_w_out, nsa_q_norm, ffn_w_gu, ffn_w_down, moe_router, moe_w_gu, moe_w_down):
    bsz, seq, d = x.shape
    lb = jnp.cumsum(jax.nn.softmax(hgrn_lb_logits.astype(F32), axis=0), axis=0)
    lb = lb - lb[0:1]
    n_a = hgrn_w_in.shape[0]
    depth = norm_mix.shape[0]
    h = x.reshape(bsz * seq, d)
    kv = None
    for layer in range(depth):
        if layer < n_a:
            h = hgrn_mixer(h, seq, norm_mix[layer], hgrn_w_in[layer], hgrn_w_out[layer], hgrn_g_norm[layer],
                           lb[layer])
        else:
            if kv is None:
                kv = nsa_shared_kv(h, bsz, seq, kv_norm, kv_w, cmp_pos_k, cmp_w1_k, cmp_w2_k,
                                   cmp_pos_v, cmp_w1_v, cmp_w2_v, k_norm)
            j = layer - n_a
            h = nsa_mixer(h, bsz, seq, norm_mix[layer], nsa_w_in[j], nsa_w_out[j], nsa_q_norm[j], kv)
        if layer % 2 == 0:
            h = ffn_swiglu(h, norm_ffn[layer], ffn_w_gu[layer // 2], ffn_w_down[layer // 2])
        else:
            h = moe_swiglu_top2(h, norm_ffn[layer], moe_router[layer // 2], moe_w_gu[layer // 2],
                                moe_w_down[layer // 2])
    return h.reshape(bsz, seq, d)
```

```python
import functools

import numpy as np
import jax
import jax.numpy as jnp
from jax import lax
from jax.experimental import pallas as pl
from jax.experimental.pallas import tpu as pltpu

F32 = jnp.float32
BF16 = jnp.bfloat16

NORM_EPS = 1e-6
VMEM_LIMIT_BYTES = 56 * 1024 * 1024

A_HEADS = 8
A_HEAD_DIM = 128
SCAN_ROWS = 128


def _cparams(n_axes):
    return pltpu.CompilerParams(dimension_semantics=("arbitrary",) * n_axes,
                                vmem_limit_bytes=VMEM_LIMIT_BYTES)


def _dot(a, b):
    return jnp.dot(a, b, preferred_element_type=F32)


def _dot_nt(a, b):
    return lax.dot_general(a, b, (((1,), (1,)), ((), ())), preferred_element_type=F32)


def _dot_tn(a, b):
    return lax.dot_general(a, b, (((0,), (0,)), ((), ())), preferred_element_type=F32)


def _sigmoid(x):
    return 1.0 / (1.0 + jnp.exp(-x))


def _rms(x, g):
    return x * lax.rsqrt(jnp.mean(x * x, axis=-1, keepdims=True) + NORM_EPS) * g


def _split_bf16(x):
    hi = x.astype(BF16)
    lo = (x - hi.astype(F32)).astype(BF16)
    return hi, lo


def _hgrn_constants():
    n = SCAN_ROWS
    t = np.arange(n)
    r = np.arange(n)[None, :]
    sums, masks = [], []
    m = n // 2
    while m >= 1:
        grp = t // (2 * m)
        mid = grp * 2 * m + m
        upper = (t % (2 * m)) >= m
        sums.append(np.where(upper[:, None], (r >= mid[:, None]) & (r <= t[:, None]),
                             (r > t[:, None]) & (r < mid[:, None])))
        masks.append((grp[:, None] == grp[None, :]) & upper[:, None] & (~upper[None, :]))
        m //= 2
    masks.append(np.eye(n, dtype=bool))
    sums.append(r <= t[:, None])
    sums.append(r > t[:, None])
    w = np.concatenate(sums, 0).astype(np.float32)
    return np.concatenate([w, w], 1), np.stack(masks).astype(np.float32)


def _hgrn_kernel(x_ref, gn_ref, win_ref, lb_ref, ghn_ref, wout_ref, wsum_ref, msk_ref,
                 out_ref, xn_scr, st_scr, *, tm):
    sblk = pl.program_id(1)
    h = pl.program_id(2)
    n = SCAN_ROWS
    n_lvl = msk_ref.shape[0] - 1

    @pl.when(h == 0)
    def _():
        xn_scr[...] = _rms(x_ref[...], gn_ref[...]).astype(BF16)

    @pl.when(sblk == 0)
    def _():
        st_scr[h] = jnp.zeros((A_HEAD_DIM, A_HEAD_DIM), F32)

    proj = _dot(xn_scr[...], win_ref[...])
    q = proj[:, 0:128]
    f = proj[:, 128:256]
    v = proj[:, 256:384]
    g = proj[:, 384:512]
    lb = lb_ref[0:1, :]
    log_lb = lb_ref[1:2, :]
    log_1m_lb = lb_ref[2:3, :]
    qs = q * _sigmoid(q)
    log_sig = jnp.minimum(f, 0.0) - jnp.log1p(jnp.exp(-jnp.abs(f)))
    b = log_1m_lb + log_sig
    logf = jnp.maximum(log_lb, b) + jnp.log1p(jnp.exp(-jnp.abs(log_lb - b)))
    key = (1.0 - lb) * _sigmoid(-f)

    st = st_scr[h]
    outs = []
    for r in range(tm // n):
        rows = slice(r * n, (r + 1) * n)
        hi, lo = _split_bf16(logf[rows])
        e = jnp.exp(_dot(wsum_ref[...], jnp.concatenate([hi, lo], axis=0)))
        qb = qs[rows]
        kb = key[rows]
        vb = v[rows].astype(BF16)
        s = msk_ref[n_lvl] * _dot_nt(qb.astype(BF16), kb.astype(BF16))
        for l in range(n_lvl):
            el = e[l * n:(l + 1) * n]
            s = s + msk_ref[l] * _dot_nt((qb * el).astype(BF16), (kb * el).astype(BF16))
        e_prefix = e[n_lvl * n:(n_lvl + 1) * n]
        e_suffix = e[(n_lvl + 1) * n:(n_lvl + 2) * n]
        o = _dot(s.astype(BF16), vb) + _dot_nt((qb * e_prefix).astype(BF16), st.astype(BF16))
        st = st * e_prefix[n - 1:n, :] + _dot_tn(vb, (kb * e_suffix).astype(BF16))
        outs.append(o)
    st_scr[h] = st
    o = jnp.concatenate(outs, axis=0)

    og = (_rms(o, ghn_ref[...]) * (g * _sigmoid(g))).astype(BF16)
    contrib = _dot(og, wout_ref[...])

    @pl.when(h == 0)
    def _():
        out_ref[...] = x_ref[...] + contrib

    @pl.when(h > 0)
    def _():
        out_ref[...] += contrib


def hgrn_mixer(h2d, seq, g_norm_in, w_in, w_out, g_head, lb, *, tm=512):
    t_total, d = h2d.shape
    n_sblk = seq // tm
    hd = A_HEAD_DIM
    w_perm = w_in.reshape(d, 4, A_HEADS, hd).transpose(0, 2, 1, 3).reshape(d, 4 * A_HEADS * hd).astype(BF16)
    lb_rows = jnp.zeros((8, A_HEADS * hd), F32)
    lb_rows = lb_rows.at[0].set(lb).at[1].set(jnp.log(lb)).at[2].set(jnp.log1p(-lb))
    wsum, masks = _hgrn_constants()
    n_blk = wsum.shape[0]
    grid = (t_total // seq, n_sblk, A_HEADS)
    row = lambda b, s, h: (b * n_sblk + s, 0)
    const2 = lambda b, s, h: (0, 0)
    return pl.pallas_call(
        functools.partial(_hgrn_kernel, tm=tm),
        grid=grid,
        in_specs=[
            pl.BlockSpec((tm, d), row),
            pl.BlockSpec((1, d), const2),
            pl.BlockSpec((d, 4 * hd), lambda b, s, h: (0, h)),
            pl.BlockSpec((8, hd), lambda b, s, h: (0, h)),
            pl.BlockSpec((1, hd), const2),
            pl.BlockSpec((hd, d), lambda b, s, h: (h, 0)),
            pl.BlockSpec((n_blk, 2 * SCAN_ROWS), const2),
            pl.BlockSpec(masks.shape, lambda b, s, h: (0, 0, 0)),
        ],
        out_specs=pl.BlockSpec((tm, d), row),
        out_shape=jax.ShapeDtypeStruct((t_total, d), F32),
        scratch_shapes=[pltpu.VMEM((tm, d), BF16), pltpu.VMEM((A_HEADS, hd, hd), F32)],
        compiler_params=_cparams(3),
        name="hgrn_mixer",
    )(h2d, g_norm_in.reshape(1, d), w_perm, lb_rows, g_head.reshape(1, hd), w_out.astype(BF16),
      jnp.asarray(wsum, BF16), jnp.asarray(masks, F32))


def _resident(block_shape, index_map):
    return pl.BlockSpec(block_shape, index_map, pipeline_mode=pl.Buffered(1))


def _ffn_kernel(x_ref, gn_ref, wg_ref, wu_ref, wd_ref, out_ref, *, n_chunks):
    x = x_ref[...]
    xn = _rms(x, gn_ref[...]).astype(BF16)
    width = wg_ref.shape[1] // n_chunks
    acc = x
    for j in range(n_chunks):
        cols = slice(j * width, (j + 1) * width)
        gate = _dot(xn, wg_ref[:, cols])
        up = _dot(xn, wu_ref[:, cols])
        act = (gate * _sigmoid(gate) * up).astype(BF16)
        acc = acc + _dot(act, wd_ref[cols, :])
    out_ref[...] = acc


def ffn_swiglu(h2d, g_norm, w_gu, w_down, *, tm=512, n_chunks=2):
    t_total, d = h2d.shape
    ff = w_down.shape[0]
    w_gu = w_gu.astype(BF16)
    const2 = lambda i: (0, 0)
    return pl.pallas_call(
        functools.partial(_ffn_kernel, n_chunks=n_chunks),
        grid=(t_total // tm,),
        in_specs=[
            pl.BlockSpec((tm, d), lambda i: (i, 0)),
            _resident((1, d), const2),
            _resident((d, ff), const2),
            _resident((d, ff), lambda i: (0, 1)),
            _resident((ff, d), const2),
        ],
        out_specs=pl.BlockSpec((tm, d), lambda i: (i, 0)),
        out_shape=jax.ShapeDtypeStruct((t_total, d), F32),
        compiler_params=_cparams(1),
        name="ffn_swiglu",
    )(h2d, g_norm.reshape(1, d), w_gu, w_gu, w_down.astype(BF16))


N_EXPERTS = 8
LANES = 128
NEG = -1e30


def _router_kernel(x_ref, gn_ref, wh_ref, wl_ref, sel_ref):
    xn = _rms(x_ref[...], gn_ref[...])
    xh, xl = _split_bf16(xn)
    logits = _dot(xh, wh_ref[...]) + (_dot(xl, wh_ref[...]) + _dot(xh, wl_ref[...]))
    lane = lax.broadcasted_iota(jnp.int32, logits.shape, 1).astype(F32)
    logits = jnp.where(lane < N_EXPERTS, logits, NEG)
    m1 = jnp.max(logits, axis=-1, keepdims=True)
    i1 = jnp.min(jnp.where(logits == m1, lane, float(LANES)), axis=-1, keepdims=True)
    rest = jnp.where(lane == i1, NEG, logits)
    m2 = jnp.max(rest, axis=-1, keepdims=True)
    i2 = jnp.min(jnp.where(rest == m2, lane, float(LANES)), axis=-1, keepdims=True)
    e2 = jnp.exp(m2 - m1)
    denom = 1.0 + e2
    sel_ref[...] = jnp.where(lane == 0, i1, jnp.where(lane == 1, i2, jnp.where(
        lane == 2, 1.0 / denom, jnp.where(lane == 3, e2 / denom, 0.0))))


def moe_route(h2d, g_norm, w_router, *, tm=512):
    t_total, d = h2d.shape
    w_pad = jnp.zeros((d, LANES), F32).at[:, :N_EXPERTS].set(w_router)
    w_hi = w_pad.astype(BF16)
    w_lo = (w_pad - w_hi.astype(F32)).astype(BF16)
    const2 = lambda i: (0, 0)
    return pl.pallas_call(
        _router_kernel,
        grid=(t_total // tm,),
        in_specs=[
            pl.BlockSpec((tm, d), lambda i: (i, 0)),
            _resident((1, d), const2),
            _resident((d, LANES), const2),
            _resident((d, LANES), const2),
        ],
        out_specs=pl.BlockSpec((tm, LANES), lambda i: (i, 0)),
        out_shape=jax.ShapeDtypeStruct((t_total, LANES), F32),
        compiler_params=_cparams(1),
        name="moe_router",
    )(h2d, g_norm.reshape(1, d), w_hi, w_lo)


MOE_TILE = 1024
GATHER_UNROLL = 8


def _moe_plan(sel, tm):
    t_total = sel.shape[0]
    e = sel[:, :2].astype(jnp.int32)
    onehot = (e[:, :, None] == jnp.arange(N_EXPERTS, dtype=jnp.int32)[None, None, :]).astype(jnp.int32).sum(1)
    csum = jnp.cumsum(onehot, axis=0)
    rank = csum - onehot
    padded = (csum[-1] + tm - 1) // tm * tm
    ends = jnp.cumsum(padded)
    pos = (ends - padded)[e] + jnp.take_along_axis(rank, e, axis=1)
    n_slots = 2 * t_total + N_EXPERTS * tm
    tile_start = jnp.arange(n_slots // tm, dtype=jnp.int32) * tm
    tile_expert = jnp.minimum(jnp.searchsorted(ends, tile_start, side="right"), N_EXPERTS - 1).astype(jnp.int32)
    tile_used = (tile_start < ends[-1]).astype(jnp.int32)
    return n_slots, tile_expert, tile_used, pos[:, 0], pos[:, 1]


def _load_indices(idx_hbm, idx_smem, sem, tile, tm):
    return pltpu.make_async_copy(idx_hbm.at[pl.ds(tile * tm, tm)], idx_smem, sem)


def _moe_dispatch_kernel(pa_hbm, pb_hbm, h_ref, zeros_hbm, xs_hbm, ia_smem, ib_smem, row_sem, idx_sem, *, tm):
    del zeros_hbm
    i = pl.program_id(0)
    ca = _load_indices(pa_hbm, ia_smem, idx_sem.at[0], i, tm)
    cb = _load_indices(pb_hbm, ib_smem, idx_sem.at[1], i, tm)
    ca.start()
    cb.start()
    ca.wait()
    cb.wait()

    def body(r, carry):
        src = h_ref.at[pl.ds(r, 1), :]
        pltpu.make_async_copy(src, xs_hbm.at[pl.ds(ia_smem[r], 1), :], row_sem.at[0]).start()
        pltpu.make_async_copy(src, xs_hbm.at[pl.ds(ib_smem[r], 1), :], row_sem.at[1]).start()
        return carry
    lax.fori_loop(0, tm, body, 0, unroll=GATHER_UNROLL)
    pltpu.make_async_copy(h_ref, xs_hbm.at[pl.ds(0, tm), :], row_sem.at[0]).wait()
    pltpu.make_async_copy(h_ref, xs_hbm.at[pl.ds(0, tm), :], row_sem.at[1]).wait()


def _moe_group_kernel(te_ref, used_ref, xs_ref, gn_ref, wg_ref, wu_ref, wd_ref, y_ref, xn_scr):
    i = pl.program_id(0)
    j = pl.program_id(1)

    @pl.when(used_ref[i] == 0)
    def _():
        y_ref[...] = jnp.zeros(y_ref.shape, F32)

    @pl.when(used_ref[i] != 0)
    def _():
        @pl.when(j == 0)
        def _():
            xn_scr[...] = _rms(xs_ref[...], gn_ref[...]).astype(BF16)

        xn = xn_scr[...]
        gate = _dot(xn, wg_ref[0])
        up = _dot(xn, wu_ref[0])
        act = (gate * _sigmoid(gate) * up).astype(BF16)
        contrib = _dot(act, wd_ref[0])

        @pl.when(j == 0)
        def _():
            y_ref[...] = contrib

        @pl.when(j > 0)
        def _():
            y_ref[...] += contrib


def _moe_combine_kernel(pa_hbm, pb_hbm, y_hbm, h_ref, sel_ref, out_ref, ia_smem, ib_smem, buf_a, buf_b,
                        row_sem, idx_sem, *, tm):
    i = pl.program_id(0)
    ca = _load_indices(pa_hbm, ia_smem, idx_sem.at[0], i, tm)
    cb = _load_indices(pb_hbm, ib_smem, idx_sem.at[1], i, tm)
    ca.start()
    cb.start()
    ca.wait()
    cb.wait()

    def body(r, carry):
        pltpu.make_async_copy(y_hbm.at[pl.ds(ia_smem[r], 1), :], buf_a.at[pl.ds(r, 1), :], row_sem.at[0]).start()
        pltpu.make_async_copy(y_hbm.at[pl.ds(ib_smem[r], 1), :], buf_b.at[pl.ds(r, 1), :], row_sem.at[1]).start()
        return carry
    lax.fori_loop(0, tm, body, 0, unroll=GATHER_UNROLL)
    pltpu.make_async_copy(buf_a, buf_a, row_sem.at[0]).wait()
    pltpu.make_async_copy(buf_b, buf_b, row_sem.at[1]).wait()
    sel = sel_ref[...]
    out_ref[...] = h_ref[...] + (sel[:, 2:3] * buf_a[...] + sel[:, 3:4] * buf_b[...])


def moe_swiglu_top2(h2d, g_norm, w_router, w_gu, w_down, *, n_chunks=4):
    t_total, d = h2d.shape
    tm = MOE_TILE
    ff = w_down.shape[1]
    tf = ff // n_chunks
    sel = moe_route(h2d, g_norm, w_router)
    n_slots, tile_expert, tile_used, pos_a, pos_b = _moe_plan(sel, tm)
    any_spec = pl.BlockSpec(memory_space=pl.ANY)
    index_scratch = [pltpu.SMEM((tm,), jnp.int32), pltpu.SMEM((tm,), jnp.int32)]
    sems = [pltpu.SemaphoreType.DMA((2,)), pltpu.SemaphoreType.DMA((2,))]
    xs = pl.pallas_call(
        functools.partial(_moe_dispatch_kernel, tm=tm),
        grid=(t_total // tm,),
        in_specs=[any_spec, any_spec, pl.BlockSpec((tm, d), lambda i: (i, 0)), any_spec],
        out_specs=any_spec,
        out_shape=jax.ShapeDtypeStruct((n_slots, d), F32),
        scratch_shapes=index_scratch + sems,
        input_output_aliases={3: 0},
        compiler_params=_cparams(1),
        name="moe_dispatch",
    )(pos_a, pos_b, h2d, jnp.zeros((n_slots, d), F32))
    w_gu = w_gu.astype(BF16)
    row = lambda i, j, te, used: (i, 0)
    y = pl.pallas_call(
        _moe_group_kernel,
        grid_spec=pltpu.PrefetchScalarGridSpec(
            num_scalar_prefetch=2,
            grid=(n_slots // tm, n_chunks),
            in_specs=[
                pl.BlockSpec((tm, d), row),
                pl.BlockSpec((1, d), lambda i, j, te, used: (0, 0)),
                pl.BlockSpec((1, d, tf), lambda i, j, te, used: (te[i], 0, j)),
                pl.BlockSpec((1, d, tf), lambda i, j, te, used: (te[i], 0, j + n_chunks)),
                pl.BlockSpec((1, tf, d), lambda i, j, te, used: (te[i], j, 0)),
            ],
            out_specs=pl.BlockSpec((tm, d), row),
            scratch_shapes=[pltpu.VMEM((tm, d), BF16)],
        ),
        out_shape=jax.ShapeDtypeStruct((n_slots, d), F32),
        compiler_params=_cparams(2),
        name="moe_group_ffn",
    )(tile_expert, tile_used, xs, g_norm.reshape(1, d), w_gu, w_gu, w_down.astype(BF16))
    return pl.pallas_call(
        functools.partial(_moe_combine_kernel, tm=tm),
        grid=(t_total // tm,),
        in_specs=[any_spec, any_spec, any_spec, pl.BlockSpec((tm, d), lambda i: (i, 0)),
                  pl.BlockSpec((tm, LANES), lambda i: (i, 0))],
        out_specs=pl.BlockSpec((tm, d), lambda i: (i, 0)),
        out_shape=jax.ShapeDtypeStruct((t_total, d), F32),
        scratch_shapes=index_scratch + [pltpu.VMEM((tm, d), F32), pltpu.VMEM((tm, d), F32)] + sems,
        compiler_params=_cparams(1),
        name="moe_combine",
    )(pos_a, pos_b, y, h2d, sel)


B_HEADS = 16
B_KV_HEADS = 4
B_GROUP = 4
B_HEAD_DIM = 64
CMP_STRIDE = 16
CMP_BLOCK = 32
SEL_BLOCK = 32
N_SELECT = 8
WINDOW = 512
SEL_BIG = 1e9
TINY = 1e-30
N_KV_KINDS = 6
ONES_LANE = B_HEAD_DIM


def _group_rms_rows(x, gain_col, n_groups):
    r, c = x.shape
    x3 = x.reshape(n_groups, r // n_groups, c)
    y = x3 * lax.rsqrt(jnp.mean(x3 * x3, axis=1, keepdims=True) + NORM_EPS)
    return y.reshape(r, c) * gain_col


LOG2E = 1.4426950408889634
POS_RADIX = 64
N_SLOPE_PARTS = 3


def _alibi_key_part(pos, idx):
    hi = (pos // POS_RADIX * POS_RADIX).astype(F32)
    lo = (pos % POS_RADIX).astype(F32)
    return jnp.where(idx < N_SLOPE_PARTS, hi, jnp.where(idx < 2 * N_SLOPE_PARTS, lo, 0.0))


def _alibi_query_part(slopes):
    parts, rest = [], slopes.astype(F32)
    for _ in range(N_SLOPE_PARTS):
        p = rest.astype(BF16)
        parts.append(p)
        rest = rest - p.astype(F32)
    return jnp.stack(parts + parts, axis=-1)


def _ones_lane(shape):
    lane = lax.broadcasted_iota(jnp.int32, shape, len(shape) - 1)
    return jnp.where(lane % LANES == ONES_LANE, 1.0, 0.0)


def _kv_proj_kernel(x_ref, gn_ref, wkt_ref, kgain_ref, wv_ref, wc_ref,
                    kst_ref, kwt_ref, vs_ref, vw_ref, kc_ref, vc_ref):
    g = B_KV_HEADS
    hn = _rms(x_ref[...], gn_ref[...]).astype(BF16)
    kt = _dot_nt(wkt_ref[...], hn)
    kt = _group_rms_rows(kt, kgain_ref[...], 2 * g)
    half = g * B_HEAD_DIM
    tm = kt.shape[1]
    pos = pl.program_id(1) * tm + lax.broadcasted_iota(jnp.int32, (LANES - B_HEAD_DIM, tm), 1)
    pos_rows = _alibi_key_part(pos, lax.broadcasted_iota(jnp.int32, pos.shape, 0))
    for j in range(g):
        kst_ref[0, j] = jnp.concatenate([kt[j * B_HEAD_DIM:(j + 1) * B_HEAD_DIM], pos_rows], axis=0).astype(BF16)
        kwt_ref[0, j] = jnp.concatenate([kt[half + j * B_HEAD_DIM:half + (j + 1) * B_HEAD_DIM], pos_rows],
                                        axis=0).astype(BF16)
    v = _dot(hn, wv_ref[...])
    v = (v + _ones_lane(v.shape)).astype(BF16)
    for j in range(g):
        vs_ref[0, j] = v[:, j * LANES:(j + 1) * LANES]
        vw_ref[0, j] = v[:, (g + j) * LANES:(g + j + 1) * LANES]
    c = _dot(hn, wc_ref[...])
    kc_ref[...] = c[:, :half].astype(BF16)
    vc_ref[...] = c[:, half:].astype(BF16)


def nsa_kv_proj(h2d, bsz, seq, kv_norm, kv_w, k_norm, *, tm=512):
    t_total, d = h2d.shape
    g, dh = B_KV_HEADS, B_HEAD_DIM
    n_sblk = seq // tm
    w = kv_w.reshape(d, N_KV_KINDS, g, dh)
    wkt = jnp.concatenate([w[:, 2].reshape(d, g * dh), w[:, 4].reshape(d, g * dh)], axis=1).T.astype(BF16)
    kgain = jnp.concatenate([jnp.tile(k_norm[1], g), jnp.tile(k_norm[2], g)]).reshape(2 * g * dh, 1)
    wv = jnp.stack([w[:, 3], w[:, 5]], axis=1)
    wv = jnp.pad(wv, ((0, 0), (0, 0), (0, 0), (0, LANES - dh))).reshape(d, 2 * g * LANES).astype(BF16)
    wc = jnp.concatenate([w[:, 0].reshape(d, g * dh), w[:, 1].reshape(d, g * dh)], axis=1).astype(BF16)
    const2 = lambda b, s: (0, 0)
    row = lambda b, s: (b * n_sblk + s, 0)
    kt_spec = pl.BlockSpec((1, g, LANES, tm), lambda b, s: (b, 0, 0, s))
    v_spec = pl.BlockSpec((1, g, tm, LANES), lambda b, s: (b, 0, s, 0))
    kt_shape = jax.ShapeDtypeStruct((bsz, g, LANES, seq), BF16)
    v_shape = jax.ShapeDtypeStruct((bsz, g, seq, LANES), BF16)
    c_shape = jax.ShapeDtypeStruct((t_total, g * dh), BF16)
    return pl.pallas_call(
        _kv_proj_kernel,
        grid=(bsz, n_sblk),
        in_specs=[
            pl.BlockSpec((tm, d), row),
            _resident((1, d), const2),
            _resident((2 * g * dh, d), const2),
            _resident((2 * g * dh, 1), const2),
            _resident((d, 2 * g * LANES), const2),
            _resident((d, 2 * g * dh), const2),
        ],
        out_specs=[kt_spec, kt_spec, v_spec, v_spec,
                   pl.BlockSpec((tm, g * dh), row), pl.BlockSpec((tm, g * dh), row)],
        out_shape=[kt_shape, kt_shape, v_shape, v_shape, c_shape, c_shape],
        compiler_params=_cparams(2),
        name="nsa_kv_proj",
    )(h2d, kv_norm.reshape(1, d), wkt, kgain, wv, wc)


def _compress_kernel(ak_ref, av_ref, posk_ref, posv_ref, w1k_ref, w1v_ref, w2k_ref, w2v_ref, kgain_ref,
                     kca_ref, vca_ref, *, n_half):
    half_w = ak_ref.shape[1]

    def hidden(a_ref, pos_ref, w1_ref):
        a = a_ref[...]
        top = _dot(a, w1_ref[:half_w, :])
        bot = _dot(a, w1_ref[half_w:, :])
        bias = _dot(pos_ref[...], w1_ref[...])[0:1, :]
        pre = top + pltpu.roll(bot, shift=a.shape[0] - 1, axis=0) + bias
        return (pre * _sigmoid(pre)).astype(BF16)

    act_k = hidden(ak_ref, posk_ref, w1k_ref)
    kc = _dot(act_k, w2k_ref[...])
    ms = jnp.sum(kc * kc, axis=-1, keepdims=True) * (1.0 / B_HEAD_DIM)
    kc = kc * lax.rsqrt(ms + NORM_EPS) * kgain_ref[...]
    blk = lax.broadcasted_iota(jnp.int32, kc.shape, 0) % n_half
    lane = lax.broadcasted_iota(jnp.int32, kc.shape, 1)
    c_end = blk * CMP_STRIDE + (CMP_BLOCK - 1)
    kca_ref[...] = (kc + jnp.where(lane >= B_HEAD_DIM, _alibi_key_part(c_end, lane - B_HEAD_DIM), 0.0)).astype(BF16)
    act_v = hidden(av_ref, posv_ref, w1v_ref)
    vca = _dot(act_v, w2v_ref[...])
    vca_ref[...] = (vca + _ones_lane(vca.shape)).astype(BF16)


def nsa_compress(kc_raw, vc_raw, bsz, seq, pos_k, w1_k, w2_k, pos_v, w1_v, w2_v, k_gain, *, groups_per_step=4):
    g, dh = B_KV_HEADS, B_HEAD_DIM
    n_half = seq // CMP_STRIDE
    feat = CMP_STRIDE * dh

    def to_half_blocks(a):
        a = a.reshape(bsz, n_half, CMP_STRIDE, g, dh).transpose(0, 3, 1, 2, 4)
        return a.reshape(bsz * g * n_half, feat)

    def pos_rows(pos):
        return jnp.zeros((8, 2 * feat), F32).at[0].set(pos.reshape(-1)).astype(BF16)

    hid = w1_k.shape[1]
    pad_lanes = lambda a: jnp.pad(a, ((0, 0), (0, LANES - dh)))
    rows = groups_per_step * n_half
    const2 = lambda i: (0, 0)
    return pl.pallas_call(
        functools.partial(_compress_kernel, n_half=n_half),
        grid=(bsz * g // groups_per_step,),
        in_specs=[
            pl.BlockSpec((rows, feat), lambda i: (i, 0)),
            pl.BlockSpec((rows, feat), lambda i: (i, 0)),
            _resident((8, 2 * feat), const2),
            _resident((8, 2 * feat), const2),
            _resident((2 * feat, hid), const2),
            _resident((2 * feat, hid), const2),
            _resident((hid, LANES), const2),
            _resident((hid, LANES), const2),
            _resident((1, LANES), const2),
        ],
        out_specs=[pl.BlockSpec((rows, LANES), lambda i: (i, 0)),
                   pl.BlockSpec((rows, LANES), lambda i: (i, 0))],
        out_shape=[jax.ShapeDtypeStruct((bsz * g * n_half, LANES), BF16),
                   jax.ShapeDtypeStruct((bsz * g * n_half, LANES), BF16)],
        compiler_params=_cparams(1),
        name="nsa_compress",
    )(to_half_blocks(kc_raw), to_half_blocks(vc_raw), pos_rows(pos_k), pos_rows(pos_v),
      w1_k.astype(BF16), w1_v.astype(BF16), pad_lanes(w2_k).astype(BF16), pad_lanes(w2_v).astype(BF16),
      pad_lanes(k_gain.reshape(1, dh)))


def _q_proj_kernel(x_ref, gn_ref, w_ref, qgain_ref, qconst_ref, q_ref, gate_ref):
    d_q = q_ref.shape[1]
    xn = _rms(x_ref[...], gn_ref[...]).astype(BF16)
    proj = _dot(xn, w_ref[...])
    ones = jnp.ones((LANES, LANES), BF16)
    for h in range(d_q // LANES):
        lanes = slice(h * LANES, (h + 1) * LANES)
        q = proj[:, lanes]
        sq_hi, sq_lo = _split_bf16(q * q)
        ms = (_dot(sq_hi, ones) + _dot(sq_lo, ones)) * (1.0 / B_HEAD_DIM)
        q_ref[:, lanes] = (q * lax.rsqrt(ms + NORM_EPS) * qgain_ref[:, lanes] + qconst_ref[:, lanes]).astype(BF16)
    gate_ref[...] = _sigmoid(proj[:, d_q:])


def nsa_q_proj(h2d, g_norm, w_in, q_norm, *, tm=512):
    t_total, d = h2d.shape
    g, hpg, dh = B_KV_HEADS, B_GROUP, B_HEAD_DIM
    d_q = B_HEADS * LANES
    wq = jnp.pad(w_in[:, :B_HEADS * dh].reshape(d, B_HEADS, dh), ((0, 0), (0, 0), (0, LANES - dh))).reshape(d, d_q)
    wg = w_in[:, B_HEADS * dh:].reshape(d, g, hpg * 3)
    wg = jnp.pad(wg, ((0, 0), (0, 0), (0, LANES - hpg * 3))).reshape(d, g * LANES)
    w = jnp.concatenate([wq, wg], axis=1).astype(BF16)
    qgain = jnp.tile(jnp.pad(q_norm * (dh ** -0.5 * LOG2E), (0, LANES - dh)), B_HEADS).reshape(1, d_q)
    slopes = np.array([2.0 ** (-8.0 * (h + 1) / B_HEADS) for h in range(B_HEADS)], dtype=np.float64) * LOG2E
    feats = _alibi_query_part(jnp.asarray(slopes, F32)).astype(F32)
    qconst = jnp.pad(feats, ((0, 0), (dh, LANES - dh - feats.shape[1]))).reshape(1, d_q)
    const2 = lambda i: (0, 0)
    return pl.pallas_call(
        _q_proj_kernel,
        grid=(t_total // tm,),
        in_specs=[
            pl.BlockSpec((tm, d), lambda i: (i, 0)),
            _resident((1, d), const2),
            _resident((d, d_q + g * LANES), const2),
            _resident((1, d_q), const2),
            _resident((1, d_q), const2),
        ],
        out_specs=[pl.BlockSpec((tm, d_q), lambda i: (i, 0)), pl.BlockSpec((tm, g * LANES), lambda i: (i, 0))],
        out_shape=[jax.ShapeDtypeStruct((t_total, d_q), BF16), jax.ShapeDtypeStruct((t_total, g * LANES), F32)],
        compiler_params=_cparams(1),
        name="nsa_q_proj",
    )(h2d, g_norm.reshape(1, d), w, qgain, qconst)


def _nsa_attn_kernel(q_ref, gate_ref, kca_ref, vca_ref, kst_ref, vs_ref, kwt_ref, vw_ref,
                     ovlt_ref, esel_ref, o_ref, *, tq, tk):
    hpg, dh = B_GROUP, B_HEAD_DIM
    t0 = pl.program_id(2) * tq
    rows = hpg * tq
    qb = q_ref[...]
    q_st = jnp.concatenate([qb[:, i * LANES:(i + 1) * LANES] for i in range(hpg)], axis=0)
    t_q = t0 + lax.broadcasted_iota(jnp.int32, (tq, 1), 0)

    n_half = kca_ref.shape[0]
    c_end = lax.broadcasted_iota(jnp.int32, (n_half, 1), 0) * CMP_STRIDE + (CMP_BLOCK - 1)
    t_lane = t0 + lax.broadcasted_iota(jnp.int32, (1, rows), 1) % tq
    seen = t_lane >= c_end
    s = jnp.where(seen, _dot_nt(kca_ref[...], q_st), NEG)
    m = jnp.max(s, axis=0, keepdims=True)
    e = jnp.where(seen, jnp.exp2(s - m), 0.0)
    p = e / jnp.maximum(jnp.sum(e, axis=0, keepdims=True), TINY)
    o_cmp = _dot_tn(p.astype(BF16), vca_ref[...])[:, :dh]

    p_grp = p[:, 0:tq]
    for i in range(1, hpg):
        p_grp = p_grp + p[:, i * tq:(i + 1) * tq]
    p_hi, p_lo = _split_bf16(p_grp)
    n_blk = LANES // 2
    imp = (_dot(ovlt_ref[...], p_hi) + _dot(ovlt_ref[...], p_lo))[:n_blk]
    blk = lax.broadcasted_iota(jnp.int32, (n_blk, tq), 0)
    t_l = t0 + lax.broadcasted_iota(jnp.int32, (1, tq), 1)
    valid = blk * SEL_BLOCK <= t_l
    cur = t_l // SEL_BLOCK
    forced = valid & ((blk == 0) | (blk == cur) | (blk == cur - 1))
    work = jnp.where(forced, SEL_BIG, jnp.where(valid, imp, -SEL_BIG))
    blk_f = blk.astype(F32)
    chosen = jnp.zeros((n_blk, tq), F32)
    for _ in range(N_SELECT):
        top = jnp.max(work, axis=0, keepdims=True)
        first = jnp.min(jnp.where(work == top, blk_f, float(LANES)), axis=0, keepdims=True)
        hit = blk_f == first
        chosen = jnp.where(hit & (top > -0.5 * SEL_BIG), 1.0, chosen)
        work = jnp.where(hit, -3e38, work)
    chosen = jnp.concatenate([chosen, jnp.zeros((LANES - n_blk, tq), F32)], axis=0).T.astype(BF16)

    span = WINDOW + tq
    w0 = pl.multiple_of(jnp.maximum(t0 - WINDOW, 0), LANES)
    dist = t_q - (w0 + lax.broadcasted_iota(jnp.int32, (1, span), 1))
    pen = jnp.where((dist >= 0) & (dist < WINDOW), 0.0, NEG)
    s = _dot(q_st, kwt_ref[0, 0, :, pl.ds(w0, span)]) + jnp.concatenate([pen] * hpg, axis=0)
    m = jnp.max(s, axis=-1, keepdims=True)
    acc = _dot(jnp.exp2(s - m).astype(BF16), vw_ref[0, 0, pl.ds(w0, span), :])
    o_win = acc[:, :dh] / acc[:, ONES_LANE:ONES_LANE + 1]

    def sel_step(kt, carry):
        m_run, acc = carry
        k0 = pl.multiple_of(kt * tk, tk)
        pos = k0 + lax.broadcasted_iota(jnp.int32, (1, tk), 1)
        picked = _dot(chosen, esel_ref[:, pl.ds(k0, tk)])
        pen = jnp.where((picked > 0.5) & (pos <= t_q), 0.0, NEG)
        sc = _dot(q_st, kst_ref[0, 0, :, pl.ds(k0, tk)]) + jnp.concatenate([pen] * hpg, axis=0)
        m_new = jnp.maximum(m_run, jnp.max(sc, axis=-1, keepdims=True))
        pr = jnp.exp2(sc - m_new).astype(BF16)
        acc = jnp.exp2(m_run - m_new) * acc + _dot(pr, vs_ref[0, 0, pl.ds(k0, tk), :])
        return m_new, acc

    n_kt = (t0 + tq + tk - 1) // tk
    _, acc = lax.fori_loop(0, n_kt, sel_step,
                           (jnp.full((rows, 1), NEG, F32), jnp.zeros((rows, LANES), F32)))
    o_sel = acc[:, :dh] / acc[:, ONES_LANE:ONES_LANE + 1]

    gate = gate_ref[...]
    outs = []
    for i in range(hpg):
        r = slice(i * tq, (i + 1) * tq)
        outs.append(gate[:, 3 * i:3 * i + 1] * o_cmp[r] + gate[:, 3 * i + 1:3 * i + 2] * o_sel[r]
                    + gate[:, 3 * i + 2:3 * i + 3] * o_win[r])
    o_ref[...] = jnp.concatenate(outs, axis=1).astype(BF16)


def nsa_attention(q, gates, kca, vca, kst, vs, kwt, vw, bsz, seq, *, tq=128, tk=512):
    g, hpg, dh = B_KV_HEADS, B_GROUP, B_HEAD_DIM
    n_q = seq // tq
    n_half = seq // CMP_STRIDE
    n_cmp = n_half - 1
    n_sel = seq // SEL_BLOCK
    assert n_sel <= LANES // 2 and tq == LANES
    c_start = np.arange(n_half) * CMP_STRIDE
    j_sel = np.arange(LANES)
    ovl = ((c_start[:, None] < (j_sel[None, :] + 1) * SEL_BLOCK)
           & (c_start[:, None] + CMP_BLOCK > j_sel[None, :] * SEL_BLOCK)
           & (np.arange(n_half)[:, None] < n_cmp) & (j_sel[None, :] < n_sel))
    esel = (np.arange(seq)[None, :] // SEL_BLOCK) == j_sel[:, None]
    row = lambda b, j, i: (b * n_q + i, j)
    per_bg = lambda b, j, i: (b, j, 0, 0)
    return pl.pallas_call(
        functools.partial(_nsa_attn_kernel, tq=tq, tk=tk),
        grid=(bsz, g, n_q),
        in_specs=[
            pl.BlockSpec((tq, hpg * LANES), row),
            pl.BlockSpec((tq, LANES), row),
            pl.BlockSpec((n_half, LANES), lambda b, j, i: (b * g + j, 0)),
            pl.BlockSpec((n_half, LANES), lambda b, j, i: (b * g + j, 0)),
            pl.BlockSpec((1, 1, LANES, seq), per_bg),
            pl.BlockSpec((1, 1, seq, LANES), per_bg),
            pl.BlockSpec((1, 1, LANES, seq), per_bg),
            pl.BlockSpec((1, 1, seq, LANES), per_bg),
            _resident((LANES, n_half), lambda b, j, i: (0, 0)),
            _resident((LANES, seq), lambda b, j, i: (0, 0)),
        ],
        out_specs=pl.BlockSpec((tq, hpg * dh), row),
        out_shape=jax.ShapeDtypeStruct((bsz * seq, B_HEADS * dh), BF16),
        compiler_params=_cparams(3),
        name="nsa_attention",
    )(q, gates, kca, vca, kst, vs, kwt, vw, jnp.asarray(ovl.T, BF16), jnp.asarray(esel, BF16))


def _out_proj_kernel(h_ref, o_ref, w_ref, out_ref):
    out_ref[...] = h_ref[...] + _dot(o_ref[...], w_ref[...])


def out_proj_residual(h2d, o, w_out, *, tm=512):
    t_total, d = h2d.shape
    k = o.shape[1]
    return pl.pallas_call(
        _out_proj_kernel,
        grid=(t_total // tm,),
        in_specs=[pl.BlockSpec((tm, d), lambda i: (i, 0)), pl.BlockSpec((tm, k), lambda i: (i, 0)),
                  _resident((k, d), lambda i: (0, 0))],
        out_specs=pl.BlockSpec((tm, d), lambda i: (i, 0)),
        out_shape=jax.ShapeDtypeStruct((t_total, d), F32),
        compiler_params=_cparams(1),
        name="out_proj_residual",
    )(h2d, o, w_out.astype(BF16))


def nsa_shared_kv(h2d, bsz, seq, kv_norm, kv_w, cmp_pos_k, cmp_w1_k, cmp_w2_k, cmp_pos_v, cmp_w1_v, cmp_w2_v, k_norm):
    kst, kwt, vs, vw, kc_raw, vc_raw = nsa_kv_proj(h2d, bsz, seq, kv_norm, kv_w, k_norm)
    kct, vca = nsa_compress(kc_raw, vc_raw, bsz, seq, cmp_pos_k, cmp_w1_k, cmp_w2_k,
                            cmp_pos_v, cmp_w1_v, cmp_w2_v, k_norm[0])
    return kct, vca, kst, vs, kwt, vw


def nsa_mixer(h2d, bsz, seq, g_norm, w_in, w_out, q_norm, kv):
    q, gates = nsa_q_proj(h2d, g_norm, w_in, q_norm)
    o = nsa_attention(q, gates, *kv, bsz, seq)
    return out_proj_residual(h2d, o, w_out)


def kernel(x, norm_mix, norm_ffn, hgrn_w_in, hgrn_w_out, hgrn_g_norm, hgrn_lb_logits, kv_norm, kv_w, cmp_pos_k, cmp_w1_k, cmp_w2_k, cmp_pos_v, cmp_w1_v, cmp_w2_v, k_norm, nsa_w_in, nsa_w_out, nsa_q_norm, ffn_w_gu, ffn_w_down, moe_router, moe_w_gu, moe_w_down):
    bsz, seq, d = x.shape
    lb = jnp.cumsum(jax.nn.softmax(hgrn_lb_logits.astype(F32), axis=0), axis=0)
    lb = lb - lb[0:1]
    n_a = hgrn_w_in.shape[0]
    depth = norm_mix.shape[0]
    h = x.reshape(bsz * seq, d)
    kv = None
    for layer in range(depth):
        if layer < n_a:
            h = hgrn_mixer(h, seq, norm_mix[layer], hgrn_w_in[layer], hgrn_w_out[layer], hgrn_g_norm[layer],
                           lb[layer])
        else:
            if kv is None:
                kv = nsa_shared_kv(h, bsz, seq, kv_norm, kv_w, cmp_pos_k, cmp_w1_k, cmp_w2_k,
                                   cmp_pos_v, cmp_w1_v, cmp_w2_v, k_norm)
            j = layer - n_a
            h = nsa_mixer(h, bsz, seq, norm_mix[layer], nsa_w_in[j], nsa_w_out[j], nsa_q_norm[j], kv)
        if layer % 2 == 0:
            h = ffn_swiglu(h, norm_ffn[layer], ffn_w_gu[layer // 2], ffn_w_down[layer // 2])
        else:
            h = moe_swiglu_top2(h, norm_ffn[layer], moe_router[layer // 2], moe_w_gu[layer // 2],
                                moe_w_down[layer // 2])
    return h.reshape(bsz, seq, d)
```

```python
import functools

import numpy as np
import jax
import jax.numpy as jnp
from jax import lax
from jax.experimental import pallas as pl
from jax.experimental.pallas import tpu as pltpu

F32 = jnp.float32
BF16 = jnp.bfloat16

NORM_EPS = 1e-6
VMEM_LIMIT_BYTES = 56 * 1024 * 1024

A_HEADS = 8
A_HEAD_DIM = 128
SCAN_ROWS = 128
SUBLANES = 8


def _cparams(n_axes):
    return pltpu.CompilerParams(dimension_semantics=("arbitrary",) * n_axes,
                                vmem_limit_bytes=VMEM_LIMIT_BYTES)


def _dot(a, b):
    return jnp.dot(a, b, preferred_element_type=F32)


def _dot_nt(a, b):
    return lax.dot_general(a, b, (((1,), (1,)), ((), ())), preferred_element_type=F32)


def _dot_tn(a, b):
    return lax.dot_general(a, b, (((0,), (0,)), ((), ())), preferred_element_type=F32)


def _sigmoid(x):
    return 1.0 / (1.0 + jnp.exp(-x))


def _rms(x, g):
    return x * lax.rsqrt(jnp.mean(x * x, axis=-1, keepdims=True) + NORM_EPS) * g


def _split_bf16(x):
    hi = x.astype(BF16)
    lo = (x - hi.astype(F32)).astype(BF16)
    return hi, lo


def _hgrn_constants():
    n = SCAN_ROWS
    t = np.arange(n)
    r = np.arange(n)[None, :]
    sums, masks = [], []
    m = n // 2
    while m >= 1:
        grp = t // (2 * m)
        mid = grp * 2 * m + m
        upper = (t % (2 * m)) >= m
        if m < SUBLANES:
            sums.append(np.where(upper[:, None], (r >= mid[:, None]) & (r <= t[:, None]),
                                 (r > t[:, None]) & (r < mid[:, None])))
        masks.append((grp[:, None] == grp[None, :]) & upper[:, None] & (~upper[None, :]))
        m //= 2
    masks.append(np.eye(n, dtype=bool))
    sums.append(r <= t[:, None])
    w = np.concatenate(sums, 0).astype(np.float32)
    return np.concatenate([w, w], 1), np.stack(masks).astype(np.float32)


def _hgrn_kernel(x_ref, gn_ref, win_ref, lb_ref, ghn_ref, wout_ref, wsum_ref, msk_ref,
                 out_ref, xn_scr, st_scr, og_scr, *, tm):
    sblk = pl.program_id(1)
    h = pl.program_id(2)
    n = SCAN_ROWS
    n_lvl = msk_ref.shape[0] - 1

    @pl.when(h == 0)
    def _():
        xn_scr[...] = _rms(x_ref[...], gn_ref[...]).astype(BF16)

    @pl.when(sblk == 0)
    def _():
        st_scr[h] = jnp.zeros((A_HEAD_DIM, A_HEAD_DIM), F32)

    proj = _dot(xn_scr[...], win_ref[...])
    q = proj[:, 0:128]
    f = proj[:, 128:256]
    v = proj[:, 256:384]
    g = proj[:, 384:512]
    lb = lb_ref[0:1, :]
    log_lb = lb_ref[1:2, :]
    log_1m_lb = lb_ref[2:3, :]
    qs = q * _sigmoid(q)
    log_sig = jnp.minimum(f, 0.0) - jnp.log1p(jnp.exp(-jnp.abs(f)))
    b = log_1m_lb + log_sig
    logf = jnp.maximum(log_lb, b) + jnp.log1p(jnp.exp(-jnp.abs(log_lb - b)))
    key = (1.0 - lb) * _sigmoid(-f)

    wide = [n >> (l + 1) for l in range(n_lvl) if n >> (l + 1) >= SUBLANES]
    row_id = lax.broadcasted_iota(jnp.int32, (n, A_HEAD_DIM), 0)
    in_upper_half = [(row_id % (2 * m)) >= m for m in wide]

    st = st_scr[h]
    outs = []
    for r in range(tm // n):
        rows = slice(r * n, (r + 1) * n)
        hi, lo = _split_bf16(logf[rows])
        sums = _dot(wsum_ref[...], jnp.concatenate([hi, lo], axis=0))
        cum = sums[(n_lvl - len(wide)) * n:]
        e_lvl = []
        for m, upper in zip(wide, in_upper_half):
            bound = jnp.concatenate([jnp.broadcast_to(cum[i + m - 1:i + m], (2 * m, A_HEAD_DIM))
                                     for i in range(0, n, 2 * m)], axis=0)
            e_lvl.append(jnp.exp(jnp.where(upper, cum - bound, bound - cum)))
        e_sums = jnp.exp(sums)
        e_lvl += [e_sums[i * n:(i + 1) * n] for i in range(n_lvl - len(wide))]
        e_prefix = e_sums[(n_lvl - len(wide)) * n:]
        e_suffix = jnp.exp(cum[n - 1:n] - cum)
        qb = qs[rows]
        kb = key[rows]
        vb = v[rows].astype(BF16)
        s = msk_ref[n_lvl] * _dot_nt(qb.astype(BF16), kb.astype(BF16))
        for l in range(n_lvl):
            s = s + msk_ref[l] * _dot_nt((qb * e_lvl[l]).astype(BF16), (kb * e_lvl[l]).astype(BF16))
        o = _dot(s.astype(BF16), vb) + _dot_nt((qb * e_prefix).astype(BF16), st.astype(BF16))
        st = st * e_prefix[n - 1:n, :] + _dot_tn(vb, (kb * e_suffix).astype(BF16))
        outs.append(o)
    st_scr[h] = st
    o = jnp.concatenate(outs, axis=0)

    og_scr[:, pl.ds(pl.multiple_of(h * A_HEAD_DIM, A_HEAD_DIM), A_HEAD_DIM)] = (
        _rms(o, ghn_ref[...]) * (g * _sigmoid(g))).astype(BF16)

    @pl.when(h == A_HEADS - 1)
    def _():
        out_ref[...] = x_ref[...] + _dot(og_scr[...], wout_ref[...])


def hgrn_mixer(h2d, seq, g_norm_in, w_in, w_out, g_head, lb, *, tm=512):
    t_total, d = h2d.shape
    n_sblk = seq // tm
    hd = A_HEAD_DIM
    w_perm = w_in.reshape(d, 4, A_HEADS, hd).transpose(0, 2, 1, 3).reshape(d, 4 * A_HEADS * hd).astype(BF16)
    lb_rows = jnp.zeros((8, A_HEADS * hd), F32)
    lb_rows = lb_rows.at[0].set(lb).at[1].set(jnp.log(lb)).at[2].set(jnp.log1p(-lb))
    wsum, masks = _hgrn_constants()
    n_blk = wsum.shape[0]
    grid = (t_total // seq, n_sblk, A_HEADS)
    row = lambda b, s, h: (b * n_sblk + s, 0)
    const2 = lambda b, s, h: (0, 0)
    return pl.pallas_call(
        functools.partial(_hgrn_kernel, tm=tm),
        grid=grid,
        in_specs=[
            pl.BlockSpec((tm, d), row),
            pl.BlockSpec((1, d), const2),
            pl.BlockSpec((d, 4 * hd), lambda b, s, h: (0, h)),
            pl.BlockSpec((8, hd), lambda b, s, h: (0, h)),
            pl.BlockSpec((1, hd), const2),
            _resident((A_HEADS * hd, d), const2),
            _resident((n_blk, 2 * SCAN_ROWS), const2),
            _resident(masks.shape, lambda b, s, h: (0, 0, 0)),
        ],
        out_specs=pl.BlockSpec((tm, d), row),
        out_shape=jax.ShapeDtypeStruct((t_total, d), F32),
        scratch_shapes=[pltpu.VMEM((tm, d), BF16), pltpu.VMEM((A_HEADS, hd, hd), F32),
                        pltpu.VMEM((tm, A_HEADS * hd), BF16)],
        compiler_params=_cparams(3),
        name="hgrn_mixer",
    )(h2d, g_norm_in.reshape(1, d), w_perm, lb_rows, g_head.reshape(1, hd), w_out.astype(BF16),
      jnp.asarray(wsum, BF16), jnp.asarray(masks, F32))


def _resident(block_shape, index_map):
    return pl.BlockSpec(block_shape, index_map, pipeline_mode=pl.Buffered(1))


MXU_TILE = 256


def _swiglu(xn, wg_ref, wu_ref, wd_ref):
    acc = None
    for c in range(wg_ref.shape[1] // MXU_TILE):
        cols = slice(c * MXU_TILE, (c + 1) * MXU_TILE)
        gate = _dot(xn, wg_ref[:, cols])
        up = _dot(xn, wu_ref[:, cols])
        part = _dot((gate * _sigmoid(gate) * up).astype(BF16), wd_ref[cols, :])
        acc = part if acc is None else acc + part
    return acc


def _ffn_kernel(x_ref, gn_ref, wg_ref, wu_ref, wd_ref, out_ref):
    x = x_ref[...]
    xn = _rms(x, gn_ref[...]).astype(BF16)
    out_ref[...] = x + _swiglu(xn, wg_ref, wu_ref, wd_ref)


def ffn_swiglu(h2d, g_norm, w_gu, w_down, *, tm=512):
    t_total, d = h2d.shape
    ff = w_down.shape[0]
    w_gu = w_gu.astype(BF16)
    const2 = lambda i: (0, 0)
    return pl.pallas_call(
        _ffn_kernel,
        grid=(t_total // tm,),
        in_specs=[
            pl.BlockSpec((tm, d), lambda i: (i, 0)),
            _resident((1, d), const2),
            _resident((d, ff), const2),
            _resident((d, ff), lambda i: (0, 1)),
            _resident((ff, d), const2),
        ],
        out_specs=pl.BlockSpec((tm, d), lambda i: (i, 0)),
        out_shape=jax.ShapeDtypeStruct((t_total, d), F32),
        compiler_params=_cparams(1),
        name="ffn_swiglu",
    )(h2d, g_norm.reshape(1, d), w_gu, w_gu, w_down.astype(BF16))


N_EXPERTS = 8
LANES = 128
NEG = -1e30


def _router_kernel(x_ref, gn_ref, wh_ref, wl_ref, sel_ref):
    xn = _rms(x_ref[...], gn_ref[...])
    xh, xl = _split_bf16(xn)
    logits = _dot(xh, wh_ref[...]) + (_dot(xl, wh_ref[...]) + _dot(xh, wl_ref[...]))
    lane = lax.broadcasted_iota(jnp.int32, logits.shape, 1).astype(F32)
    logits = jnp.where(lane < N_EXPERTS, logits, NEG)
    m1 = jnp.max(logits, axis=-1, keepdims=True)
    i1 = jnp.min(jnp.where(logits == m1, lane, float(LANES)), axis=-1, keepdims=True)
    rest = jnp.where(lane == i1, NEG, logits)
    m2 = jnp.max(rest, axis=-1, keepdims=True)
    i2 = jnp.min(jnp.where(rest == m2, lane, float(LANES)), axis=-1, keepdims=True)
    e2 = jnp.exp(m2 - m1)
    denom = 1.0 + e2
    sel_ref[...] = jnp.where(lane == 0, i1, jnp.where(lane == 1, i2, jnp.where(
        lane == 2, 1.0 / denom, jnp.where(lane == 3, e2 / denom, 0.0))))


def moe_route(h2d, g_norm, w_router, *, tm=512):
    t_total, d = h2d.shape
    w_pad = jnp.zeros((d, LANES), F32).at[:, :N_EXPERTS].set(w_router)
    w_hi = w_pad.astype(BF16)
    w_lo = (w_pad - w_hi.astype(F32)).astype(BF16)
    const2 = lambda i: (0, 0)
    return pl.pallas_call(
        _router_kernel,
        grid=(t_total // tm,),
        in_specs=[
            pl.BlockSpec((tm, d), lambda i: (i, 0)),
            _resident((1, d), const2),
            _resident((d, LANES), const2),
            _resident((d, LANES), const2),
        ],
        out_specs=pl.BlockSpec((tm, LANES), lambda i: (i, 0)),
        out_shape=jax.ShapeDtypeStruct((t_total, LANES), F32),
        compiler_params=_cparams(1),
        name="moe_router",
    )(h2d, g_norm.reshape(1, d), w_hi, w_lo)


MOE_TILE = 1024
GATHER_UNROLL = 8


def _moe_plan(sel, tm):
    t_total = sel.shape[0]
    e = sel[:, :2].astype(jnp.int32)
    onehot = (e[:, :, None] == jnp.arange(N_EXPERTS, dtype=jnp.int32)[None, None, :]).astype(jnp.int32).sum(1)
    csum = jnp.cumsum(onehot, axis=0)
    rank = csum - onehot
    padded = (csum[-1] + tm - 1) // tm * tm
    ends = jnp.cumsum(padded)
    pos = (ends - padded)[e] + jnp.take_along_axis(rank, e, axis=1)
    n_slots = 2 * t_total + N_EXPERTS * tm
    tile_start = jnp.arange(n_slots // tm, dtype=jnp.int32) * tm
    tile_expert = jnp.minimum(jnp.searchsorted(ends, tile_start, side="right"), N_EXPERTS - 1).astype(jnp.int32)
    tile_used = (tile_start < ends[-1]).astype(jnp.int32)
    return n_slots, tile_expert, tile_used, pos[:, 0], pos[:, 1]


def _load_indices(idx_hbm, idx_smem, sem, tile, tm):
    return pltpu.make_async_copy(idx_hbm.at[pl.ds(tile * tm, tm)], idx_smem, sem)


def _moe_dispatch_kernel(pa_hbm, pb_hbm, h_ref, zeros_hbm, xs_hbm, ia_smem, ib_smem, row_sem, idx_sem, *, tm):
    del zeros_hbm
    i = pl.program_id(0)
    ca = _load_indices(pa_hbm, ia_smem, idx_sem.at[0], i, tm)
    cb = _load_indices(pb_hbm, ib_smem, idx_sem.at[1], i, tm)
    ca.start()
    cb.start()
    ca.wait()
    cb.wait()

    def body(r, carry):
        src = h_ref.at[pl.ds(r, 1), :]
        pltpu.make_async_copy(src, xs_hbm.at[pl.ds(ia_smem[r], 1), :], row_sem.at[0]).start()
        pltpu.make_async_copy(src, xs_hbm.at[pl.ds(ib_smem[r], 1), :], row_sem.at[1]).start()
        return carry
    lax.fori_loop(0, tm, body, 0, unroll=GATHER_UNROLL)
    pltpu.make_async_copy(h_ref, xs_hbm.at[pl.ds(0, tm), :], row_sem.at[0]).wait()
    pltpu.make_async_copy(h_ref, xs_hbm.at[pl.ds(0, tm), :], row_sem.at[1]).wait()


def _moe_group_kernel(te_ref, used_ref, xs_ref, gn_ref, wg_ref, wu_ref, wd_ref, y_ref, xn_scr):
    i = pl.program_id(0)
    j = pl.program_id(1)

    @pl.when(used_ref[i] == 0)
    def _():
        y_ref[...] = jnp.zeros(y_ref.shape, F32)

    @pl.when(used_ref[i] != 0)
    def _():
        @pl.when(j == 0)
        def _():
            xn_scr[...] = _rms(xs_ref[...], gn_ref[...]).astype(BF16)

        contrib = _swiglu(xn_scr[...], wg_ref.at[0], wu_ref.at[0], wd_ref.at[0])

        @pl.when(j == 0)
        def _():
            y_ref[...] = contrib

        @pl.when(j > 0)
        def _():
            y_ref[...] += contrib


def _moe_combine_kernel(pa_hbm, pb_hbm, y_hbm, h_ref, sel_ref, out_ref, ia_smem, ib_smem, buf_a, buf_b,
                        row_sem, idx_sem, *, tm):
    i = pl.program_id(0)
    ca = _load_indices(pa_hbm, ia_smem, idx_sem.at[0], i, tm)
    cb = _load_indices(pb_hbm, ib_smem, idx_sem.at[1], i, tm)
    ca.start()
    cb.start()
    ca.wait()
    cb.wait()

    def body(r, carry):
        pltpu.make_async_copy(y_hbm.at[pl.ds(ia_smem[r], 1), :], buf_a.at[pl.ds(r, 1), :], row_sem.at[0]).start()
        pltpu.make_async_copy(y_hbm.at[pl.ds(ib_smem[r], 1), :], buf_b.at[pl.ds(r, 1), :], row_sem.at[1]).start()
        return carry
    lax.fori_loop(0, tm, body, 0, unroll=GATHER_UNROLL)
    pltpu.make_async_copy(buf_a, buf_a, row_sem.at[0]).wait()
    pltpu.make_async_copy(buf_b, buf_b, row_sem.at[1]).wait()
    sel = sel_ref[...]
    out_ref[...] = h_ref[...] + (sel[:, 2:3] * buf_a[...] + sel[:, 3:4] * buf_b[...])


def moe_swiglu_top2(h2d, g_norm, w_router, w_gu, w_down, *, n_chunks=2):
    t_total, d = h2d.shape
    tm = MOE_TILE
    ff = w_down.shape[1]
    tf = ff // n_chunks
    sel = moe_route(h2d, g_norm, w_router)
    n_slots, tile_expert, tile_used, pos_a, pos_b = _moe_plan(sel, tm)
    any_spec = pl.BlockSpec(memory_space=pl.ANY)
    index_scratch = [pltpu.SMEM((tm,), jnp.int32), pltpu.SMEM((tm,), jnp.int32)]
    sems = [pltpu.SemaphoreType.DMA((2,)), pltpu.SemaphoreType.DMA((2,))]
    xs = pl.pallas_call(
        functools.partial(_moe_dispatch_kernel, tm=tm),
        grid=(t_total // tm,),
        in_specs=[any_spec, any_spec, pl.BlockSpec((tm, d), lambda i: (i, 0)), any_spec],
        out_specs=any_spec,
        out_shape=jax.ShapeDtypeStruct((n_slots, d), F32),
        scratch_shapes=index_scratch + sems,
        input_output_aliases={3: 0},
        compiler_params=_cparams(1),
        name="moe_dispatch",
    )(pos_a, pos_b, h2d, jnp.zeros((n_slots, d), F32))
    w_gu = w_gu.astype(BF16)
    row = lambda i, j, te, used: (i, 0)
    y = pl.pallas_call(
        _moe_group_kernel,
        grid_spec=pltpu.PrefetchScalarGridSpec(
            num_scalar_prefetch=2,
            grid=(n_slots // tm, n_chunks),
            in_specs=[
                pl.BlockSpec((tm, d), row),
                pl.BlockSpec((1, d), lambda i, j, te, used: (0, 0)),
                pl.BlockSpec((1, d, tf), lambda i, j, te, used: (te[i], 0, j)),
                pl.BlockSpec((1, d, tf), lambda i, j, te, used: (te[i], 0, j + n_chunks)),
                pl.BlockSpec((1, tf, d), lambda i, j, te, used: (te[i], j, 0)),
            ],
            out_specs=pl.BlockSpec((tm, d), row),
            scratch_shapes=[pltpu.VMEM((tm, d), BF16)],
        ),
        out_shape=jax.ShapeDtypeStruct((n_slots, d), F32),
        compiler_params=_cparams(2),
        name="moe_group_ffn",
    )(tile_expert, tile_used, xs, g_norm.reshape(1, d), w_gu, w_gu, w_down.astype(BF16))
    return pl.pallas_call(
        functools.partial(_moe_combine_kernel, tm=tm),
        grid=(t_total // tm,),
        in_specs=[any_spec, any_spec, any_spec, pl.BlockSpec((tm, d), lambda i: (i, 0)),
                  pl.BlockSpec((tm, LANES), lambda i: (i, 0))],
        out_specs=pl.BlockSpec((tm, d), lambda i: (i, 0)),
        out_shape=jax.ShapeDtypeStruct((t_total, d), F32),
        scratch_shapes=index_scratch + [pltpu.VMEM((tm, d), F32), pltpu.VMEM((tm, d), F32)] + sems,
        compiler_params=_cparams(1),
        name="moe_combine",
    )(pos_a, pos_b, y, h2d, sel)


B_HEADS = 16
B_KV_HEADS = 4
B_GROUP = 4
B_HEAD_DIM = 64
CMP_STRIDE = 16
CMP_BLOCK = 32
SEL_BLOCK = 32
N_SELECT = 8
WINDOW = 512
SEL_BIG = 1e9
TINY = 1e-30
N_KV_KINDS = 6
ONES_LANE = B_HEAD_DIM


def _group_rms_rows(x, gain_col, n_groups):
    r, c = x.shape
    x3 = x.reshape(n_groups, r // n_groups, c)
    y = x3 * lax.rsqrt(jnp.mean(x3 * x3, axis=1, keepdims=True) + NORM_EPS)
    return y.reshape(r, c) * gain_col


LOG2E = 1.4426950408889634
POS_RADIX = 64
N_SLOPE_PARTS = 3


def _alibi_key_part(pos, idx):
    hi = (pos // POS_RADIX * POS_RADIX).astype(F32)
    lo = (pos % POS_RADIX).astype(F32)
    return jnp.where(idx < N_SLOPE_PARTS, hi, jnp.where(idx < 2 * N_SLOPE_PARTS, lo, 0.0))


def _alibi_query_part(slopes):
    parts, rest = [], slopes.astype(F32)
    for _ in range(N_SLOPE_PARTS):
        p = rest.astype(BF16)
        parts.append(p)
        rest = rest - p.astype(F32)
    return jnp.stack(parts + parts, axis=-1)


def _ones_lane(shape):
    lane = lax.broadcasted_iota(jnp.int32, shape, len(shape) - 1)
    return jnp.where(lane % LANES >= ONES_LANE, 1.0, 0.0)


def _kv_proj_kernel(x_ref, gn_ref, wkt_ref, kgain_ref, wv_ref, wc_ref,
                    kst_ref, kwt_ref, vs_ref, vw_ref, kc_ref, vc_ref):
    g = B_KV_HEADS
    hn = _rms(x_ref[...], gn_ref[...]).astype(BF16)
    kt = _dot_nt(wkt_ref[...], hn)
    kt = _group_rms_rows(kt, kgain_ref[...], 2 * g)
    half = g * B_HEAD_DIM
    tm = kt.shape[1]
    pos = pl.program_id(1) * tm + lax.broadcasted_iota(jnp.int32, (LANES - B_HEAD_DIM, tm), 1)
    pos_rows = _alibi_key_part(pos, lax.broadcasted_iota(jnp.int32, pos.shape, 0))
    for j in range(g):
        kst_ref[0, j] = jnp.concatenate([kt[j * B_HEAD_DIM:(j + 1) * B_HEAD_DIM], pos_rows], axis=0).astype(BF16)
        kwt_ref[0, j] = jnp.concatenate([kt[half + j * B_HEAD_DIM:half + (j + 1) * B_HEAD_DIM], pos_rows],
                                        axis=0).astype(BF16)
    v = _dot(hn, wv_ref[...])
    v = (v + _ones_lane(v.shape)).astype(BF16)
    for j in range(g):
        vs_ref[0, j] = v[:, j * LANES:(j + 1) * LANES]
        vw_ref[0, j] = v[:, (g + j) * LANES:(g + j + 1) * LANES]
    c = _dot(hn, wc_ref[...])
    kc_ref[...] = c[:, :half].astype(BF16)
    vc_ref[...] = c[:, half:].astype(BF16)


def nsa_kv_proj(h2d, bsz, seq, kv_norm, kv_w, k_norm, *, tm=512):
    t_total, d = h2d.shape
    g, dh = B_KV_HEADS, B_HEAD_DIM
    n_sblk = seq // tm
    w = kv_w.reshape(d, N_KV_KINDS, g, dh)
    wkt = jnp.concatenate([w[:, 2].reshape(d, g * dh), w[:, 4].reshape(d, g * dh)], axis=1).T.astype(BF16)
    kgain = jnp.concatenate([jnp.tile(k_norm[1], g), jnp.tile(k_norm[2], g)]).reshape(2 * g * dh, 1)
    wv = jnp.stack([w[:, 3], w[:, 5]], axis=1)
    wv = jnp.pad(wv, ((0, 0), (0, 0), (0, 0), (0, LANES - dh))).reshape(d, 2 * g * LANES).astype(BF16)
    wc = jnp.concatenate([w[:, 0].reshape(d, g * dh), w[:, 1].reshape(d, g * dh)], axis=1).astype(BF16)
    const2 = lambda b, s: (0, 0)
    row = lambda b, s: (b * n_sblk + s, 0)
    kt_spec = pl.BlockSpec((1, g, LANES, tm), lambda b, s: (b, 0, 0, s))
    v_spec = pl.BlockSpec((1, g, tm, LANES), lambda b, s: (b, 0, s, 0))
    kt_shape = jax.ShapeDtypeStruct((bsz, g, LANES, seq), BF16)
    v_shape = jax.ShapeDtypeStruct((bsz, g, seq, LANES), BF16)
    c_shape = jax.ShapeDtypeStruct((t_total, g * dh), BF16)
    return pl.pallas_call(
        _kv_proj_kernel,
        grid=(bsz, n_sblk),
        in_specs=[
            pl.BlockSpec((tm, d), row),
            _resident((1, d), const2),
            _resident((2 * g * dh, d), const2),
            _resident((2 * g * dh, 1), const2),
            _resident((d, 2 * g * LANES), const2),
            _resident((d, 2 * g * dh), const2),
        ],
        out_specs=[kt_spec, kt_spec, v_spec, v_spec,
                   pl.BlockSpec((tm, g * dh), row), pl.BlockSpec((tm, g * dh), row)],
        out_shape=[kt_shape, kt_shape, v_shape, v_shape, c_shape, c_shape],
        compiler_params=_cparams(2),
        name="nsa_kv_proj",
    )(h2d, kv_norm.reshape(1, d), wkt, kgain, wv, wc)


def _compress_kernel(ak_ref, av_ref, posk_ref, posv_ref, w1k_ref, w1v_ref, w2k_ref, w2v_ref, kgain_ref,
                     kca_ref, vca_ref, *, n_half):
    half_w = ak_ref.shape[1]

    def hidden(a_ref, pos_ref, w1_ref):
        a = a_ref[...]
        top = _dot(a, w1_ref[:half_w, :])
        bot = _dot(a, w1_ref[half_w:, :])
        bias = _dot(pos_ref[...], w1_ref[...])[0:1, :]
        pre = top + pltpu.roll(bot, shift=a.shape[0] - 1, axis=0) + bias
        return (pre * _sigmoid(pre)).astype(BF16)

    act_k = hidden(ak_ref, posk_ref, w1k_ref)
    kc = _dot(act_k, w2k_ref[...])
    ms = jnp.sum(kc * kc, axis=-1, keepdims=True) * (1.0 / B_HEAD_DIM)
    kc = kc * lax.rsqrt(ms + NORM_EPS) * kgain_ref[...]
    blk = lax.broadcasted_iota(jnp.int32, kc.shape, 0) % n_half
    lane = lax.broadcasted_iota(jnp.int32, kc.shape, 1)
    c_end = blk * CMP_STRIDE + (CMP_BLOCK - 1)
    kca_ref[...] = (kc + jnp.where(lane >= B_HEAD_DIM, _alibi_key_part(c_end, lane - B_HEAD_DIM), 0.0)).astype(BF16)
    act_v = hidden(av_ref, posv_ref, w1v_ref)
    vca = _dot(act_v, w2v_ref[...])
    vca_ref[...] = (vca + _ones_lane(vca.shape)).astype(BF16)


def nsa_compress(kc_raw, vc_raw, bsz, seq, pos_k, w1_k, w2_k, pos_v, w1_v, w2_v, k_gain, *, groups_per_step=4):
    g, dh = B_KV_HEADS, B_HEAD_DIM
    n_half = seq // CMP_STRIDE
    feat = CMP_STRIDE * dh

    def to_half_blocks(a):
        a = a.reshape(bsz, n_half, CMP_STRIDE, g, dh).transpose(0, 3, 1, 2, 4)
        return a.reshape(bsz * g * n_half, feat)

    def pos_rows(pos):
        return jnp.zeros((8, 2 * feat), F32).at[0].set(pos.reshape(-1)).astype(BF16)

    hid = w1_k.shape[1]
    pad_lanes = lambda a: jnp.pad(a, ((0, 0), (0, LANES - dh)))
    rows = groups_per_step * n_half
    const2 = lambda i: (0, 0)
    return pl.pallas_call(
        functools.partial(_compress_kernel, n_half=n_half),
        grid=(bsz * g // groups_per_step,),
        in_specs=[
            pl.BlockSpec((rows, feat), lambda i: (i, 0)),
            pl.BlockSpec((rows, feat), lambda i: (i, 0)),
            _resident((8, 2 * feat), const2),
            _resident((8, 2 * feat), const2),
            _resident((2 * feat, hid), const2),
            _resident((2 * feat, hid), const2),
            _resident((hid, LANES), const2),
            _resident((hid, LANES), const2),
            _resident((1, LANES), const2),
        ],
        out_specs=[pl.BlockSpec((rows, LANES), lambda i: (i, 0)),
                   pl.BlockSpec((rows, LANES), lambda i: (i, 0))],
        out_shape=[jax.ShapeDtypeStruct((bsz * g * n_half, LANES), BF16),
                   jax.ShapeDtypeStruct((bsz * g * n_half, LANES), BF16)],
        compiler_params=_cparams(1),
        name="nsa_compress",
    )(to_half_blocks(kc_raw), to_half_blocks(vc_raw), pos_rows(pos_k), pos_rows(pos_v),
      w1_k.astype(BF16), w1_v.astype(BF16), pad_lanes(w2_k).astype(BF16), pad_lanes(w2_v).astype(BF16),
      pad_lanes(k_gain.reshape(1, dh)))


def _q_proj_kernel(x_ref, gn_ref, w_ref, qgain_ref, qconst_ref, q_ref, gate_ref):
    d_q = q_ref.shape[1]
    xn = _rms(x_ref[...], gn_ref[...]).astype(BF16)
    proj = _dot(xn, w_ref[...])
    ones = jnp.ones((LANES, LANES), BF16)
    for h in range(d_q // LANES):
        lanes = slice(h * LANES, (h + 1) * LANES)
        q = proj[:, lanes]
        sq_hi, sq_lo = _split_bf16(q * q)
        ms = (_dot(sq_hi, ones) + _dot(sq_lo, ones)) * (1.0 / B_HEAD_DIM)
        q_ref[:, lanes] = (q * lax.rsqrt(ms + NORM_EPS) * qgain_ref[:, lanes] + qconst_ref[:, lanes]).astype(BF16)
    gate_ref[...] = _sigmoid(proj[:, d_q:])


def nsa_q_proj(h2d, g_norm, w_in, q_norm, *, tm=512):
    t_total, d = h2d.shape
    g, hpg, dh = B_KV_HEADS, B_GROUP, B_HEAD_DIM
    d_q = B_HEADS * LANES
    wq = jnp.pad(w_in[:, :B_HEADS * dh].reshape(d, B_HEADS, dh), ((0, 0), (0, 0), (0, LANES - dh))).reshape(d, d_q)
    wg = w_in[:, B_HEADS * dh:].reshape(d, g, hpg * 3)
    wg = jnp.pad(wg, ((0, 0), (0, 0), (0, LANES - hpg * 3))).reshape(d, g * LANES)
    w = jnp.concatenate([wq, wg], axis=1).astype(BF16)
    qgain = jnp.tile(jnp.pad(q_norm * (dh ** -0.5 * LOG2E), (0, LANES - dh)), B_HEADS).reshape(1, d_q)
    slopes = np.array([2.0 ** (-8.0 * (h + 1) / B_HEADS) for h in range(B_HEADS)], dtype=np.float64) * LOG2E
    feats = _alibi_query_part(jnp.asarray(slopes, F32)).astype(F32)
    qconst = jnp.pad(feats, ((0, 0), (dh, LANES - dh - feats.shape[1]))).reshape(1, d_q)
    const2 = lambda i: (0, 0)
    return pl.pallas_call(
        _q_proj_kernel,
        grid=(t_total // tm,),
        in_specs=[
            pl.BlockSpec((tm, d), lambda i: (i, 0)),
            _resident((1, d), const2),
            _resident((d, d_q + g * LANES), const2),
            _resident((1, d_q), const2),
            _resident((1, d_q), const2),
        ],
        out_specs=[pl.BlockSpec((tm, d_q), lambda i: (i, 0)), pl.BlockSpec((tm, g * LANES), lambda i: (i, 0))],
        out_shape=[jax.ShapeDtypeStruct((t_total, d_q), BF16), jax.ShapeDtypeStruct((t_total, g * LANES), F32)],
        compiler_params=_cparams(1),
        name="nsa_q_proj",
    )(h2d, g_norm.reshape(1, d), w, qgain, qconst)


def _nsa_attn_kernel(q_ref, gate_ref, kca_ref, vca_ref, kst_ref, vs_ref, kwt_ref, vw_ref,
                     ovlt_ref, esel_ref, o_ref, live_smem, m_scr, acc_scr, *, tq, tk):
    hpg, dh = B_GROUP, B_HEAD_DIM
    t0 = pl.program_id(2) * tq
    rows = hpg * tq
    qb = q_ref[...]
    q_st = jnp.concatenate([qb[:, i * LANES:(i + 1) * LANES] for i in range(hpg)], axis=0)
    t_q = t0 + lax.broadcasted_iota(jnp.int32, (tq, 1), 0)

    n_half = kca_ref.shape[0]
    c_end = lax.broadcasted_iota(jnp.int32, (n_half, 1), 0) * CMP_STRIDE + (CMP_BLOCK - 1)
    t_lane = t0 + lax.broadcasted_iota(jnp.int32, (1, rows), 1) % tq
    seen = t_lane >= c_end
    s = jnp.where(seen, _dot_nt(kca_ref[...], q_st), NEG)
    m = jnp.max(s, axis=0, keepdims=True)
    e = jnp.where(seen, jnp.exp2(s - m), 0.0)
    p = e / jnp.maximum(jnp.sum(e, axis=0, keepdims=True), TINY)
    o_cmp = _dot_tn(p.astype(BF16), vca_ref[...])[:, :dh]

    p_grp = p[:, 0:tq]
    for i in range(1, hpg):
        p_grp = p_grp + p[:, i * tq:(i + 1) * tq]
    p_hi, p_lo = _split_bf16(p_grp)
    n_blk = LANES // 2
    imp = (_dot(ovlt_ref[...], p_hi) + _dot(ovlt_ref[...], p_lo))[:n_blk]
    blk = lax.broadcasted_iota(jnp.int32, (n_blk, tq), 0)
    t_l = t0 + lax.broadcasted_iota(jnp.int32, (1, tq), 1)
    valid = blk * SEL_BLOCK <= t_l
    cur = t_l // SEL_BLOCK
    forced = valid & ((blk == 0) | (blk == cur) | (blk == cur - 1))
    work = jnp.where(forced, SEL_BIG, jnp.where(valid, imp, -SEL_BIG))
    blk_f = blk.astype(F32)
    chosen = jnp.zeros((n_blk, tq), F32)
    for _ in range(N_SELECT):
        top = jnp.max(work, axis=0, keepdims=True)
        first = jnp.min(jnp.where(work == top, blk_f, float(LANES)), axis=0, keepdims=True)
        hit = blk_f == first
        chosen = jnp.where(hit & (top > -0.5 * SEL_BIG), 1.0, chosen)
        work = jnp.where(hit, -3e38, work)
    blocks_per_tile = tk // SEL_BLOCK
    any_q = jnp.max(chosen, axis=1, keepdims=True)
    for kt in range(n_blk // blocks_per_tile):
        tile_any = jnp.max(any_q[kt * blocks_per_tile:(kt + 1) * blocks_per_tile])
        live_smem[kt] = (tile_any > 0.5).astype(jnp.int32)
    chosen = jnp.concatenate([chosen, jnp.zeros((LANES - n_blk, tq), F32)], axis=0).T.astype(BF16)

    def normalised(acc):
        return (acc / pltpu.roll(acc, shift=LANES - ONES_LANE, axis=1))[:, :dh]

    span = WINDOW + tq
    w0 = pl.multiple_of(jnp.maximum(t0 - WINDOW, 0), LANES)
    dist = t_q - (w0 + lax.broadcasted_iota(jnp.int32, (1, span), 1))
    pen = jnp.where((dist >= 0) & (dist < WINDOW), 0.0, NEG)
    s = _dot(q_st, kwt_ref[0, 0, :, pl.ds(w0, span)]) + jnp.concatenate([pen] * hpg, axis=0)
    m = jnp.max(s, axis=-1, keepdims=True)
    o_win = normalised(_dot(jnp.exp2(s - m).astype(BF16), vw_ref[0, 0, pl.ds(w0, span), :]))

    m_scr[...] = jnp.full(m_scr.shape, NEG, F32)
    acc_scr[...] = jnp.zeros(acc_scr.shape, F32)

    def sel_step(kt, carry):
        @pl.when(live_smem[kt] != 0)
        def _():
            k0 = pl.multiple_of(kt * tk, tk)
            pos = k0 + lax.broadcasted_iota(jnp.int32, (1, tk), 1)
            picked = _dot(chosen, esel_ref[:, pl.ds(k0, tk)])
            pen = jnp.where((picked > 0.5) & (pos <= t_q), 0.0, NEG)
            sc = _dot(q_st, kst_ref[0, 0, :, pl.ds(k0, tk)]) + jnp.concatenate([pen] * hpg, axis=0)
            m_run = m_scr[...]
            m_new = jnp.maximum(m_run, jnp.broadcast_to(jnp.max(sc, axis=-1, keepdims=True), m_run.shape))
            pr = jnp.exp2(sc - jnp.concatenate([m_new] * (tk // LANES), axis=1)).astype(BF16)
            acc_scr[...] = jnp.exp2(m_run - m_new) * acc_scr[...] + _dot(pr, vs_ref[0, 0, pl.ds(k0, tk), :])
            m_scr[...] = m_new
        return carry

    lax.fori_loop(0, (t0 + tq + tk - 1) // tk, sel_step, 0)
    o_sel = normalised(acc_scr[...])

    gate = gate_ref[...]
    outs = []
    for i in range(hpg):
        r = slice(i * tq, (i + 1) * tq)
        outs.append(gate[:, 3 * i:3 * i + 1] * o_cmp[r] + gate[:, 3 * i + 1:3 * i + 2] * o_sel[r]
                    + gate[:, 3 * i + 2:3 * i + 3] * o_win[r])
    o_ref[...] = jnp.concatenate(outs, axis=1).astype(BF16)


def nsa_attention(q, gates, kca, vca, kst, vs, kwt, vw, bsz, seq, *, tq=128, tk=512):
    g, hpg, dh = B_KV_HEADS, B_GROUP, B_HEAD_DIM
    n_q = seq // tq
    n_half = seq // CMP_STRIDE
    n_cmp = n_half - 1
    n_sel = seq // SEL_BLOCK
    assert n_sel <= LANES // 2 and tq == LANES
    c_start = np.arange(n_half) * CMP_STRIDE
    j_sel = np.arange(LANES)
    ovl = ((c_start[:, None] < (j_sel[None, :] + 1) * SEL_BLOCK)
           & (c_start[:, None] + CMP_BLOCK > j_sel[None, :] * SEL_BLOCK)
           & (np.arange(n_half)[:, None] < n_cmp) & (j_sel[None, :] < n_sel))
    esel = (np.arange(seq)[None, :] // SEL_BLOCK) == j_sel[:, None]
    row = lambda b, j, i: (b * n_q + i, j)
    per_bg = lambda b, j, i: (b, j, 0, 0)
    return pl.pallas_call(
        functools.partial(_nsa_attn_kernel, tq=tq, tk=tk),
        grid=(bsz, g, n_q),
        in_specs=[
            pl.BlockSpec((tq, hpg * LANES), row),
            pl.BlockSpec((tq, LANES), row),
            pl.BlockSpec((n_half, LANES), lambda b, j, i: (b * g + j, 0)),
            pl.BlockSpec((n_half, LANES), lambda b, j, i: (b * g + j, 0)),
            pl.BlockSpec((1, 1, LANES, seq), per_bg),
            pl.BlockSpec((1, 1, seq, LANES), per_bg),
            pl.BlockSpec((1, 1, LANES, seq), per_bg),
            pl.BlockSpec((1, 1, seq, LANES), per_bg),
            _resident((LANES, n_half), lambda b, j, i: (0, 0)),
            _resident((LANES, seq), lambda b, j, i: (0, 0)),
        ],
        out_specs=pl.BlockSpec((tq, hpg * dh), row),
        out_shape=jax.ShapeDtypeStruct((bsz * seq, B_HEADS * dh), BF16),
        scratch_shapes=[pltpu.SMEM((seq // tk,), jnp.int32), pltpu.VMEM((hpg * tq, LANES), F32),
                        pltpu.VMEM((hpg * tq, LANES), F32)],
        compiler_params=_cparams(3),
        name="nsa_attention",
    )(q, gates, kca, vca, kst, vs, kwt, vw, jnp.asarray(ovl.T, BF16), jnp.asarray(esel, BF16))


def _out_proj_kernel(h_ref, o_ref, w_ref, out_ref):
    out_ref[...] = h_ref[...] + _dot(o_ref[...], w_ref[...])


def out_proj_residual(h2d, o, w_out, *, tm=512):
    t_total, d = h2d.shape
    k = o.shape[1]
    return pl.pallas_call(
        _out_proj_kernel,
        grid=(t_total // tm,),
        in_specs=[pl.BlockSpec((tm, d), lambda i: (i, 0)), pl.BlockSpec((tm, k), lambda i: (i, 0)),
                  _resident((k, d), lambda i: (0, 0))],
        out_specs=pl.BlockSpec((tm, d), lambda i: (i, 0)),
        out_shape=jax.ShapeDtypeStruct((t_total, d), F32),
        compiler_params=_cparams(1),
        name="out_proj_residual",
    )(h2d, o, w_out.astype(BF16))


def nsa_shared_kv(h2d, bsz, seq, kv_norm, kv_w, cmp_pos_k, cmp_w1_k, cmp_w2_k, cmp_pos_v, cmp_w1_v, cmp_w2_v, k_norm):
    kst, kwt, vs, vw, kc_raw, vc_raw = nsa_kv_proj(h2d, bsz, seq, kv_norm, kv_w, k_norm)
    kct, vca = nsa_compress(kc_raw, vc_raw, bsz, seq, cmp_pos_k, cmp_w1_k, cmp_w2_k,
                            cmp_pos_v, cmp_w1_v, cmp_w2_v, k_norm[0])
    return kct, vca, kst, vs, kwt, vw


def nsa_mixer(h2d, bsz, seq, g_norm, w_in, w_out, q_norm, kv):
    q, gates = nsa_q_proj(h2d, g_norm, w_in, q_norm)
    o = nsa_attention(q, gates, *kv, bsz, seq)
    return out_proj_residual(h2d, o, w_out)


def kernel(x, norm_mix, norm_ffn, hgrn_w_in, hgrn_w_out, hgrn_g_norm, hgrn_lb_logits, kv_norm, kv_w, cmp_pos_k, cmp_w1_k, cmp_w2_k, cmp_pos_v, cmp_w1_v, cmp_w2_v, k_norm, nsa_w_in, nsa_w_out, nsa_q_norm, ffn_w_gu, ffn_w_down, moe_router, moe_w_gu, moe_w_down):
    bsz, seq, d = x.shape
    lb = jnp.cumsum(jax.nn.softmax(hgrn_lb_logits.astype(F32), axis=0), axis=0)
    lb = lb - lb[0:1]
    n_a = hgrn_w_in.shape[0]
    depth = norm_mix.shape[0]
    h = x.reshape(bsz * seq, d)
    kv = None
    for layer in range(depth):
        if layer < n_a:
            h = hgrn_mixer(h, seq, norm_mix[layer], hgrn_w_in[layer], hgrn_w_out[layer], hgrn_g_norm[layer],
                           lb[layer])
        else:
            if kv is None:
                kv = nsa_shared_kv(h, bsz, seq, kv_norm, kv_w, cmp_pos_k, cmp_w1_k, cmp_w2_k,
                                   cmp_pos_v, cmp_w1_v, cmp_w2_v, k_norm)
            j = layer - n_a
            h = nsa_mixer(h, bsz, seq, norm_mix[layer], nsa_w_in[j], nsa_w_out[j], nsa_q_norm[j], kv)
        if layer % 2 == 0:
            h = ffn_swiglu(h, norm_ffn[layer], ffn_w_gu[layer // 2], ffn_w_down[layer // 2])
        else:
            h = moe_swiglu_top2(h, norm_ffn[layer], moe_router[layer // 2], moe_w_gu[layer // 2],
                                moe_w_down[layer // 2])
    return h.reshape(bsz, seq, d)
```

```python
import functools

import numpy as np
import jax
import jax.numpy as jnp
from jax import lax
from jax.experimental import pallas as pl
from jax.experimental.pallas import tpu as pltpu

F32 = jnp.float32
BF16 = jnp.bfloat16

NORM_EPS = 1e-6
VMEM_LIMIT_BYTES = 56 * 1024 * 1024

A_HEADS = 8
A_HEAD_DIM = 128
SCAN_ROWS = 128
SUBLANES = 8


def _cparams(n_axes):
    return pltpu.CompilerParams(dimension_semantics=("arbitrary",) * n_axes,
                                vmem_limit_bytes=VMEM_LIMIT_BYTES)


def _dot(a, b):
    return jnp.dot(a, b, preferred_element_type=F32)


def _dot_nt(a, b):
    return lax.dot_general(a, b, (((1,), (1,)), ((), ())), preferred_element_type=F32)


def _dot_tn(a, b):
    return lax.dot_general(a, b, (((0,), (0,)), ((), ())), preferred_element_type=F32)


def _sigmoid(x):
    return 1.0 / (1.0 + jnp.exp(-x))


def _rms(x, g):
    return x * lax.rsqrt(jnp.mean(x * x, axis=-1, keepdims=True) + NORM_EPS) * g


def _split_bf16(x):
    hi = x.astype(BF16)
    lo = (x - hi.astype(F32)).astype(BF16)
    return hi, lo


def _hgrn_constants():
    n = SCAN_ROWS
    t = np.arange(n)
    r = np.arange(n)[None, :]
    sums, masks = [], []
    m = n // 2
    while m >= 1:
        grp = t // (2 * m)
        mid = grp * 2 * m + m
        upper = (t % (2 * m)) >= m
        if m < SUBLANES:
            sums.append(np.where(upper[:, None], (r >= mid[:, None]) & (r <= t[:, None]),
                                 (r > t[:, None]) & (r < mid[:, None])))
        masks.append((grp[:, None] == grp[None, :]) & upper[:, None] & (~upper[None, :]))
        m //= 2
    masks.append(np.eye(n, dtype=bool))
    sums.append(r <= t[:, None])
    w = np.concatenate(sums, 0).astype(np.float32)
    return np.concatenate([w, w], 1), np.stack(masks).astype(np.float32)


def _hgrn_kernel(x_ref, gn_ref, win_ref, lb_ref, ghn_ref, wout_ref, wsum_ref, msk_ref,
                 out_ref, xn_scr, st_scr, og_scr, *, tm):
    sblk = pl.program_id(1)
    h = pl.program_id(2)
    n = SCAN_ROWS
    n_lvl = msk_ref.shape[0] - 1

    @pl.when(h == 0)
    def _():
        xn_scr[...] = _rms(x_ref[...], gn_ref[...]).astype(BF16)

    @pl.when(sblk == 0)
    def _():
        st_scr[h] = jnp.zeros((A_HEAD_DIM, A_HEAD_DIM), F32)

    proj = _dot(xn_scr[...], win_ref[...])
    q = proj[:, 0:128]
    f = proj[:, 128:256]
    v = proj[:, 256:384]
    g = proj[:, 384:512]
    lb = lb_ref[0:1, :]
    log_lb = lb_ref[1:2, :]
    log_1m_lb = lb_ref[2:3, :]
    qs = q * _sigmoid(q)
    log_sig = jnp.minimum(f, 0.0) - jnp.log1p(jnp.exp(-jnp.abs(f)))
    b = log_1m_lb + log_sig
    logf = jnp.maximum(log_lb, b) + jnp.log1p(jnp.exp(-jnp.abs(log_lb - b)))
    key = (1.0 - lb) * _sigmoid(-f)

    wide = [n >> (l + 1) for l in range(n_lvl) if n >> (l + 1) >= SUBLANES]
    row_id = lax.broadcasted_iota(jnp.int32, (n, A_HEAD_DIM), 0)
    in_upper_half = [(row_id % (2 * m)) >= m for m in wide]

    st = st_scr[h]
    outs = []
    for r in range(tm // n):
        rows = slice(r * n, (r + 1) * n)
        hi, lo = _split_bf16(logf[rows])
        sums = _dot(wsum_ref[...], jnp.concatenate([hi, lo], axis=0))
        cum = sums[(n_lvl - len(wide)) * n:]
        e_lvl = []
        for m, upper in zip(wide, in_upper_half):
            bound = jnp.concatenate([jnp.broadcast_to(cum[i + m - 1:i + m], (2 * m, A_HEAD_DIM))
                                     for i in range(0, n, 2 * m)], axis=0)
            e_lvl.append(jnp.exp(jnp.where(upper, cum - bound, bound - cum)))
        e_sums = jnp.exp(sums)
        e_lvl += [e_sums[i * n:(i + 1) * n] for i in range(n_lvl - len(wide))]
        e_prefix = e_sums[(n_lvl - len(wide)) * n:]
        e_suffix = jnp.exp(cum[n - 1:n] - cum)
        qb = qs[rows]
        kb = key[rows]
        vb = v[rows].astype(BF16)
        s = msk_ref[n_lvl] * _dot_nt(qb.astype(BF16), kb.astype(BF16))
        for l in range(n_lvl):
            s = s + msk_ref[l] * _dot_nt((qb * e_lvl[l]).astype(BF16), (kb * e_lvl[l]).astype(BF16))
        o = _dot(s.astype(BF16), vb) + _dot_nt((qb * e_prefix).astype(BF16), st.astype(BF16))
        st = st * e_prefix[n - 1:n, :] + _dot_tn(vb, (kb * e_suffix).astype(BF16))
        outs.append(o)
    st_scr[h] = st
    o = jnp.concatenate(outs, axis=0)

    og_scr[:, pl.ds(pl.multiple_of(h * A_HEAD_DIM, A_HEAD_DIM), A_HEAD_DIM)] = (
        _rms(o, ghn_ref[...]) * (g * _sigmoid(g))).astype(BF16)

    @pl.when(h == A_HEADS - 1)
    def _():
        out_ref[...] = x_ref[...] + _dot(og_scr[...], wout_ref[...])


def hgrn_mixer(h2d, seq, g_norm_in, w_in, w_out, g_head, lb, *, tm=512):
    t_total, d = h2d.shape
    n_sblk = seq // tm
    hd = A_HEAD_DIM
    w_perm = w_in.reshape(d, 4, A_HEADS, hd).transpose(0, 2, 1, 3).reshape(d, 4 * A_HEADS * hd).astype(BF16)
    lb_rows = jnp.zeros((8, A_HEADS * hd), F32)
    lb_rows = lb_rows.at[0].set(lb).at[1].set(jnp.log(lb)).at[2].set(jnp.log1p(-lb))
    wsum, masks = _hgrn_constants()
    n_blk = wsum.shape[0]
    grid = (t_total // seq, n_sblk, A_HEADS)
    row = lambda b, s, h: (b * n_sblk + s, 0)
    const2 = lambda b, s, h: (0, 0)
    return pl.pallas_call(
        functools.partial(_hgrn_kernel, tm=tm),
        grid=grid,
        in_specs=[
            pl.BlockSpec((tm, d), row),
            pl.BlockSpec((1, d), const2),
            pl.BlockSpec((d, 4 * hd), lambda b, s, h: (0, h)),
            pl.BlockSpec((8, hd), lambda b, s, h: (0, h)),
            pl.BlockSpec((1, hd), const2),
            _resident((A_HEADS * hd, d), const2),
            _resident((n_blk, 2 * SCAN_ROWS), const2),
            _resident(masks.shape, lambda b, s, h: (0, 0, 0)),
        ],
        out_specs=pl.BlockSpec((tm, d), row),
        out_shape=jax.ShapeDtypeStruct((t_total, d), F32),
        scratch_shapes=[pltpu.VMEM((tm, d), BF16), pltpu.VMEM((A_HEADS, hd, hd), F32),
                        pltpu.VMEM((tm, A_HEADS * hd), BF16)],
        compiler_params=_cparams(3),
        name="hgrn_mixer",
    )(h2d, g_norm_in.reshape(1, d), w_perm, lb_rows, g_head.reshape(1, hd), w_out.astype(BF16),
      jnp.asarray(wsum, BF16), jnp.asarray(masks, F32))


def _resident(block_shape, index_map):
    return pl.BlockSpec(block_shape, index_map, pipeline_mode=pl.Buffered(1))


MXU_TILE = 256


def _swiglu(xn, wg_ref, wu_ref, wd_ref):
    acc = None
    for c in range(wg_ref.shape[1] // MXU_TILE):
        cols = slice(c * MXU_TILE, (c + 1) * MXU_TILE)
        gate = _dot(xn, wg_ref[:, cols])
        up = _dot(xn, wu_ref[:, cols])
        part = _dot((gate * _sigmoid(gate) * up).astype(BF16), wd_ref[cols, :])
        acc = part if acc is None else acc + part
    return acc


def _ffn_kernel(x_ref, gn_ref, wg_ref, wu_ref, wd_ref, out_ref):
    x = x_ref[...]
    xn = _rms(x, gn_ref[...]).astype(BF16)
    out_ref[...] = x + _swiglu(xn, wg_ref, wu_ref, wd_ref)


def ffn_swiglu(h2d, g_norm, w_gu, w_down, *, tm=512):
    t_total, d = h2d.shape
    ff = w_down.shape[0]
    w_gu = w_gu.astype(BF16)
    const2 = lambda i: (0, 0)
    return pl.pallas_call(
        _ffn_kernel,
        grid=(t_total // tm,),
        in_specs=[
            pl.BlockSpec((tm, d), lambda i: (i, 0)),
            _resident((1, d), const2),
            _resident((d, ff), const2),
            _resident((d, ff), lambda i: (0, 1)),
            _resident((ff, d), const2),
        ],
        out_specs=pl.BlockSpec((tm, d), lambda i: (i, 0)),
        out_shape=jax.ShapeDtypeStruct((t_total, d), F32),
        compiler_params=_cparams(1),
        name="ffn_swiglu",
    )(h2d, g_norm.reshape(1, d), w_gu, w_gu, w_down.astype(BF16))


N_EXPERTS = 8
LANES = 128
NEG = -1e30


def _router_kernel(x_ref, gn_ref, wh_ref, wl_ref, sel_ref):
    xn = _rms(x_ref[...], gn_ref[...])
    xh, xl = _split_bf16(xn)
    logits = _dot(xh, wh_ref[...]) + (_dot(xl, wh_ref[...]) + _dot(xh, wl_ref[...]))
    lane = lax.broadcasted_iota(jnp.int32, logits.shape, 1).astype(F32)
    logits = jnp.where(lane < N_EXPERTS, logits, NEG)
    m1 = jnp.max(logits, axis=-1, keepdims=True)
    i1 = jnp.min(jnp.where(logits == m1, lane, float(LANES)), axis=-1, keepdims=True)
    rest = jnp.where(lane == i1, NEG, logits)
    m2 = jnp.max(rest, axis=-1, keepdims=True)
    i2 = jnp.min(jnp.where(rest == m2, lane, float(LANES)), axis=-1, keepdims=True)
    e2 = jnp.exp(m2 - m1)
    denom = 1.0 + e2
    sel_ref[...] = jnp.where(lane == 0, i1, jnp.where(lane == 1, i2, jnp.where(
        lane == 2, 1.0 / denom, jnp.where(lane == 3, e2 / denom, 0.0))))


def moe_route(h2d, g_norm, w_router, *, tm=512):
    t_total, d = h2d.shape
    w_pad = jnp.zeros((d, LANES), F32).at[:, :N_EXPERTS].set(w_router)
    w_hi = w_pad.astype(BF16)
    w_lo = (w_pad - w_hi.astype(F32)).astype(BF16)
    const2 = lambda i: (0, 0)
    return pl.pallas_call(
        _router_kernel,
        grid=(t_total // tm,),
        in_specs=[
            pl.BlockSpec((tm, d), lambda i: (i, 0)),
            _resident((1, d), const2),
            _resident((d, LANES), const2),
            _resident((d, LANES), const2),
        ],
        out_specs=pl.BlockSpec((tm, LANES), lambda i: (i, 0)),
        out_shape=jax.ShapeDtypeStruct((t_total, LANES), F32),
        compiler_params=_cparams(1),
        name="moe_router",
    )(h2d, g_norm.reshape(1, d), w_hi, w_lo)


MOE_TILE = 1024
GATHER_UNROLL = 8


def _moe_plan(sel, tm):
    t_total = sel.shape[0]
    e = sel[:, :2].astype(jnp.int32)
    onehot = (e[:, :, None] == jnp.arange(N_EXPERTS, dtype=jnp.int32)[None, None, :]).astype(jnp.int32).sum(1)
    csum = jnp.cumsum(onehot, axis=0)
    rank = csum - onehot
    padded = (csum[-1] + tm - 1) // tm * tm
    ends = jnp.cumsum(padded)
    pos = (ends - padded)[e] + jnp.take_along_axis(rank, e, axis=1)
    n_slots = 2 * t_total + N_EXPERTS * tm
    tile_start = jnp.arange(n_slots // tm, dtype=jnp.int32) * tm
    tile_expert = jnp.minimum(jnp.searchsorted(ends, tile_start, side="right"), N_EXPERTS - 1).astype(jnp.int32)
    tile_used = (tile_start < ends[-1]).astype(jnp.int32)
    return n_slots, tile_expert, tile_used, pos[:, 0], pos[:, 1]


def _load_indices(idx_hbm, idx_smem, sem, tile, tm):
    return pltpu.make_async_copy(idx_hbm.at[pl.ds(tile * tm, tm)], idx_smem, sem)


def _moe_dispatch_kernel(pa_hbm, pb_hbm, h_ref, zeros_hbm, xs_hbm, ia_smem, ib_smem, row_sem, idx_sem, *, tm):
    del zeros_hbm
    i = pl.program_id(0)
    ca = _load_indices(pa_hbm, ia_smem, idx_sem.at[0], i, tm)
    cb = _load_indices(pb_hbm, ib_smem, idx_sem.at[1], i, tm)
    ca.start()
    cb.start()
    ca.wait()
    cb.wait()

    def body(r, carry):
        src = h_ref.at[pl.ds(r, 1), :]
        pltpu.make_async_copy(src, xs_hbm.at[pl.ds(ia_smem[r], 1), :], row_sem.at[0]).start()
        pltpu.make_async_copy(src, xs_hbm.at[pl.ds(ib_smem[r], 1), :], row_sem.at[1]).start()
        return carry
    lax.fori_loop(0, tm, body, 0, unroll=GATHER_UNROLL)
    pltpu.make_async_copy(h_ref, xs_hbm.at[pl.ds(0, tm), :], row_sem.at[0]).wait()
    pltpu.make_async_copy(h_ref, xs_hbm.at[pl.ds(0, tm), :], row_sem.at[1]).wait()


def _moe_group_kernel(te_ref, used_ref, xs_ref, gn_ref, wg_ref, wu_ref, wd_ref, y_ref, xn_scr):
    i = pl.program_id(0)
    j = pl.program_id(1)

    @pl.when(used_ref[i] == 0)
    def _():
        y_ref[...] = jnp.zeros(y_ref.shape, F32)

    @pl.when(used_ref[i] != 0)
    def _():
        @pl.when(j == 0)
        def _():
            xn_scr[...] = _rms(xs_ref[...], gn_ref[...]).astype(BF16)

        contrib = _swiglu(xn_scr[...], wg_ref.at[0], wu_ref.at[0], wd_ref.at[0])

        @pl.when(j == 0)
        def _():
            y_ref[...] = contrib

        @pl.when(j > 0)
        def _():
            y_ref[...] += contrib


def _moe_combine_kernel(pa_hbm, pb_hbm, y_hbm, h_ref, sel_ref, out_ref, ia_smem, ib_smem, buf_a, buf_b,
                        row_sem, idx_sem, *, tm):
    i = pl.program_id(0)
    ca = _load_indices(pa_hbm, ia_smem, idx_sem.at[0], i, tm)
    cb = _load_indices(pb_hbm, ib_smem, idx_sem.at[1], i, tm)
    ca.start()
    cb.start()
    ca.wait()
    cb.wait()

    def body(r, carry):
        pltpu.make_async_copy(y_hbm.at[pl.ds(ia_smem[r], 1), :], buf_a.at[pl.ds(r, 1), :], row_sem.at[0]).start()
        pltpu.make_async_copy(y_hbm.at[pl.ds(ib_smem[r], 1), :], buf_b.at[pl.ds(r, 1), :], row_sem.at[1]).start()
        return carry
    lax.fori_loop(0, tm, body, 0, unroll=GATHER_UNROLL)
    pltpu.make_async_copy(buf_a, buf_a, row_sem.at[0]).wait()
    pltpu.make_async_copy(buf_b, buf_b, row_sem.at[1]).wait()
    sel = sel_ref[...]
    out_ref[...] = h_ref[...] + (sel[:, 2:3] * buf_a[...] + sel[:, 3:4] * buf_b[...])


def moe_swiglu_top2(h2d, g_norm, w_router, w_gu, w_down, *, n_chunks=2):
    t_total, d = h2d.shape
    tm = MOE_TILE
    ff = w_down.shape[1]
    tf = ff // n_chunks
    sel = moe_route(h2d, g_norm, w_router)
    n_slots, tile_expert, tile_used, pos_a, pos_b = _moe_plan(sel, tm)
    any_spec = pl.BlockSpec(memory_space=pl.ANY)
    index_scratch = [pltpu.SMEM((tm,), jnp.int32), pltpu.SMEM((tm,), jnp.int32)]
    sems = [pltpu.SemaphoreType.DMA((2,)), pltpu.SemaphoreType.DMA((2,))]
    xs = pl.pallas_call(
        functools.partial(_moe_dispatch_kernel, tm=tm),
        grid=(t_total // tm,),
        in_specs=[any_spec, any_spec, pl.BlockSpec((tm, d), lambda i: (i, 0)), any_spec],
        out_specs=any_spec,
        out_shape=jax.ShapeDtypeStruct((n_slots, d), F32),
        scratch_shapes=index_scratch + sems,
        input_output_aliases={3: 0},
        compiler_params=_cparams(1),
        name="moe_dispatch",
    )(pos_a, pos_b, h2d, jnp.zeros((n_slots, d), F32))
    w_gu = w_gu.astype(BF16)
    row = lambda i, j, te, used: (i, 0)
    y = pl.pallas_call(
        _moe_group_kernel,
        grid_spec=pltpu.PrefetchScalarGridSpec(
            num_scalar_prefetch=2,
            grid=(n_slots // tm, n_chunks),
            in_specs=[
                pl.BlockSpec((tm, d), row),
                pl.BlockSpec((1, d), lambda i, j, te, used: (0, 0)),
                pl.BlockSpec((1, d, tf), lambda i, j, te, used: (te[i], 0, j)),
                pl.BlockSpec((1, d, tf), lambda i, j, te, used: (te[i], 0, j + n_chunks)),
                pl.BlockSpec((1, tf, d), lambda i, j, te, used: (te[i], j, 0)),
            ],
            out_specs=pl.BlockSpec((tm, d), row),
            scratch_shapes=[pltpu.VMEM((tm, d), BF16)],
        ),
        out_shape=jax.ShapeDtypeStruct((n_slots, d), F32),
        compiler_params=_cparams(2),
        name="moe_group_ffn",
    )(tile_expert, tile_used, xs, g_norm.reshape(1, d), w_gu, w_gu, w_down.astype(BF16))
    return pl.pallas_call(
        functools.partial(_moe_combine_kernel, tm=tm),
        grid=(t_total // tm,),
        in_specs=[any_spec, any_spec, any_spec, pl.BlockSpec((tm, d), lambda i: (i, 0)),
                  pl.BlockSpec((tm, LANES), lambda i: (i, 0))],
        out_specs=pl.BlockSpec((tm, d), lambda i: (i, 0)),
        out_shape=jax.ShapeDtypeStruct((t_total, d), F32),
        scratch_shapes=index_scratch + [pltpu.VMEM((tm, d), F32), pltpu.VMEM((tm, d), F32)] + sems,
        compiler_params=_cparams(1),
        name="moe_combine",
    )(pos_a, pos_b, y, h2d, sel)


B_HEADS = 16
B_KV_HEADS = 4
B_GROUP = 4
B_HEAD_DIM = 64
CMP_STRIDE = 16
CMP_BLOCK = 32
SEL_BLOCK = 32
N_SELECT = 8
WINDOW = 512
SEL_BIG = 1e9
TINY = 1e-30
N_KV_KINDS = 6
ONES_LANE = B_HEAD_DIM


def _group_rms_rows(x, gain_col, n_groups):
    r, c = x.shape
    x3 = x.reshape(n_groups, r // n_groups, c)
    y = x3 * lax.rsqrt(jnp.mean(x3 * x3, axis=1, keepdims=True) + NORM_EPS)
    return y.reshape(r, c) * gain_col


LOG2E = 1.4426950408889634
POS_RADIX = 64
N_SLOPE_PARTS = 3


def _alibi_key_part(pos, idx):
    hi = (pos // POS_RADIX * POS_RADIX).astype(F32)
    lo = (pos % POS_RADIX).astype(F32)
    return jnp.where(idx < N_SLOPE_PARTS, hi, jnp.where(idx < 2 * N_SLOPE_PARTS, lo, 0.0))


def _alibi_query_part(slopes):
    parts, rest = [], slopes.astype(F32)
    for _ in range(N_SLOPE_PARTS):
        p = rest.astype(BF16)
        parts.append(p)
        rest = rest - p.astype(F32)
    return jnp.stack(parts + parts, axis=-1)


def _ones_lane(shape):
    lane = lax.broadcasted_iota(jnp.int32, shape, len(shape) - 1)
    return jnp.where(lane % LANES >= ONES_LANE, 1.0, 0.0)


def _kv_proj_kernel(x_ref, gn_ref, wkt_ref, kgain_ref, wv_ref, wc_ref,
                    kst_ref, kwt_ref, vs_ref, vw_ref, kc_ref, vc_ref):
    g = B_KV_HEADS
    hn = _rms(x_ref[...], gn_ref[...]).astype(BF16)
    kt = _dot_nt(wkt_ref[...], hn)
    kt = _group_rms_rows(kt, kgain_ref[...], 2 * g)
    half = g * B_HEAD_DIM
    tm = kt.shape[1]
    pos = pl.program_id(1) * tm + lax.broadcasted_iota(jnp.int32, (LANES - B_HEAD_DIM, tm), 1)
    pos_rows = _alibi_key_part(pos, lax.broadcasted_iota(jnp.int32, pos.shape, 0))
    for j in range(g):
        kst_ref[0, j] = jnp.concatenate([kt[j * B_HEAD_DIM:(j + 1) * B_HEAD_DIM], pos_rows], axis=0).astype(BF16)
        kwt_ref[0, j] = jnp.concatenate([kt[half + j * B_HEAD_DIM:half + (j + 1) * B_HEAD_DIM], pos_rows],
                                        axis=0).astype(BF16)
    v = _dot(hn, wv_ref[...])
    v = (v + _ones_lane(v.shape)).astype(BF16)
    for j in range(g):
        vs_ref[0, j] = v[:, j * LANES:(j + 1) * LANES]
        vw_ref[0, j] = v[:, (g + j) * LANES:(g + j + 1) * LANES]
    c = _dot(hn, wc_ref[...])
    kc_ref[...] = c[:, :half].astype(BF16)
    vc_ref[...] = c[:, half:].astype(BF16)


def nsa_kv_proj(h2d, bsz, seq, kv_norm, kv_w, k_norm, *, tm=512):
    t_total, d = h2d.shape
    g, dh = B_KV_HEADS, B_HEAD_DIM
    n_sblk = seq // tm
    w = kv_w.reshape(d, N_KV_KINDS, g, dh)
    wkt = jnp.concatenate([w[:, 2].reshape(d, g * dh), w[:, 4].reshape(d, g * dh)], axis=1).T.astype(BF16)
    kgain = jnp.concatenate([jnp.tile(k_norm[1], g), jnp.tile(k_norm[2], g)]).reshape(2 * g * dh, 1)
    wv = jnp.stack([w[:, 3], w[:, 5]], axis=1)
    wv = jnp.pad(wv, ((0, 0), (0, 0), (0, 0), (0, LANES - dh))).reshape(d, 2 * g * LANES).astype(BF16)
    wc = jnp.concatenate([w[:, 0].reshape(d, g * dh), w[:, 1].reshape(d, g * dh)], axis=1).astype(BF16)
    const2 = lambda b, s: (0, 0)
    row = lambda b, s: (b * n_sblk + s, 0)
    kt_spec = pl.BlockSpec((1, g, LANES, tm), lambda b, s: (b, 0, 0, s))
    v_spec = pl.BlockSpec((1, g, tm, LANES), lambda b, s: (b, 0, s, 0))
    kt_shape = jax.ShapeDtypeStruct((bsz, g, LANES, seq), BF16)
    v_shape = jax.ShapeDtypeStruct((bsz, g, seq, LANES), BF16)
    c_shape = jax.ShapeDtypeStruct((t_total, g * dh), BF16)
    return pl.pallas_call(
        _kv_proj_kernel,
        grid=(bsz, n_sblk),
        in_specs=[
            pl.BlockSpec((tm, d), row),
            _resident((1, d), const2),
            _resident((2 * g * dh, d), const2),
            _resident((2 * g * dh, 1), const2),
            _resident((d, 2 * g * LANES), const2),
            _resident((d, 2 * g * dh), const2),
        ],
        out_specs=[kt_spec, kt_spec, v_spec, v_spec,
                   pl.BlockSpec((tm, g * dh), row), pl.BlockSpec((tm, g * dh), row)],
        out_shape=[kt_shape, kt_shape, v_shape, v_shape, c_shape, c_shape],
        compiler_params=_cparams(2),
        name="nsa_kv_proj",
    )(h2d, kv_norm.reshape(1, d), wkt, kgain, wv, wc)


def _compress_kernel(ak_ref, av_ref, posk_ref, posv_ref, w1k_ref, w1v_ref, w2k_ref, w2v_ref, kgain_ref,
                     kca_ref, vca_ref, *, n_half):
    half_w = ak_ref.shape[1]

    def hidden(a_ref, pos_ref, w1_ref):
        a = a_ref[...]
        top = _dot(a, w1_ref[:half_w, :])
        bot = _dot(a, w1_ref[half_w:, :])
        bias = _dot(pos_ref[...], w1_ref[...])[0:1, :]
        pre = top + pltpu.roll(bot, shift=a.shape[0] - 1, axis=0) + bias
        return (pre * _sigmoid(pre)).astype(BF16)

    act_k = hidden(ak_ref, posk_ref, w1k_ref)
    kc = _dot(act_k, w2k_ref[...])
    ms = jnp.sum(kc * kc, axis=-1, keepdims=True) * (1.0 / B_HEAD_DIM)
    kc = kc * lax.rsqrt(ms + NORM_EPS) * kgain_ref[...]
    blk = lax.broadcasted_iota(jnp.int32, kc.shape, 0) % n_half
    lane = lax.broadcasted_iota(jnp.int32, kc.shape, 1)
    c_end = blk * CMP_STRIDE + (CMP_BLOCK - 1)
    kca_ref[...] = (kc + jnp.where(lane >= B_HEAD_DIM, _alibi_key_part(c_end, lane - B_HEAD_DIM), 0.0)).astype(BF16)
    act_v = hidden(av_ref, posv_ref, w1v_ref)
    vca = _dot(act_v, w2v_ref[...])
    vca_ref[...] = (vca + _ones_lane(vca.shape)).astype(BF16)


def nsa_compress(kc_raw, vc_raw, bsz, seq, pos_k, w1_k, w2_k, pos_v, w1_v, w2_v, k_gain, *, groups_per_step=4):
    g, dh = B_KV_HEADS, B_HEAD_DIM
    n_half = seq // CMP_STRIDE
    feat = CMP_STRIDE * dh

    def to_half_blocks(a):
        a = a.reshape(bsz, n_half, CMP_STRIDE, g, dh).transpose(0, 3, 1, 2, 4)
        return a.reshape(bsz * g * n_half, feat)

    def pos_rows(pos):
        return jnp.zeros((8, 2 * feat), F32).at[0].set(pos.reshape(-1)).astype(BF16)

    hid = w1_k.shape[1]
    pad_lanes = lambda a: jnp.pad(a, ((0, 0), (0, LANES - dh)))
    rows = groups_per_step * n_half
    const2 = lambda i: (0, 0)
    return pl.pallas_call(
        functools.partial(_compress_kernel, n_half=n_half),
        grid=(bsz * g // groups_per_step,),
        in_specs=[
            pl.BlockSpec((rows, feat), lambda i: (i, 0)),
            pl.BlockSpec((rows, feat), lambda i: (i, 0)),
            _resident((8, 2 * feat), const2),
            _resident((8, 2 * feat), const2),
            _resident((2 * feat, hid), const2),
            _resident((2 * feat, hid), const2),
            _resident((hid, LANES), const2),
            _resident((hid, LANES), const2),
            _resident((1, LANES), const2),
        ],
        out_specs=[pl.BlockSpec((rows, LANES), lambda i: (i, 0)),
                   pl.BlockSpec((rows, LANES), lambda i: (i, 0))],
        out_shape=[jax.ShapeDtypeStruct((bsz * g * n_half, LANES), BF16),
                   jax.ShapeDtypeStruct((bsz * g * n_half, LANES), BF16)],
        compiler_params=_cparams(1),
        name="nsa_compress",
    )(to_half_blocks(kc_raw), to_half_blocks(vc_raw), pos_rows(pos_k), pos_rows(pos_v),
      w1_k.astype(BF16), w1_v.astype(BF16), pad_lanes(w2_k).astype(BF16), pad_lanes(w2_v).astype(BF16),
      pad_lanes(k_gain.reshape(1, dh)))


def _q_proj_kernel(x_ref, gn_ref, w_ref, qgain_ref, qconst_ref, q_ref, gate_ref):
    d_q = q_ref.shape[1]
    xn = _rms(x_ref[...], gn_ref[...]).astype(BF16)
    proj = _dot(xn, w_ref[...])
    ones = jnp.ones((LANES, LANES), BF16)
    for h in range(d_q // LANES):
        lanes = slice(h * LANES, (h + 1) * LANES)
        q = proj[:, lanes]
        sq_hi, sq_lo = _split_bf16(q * q)
        ms = (_dot(sq_hi, ones) + _dot(sq_lo, ones)) * (1.0 / B_HEAD_DIM)
        q_ref[:, lanes] = (q * lax.rsqrt(ms + NORM_EPS) * qgain_ref[:, lanes] + qconst_ref[:, lanes]).astype(BF16)
    gate_ref[...] = _sigmoid(proj[:, d_q:])


def nsa_q_proj(h2d, g_norm, w_in, q_norm, *, tm=512):
    t_total, d = h2d.shape
    g, hpg, dh = B_KV_HEADS, B_GROUP, B_HEAD_DIM
    d_q = B_HEADS * LANES
    wq = jnp.pad(w_in[:, :B_HEADS * dh].reshape(d, B_HEADS, dh), ((0, 0), (0, 0), (0, LANES - dh))).reshape(d, d_q)
    wg = w_in[:, B_HEADS * dh:].reshape(d, g, hpg * 3)
    wg = jnp.pad(wg, ((0, 0), (0, 0), (0, LANES - hpg * 3))).reshape(d, g * LANES)
    w = jnp.concatenate([wq, wg], axis=1).astype(BF16)
    qgain = jnp.tile(jnp.pad(q_norm * (dh ** -0.5 * LOG2E), (0, LANES - dh)), B_HEADS).reshape(1, d_q)
    slopes = np.array([2.0 ** (-8.0 * (h + 1) / B_HEADS) for h in range(B_HEADS)], dtype=np.float64) * LOG2E
    feats = _alibi_query_part(jnp.asarray(slopes, F32)).astype(F32)
    qconst = jnp.pad(feats, ((0, 0), (dh, LANES - dh - feats.shape[1]))).reshape(1, d_q)
    const2 = lambda i: (0, 0)
    return pl.pallas_call(
        _q_proj_kernel,
        grid=(t_total // tm,),
        in_specs=[
            pl.BlockSpec((tm, d), lambda i: (i, 0)),
            _resident((1, d), const2),
            _resident((d, d_q + g * LANES), const2),
            _resident((1, d_q), const2),
            _resident((1, d_q), const2),
        ],
        out_specs=[pl.BlockSpec((tm, d_q), lambda i: (i, 0)), pl.BlockSpec((tm, g * LANES), lambda i: (i, 0))],
        out_shape=[jax.ShapeDtypeStruct((t_total, d_q), BF16), jax.ShapeDtypeStruct((t_total, g * LANES), F32)],
        compiler_params=_cparams(1),
        name="nsa_q_proj",
    )(h2d, g_norm.reshape(1, d), w, qgain, qconst)


def _nsa_attn_kernel(q_ref, gate_ref, kca_ref, vca_ref, kst_ref, vs_ref, kwt_ref, vw_ref,
                     ovlt_ref, esel_ref, o_ref, live_smem, m_scr, acc_scr, *, tq, tk):
    hpg, dh = B_GROUP, B_HEAD_DIM
    t0 = pl.program_id(2) * tq
    rows = hpg * tq
    qb = q_ref[...]
    q_st = jnp.concatenate([qb[:, i * LANES:(i + 1) * LANES] for i in range(hpg)], axis=0)
    t_q = t0 + lax.broadcasted_iota(jnp.int32, (tq, 1), 0)

    n_half = kca_ref.shape[0]
    c_end = lax.broadcasted_iota(jnp.int32, (n_half, 1), 0) * CMP_STRIDE + (CMP_BLOCK - 1)
    t_lane = t0 + lax.broadcasted_iota(jnp.int32, (1, rows), 1) % tq
    seen = t_lane >= c_end
    s = jnp.where(seen, _dot_nt(kca_ref[...], q_st), NEG)
    m = jnp.max(s, axis=0, keepdims=True)
    e = jnp.where(seen, jnp.exp2(s - m), 0.0)
    p = e / jnp.maximum(jnp.sum(e, axis=0, keepdims=True), TINY)
    o_cmp = _dot_tn(p.astype(BF16), vca_ref[...])[:, :dh]

    p_grp = p[:, 0:tq]
    for i in range(1, hpg):
        p_grp = p_grp + p[:, i * tq:(i + 1) * tq]
    p_hi, p_lo = _split_bf16(p_grp)
    n_blk = LANES // 2
    imp = (_dot(ovlt_ref[...], p_hi) + _dot(ovlt_ref[...], p_lo))[:n_blk]
    blk = lax.broadcasted_iota(jnp.int32, (n_blk, tq), 0)
    t_l = t0 + lax.broadcasted_iota(jnp.int32, (1, tq), 1)
    valid = blk * SEL_BLOCK <= t_l
    cur = t_l // SEL_BLOCK
    forced = valid & ((blk == 0) | (blk == cur) | (blk == cur - 1))
    work = jnp.where(forced, SEL_BIG, jnp.where(valid, imp, -SEL_BIG))
    blk_f = blk.astype(F32)
    chosen = jnp.zeros((n_blk, tq), F32)
    for _ in range(N_SELECT):
        top = jnp.max(work, axis=0, keepdims=True)
        first = jnp.min(jnp.where(work == top, blk_f, float(LANES)), axis=0, keepdims=True)
        hit = blk_f == first
        chosen = jnp.where(hit & (top > -0.5 * SEL_BIG), 1.0, chosen)
        work = jnp.where(hit, -3e38, work)
    blocks_per_tile = tk // SEL_BLOCK
    any_q = jnp.max(chosen, axis=1, keepdims=True)
    for kt in range(n_blk // blocks_per_tile):
        tile_any = jnp.max(any_q[kt * blocks_per_tile:(kt + 1) * blocks_per_tile])
        live_smem[kt] = (tile_any > 0.5).astype(jnp.int32)
    chosen = jnp.concatenate([chosen, jnp.zeros((LANES - n_blk, tq), F32)], axis=0).T.astype(BF16)

    def normalised(acc):
        return (acc / pltpu.roll(acc, shift=LANES - ONES_LANE, axis=1))[:, :dh]

    span = WINDOW + tq
    w0 = pl.multiple_of(jnp.maximum(t0 - WINDOW, 0), LANES)
    dist = t_q - (w0 + lax.broadcasted_iota(jnp.int32, (1, span), 1))
    pen = jnp.where((dist >= 0) & (dist < WINDOW), 0.0, NEG)
    s = _dot(q_st, kwt_ref[0, 0, :, pl.ds(w0, span)]) + jnp.concatenate([pen] * hpg, axis=0)
    m = jnp.max(s, axis=-1, keepdims=True)
    o_win = normalised(_dot(jnp.exp2(s - m).astype(BF16), vw_ref[0, 0, pl.ds(w0, span), :]))

    m_scr[...] = jnp.full(m_scr.shape, NEG, F32)
    acc_scr[...] = jnp.zeros(acc_scr.shape, F32)

    def sel_step(kt, carry):
        @pl.when(live_smem[kt] != 0)
        def _():
            k0 = pl.multiple_of(kt * tk, tk)
            pos = k0 + lax.broadcasted_iota(jnp.int32, (1, tk), 1)
            picked = _dot(chosen, esel_ref[:, pl.ds(k0, tk)])
            pen = jnp.where((picked > 0.5) & (pos <= t_q), 0.0, NEG)
            sc = _dot(q_st, kst_ref[0, 0, :, pl.ds(k0, tk)]) + jnp.concatenate([pen] * hpg, axis=0)
            m_run = m_scr[...]
            m_new = jnp.maximum(m_run, jnp.broadcast_to(jnp.max(sc, axis=-1, keepdims=True), m_run.shape))
            pr = jnp.exp2(sc - jnp.concatenate([m_new] * (tk // LANES), axis=1)).astype(BF16)
            acc_scr[...] = jnp.exp2(m_run - m_new) * acc_scr[...] + _dot(pr, vs_ref[0, 0, pl.ds(k0, tk), :])
            m_scr[...] = m_new
        return carry

    lax.fori_loop(0, (t0 + tq + tk - 1) // tk, sel_step, 0)
    o_sel = normalised(acc_scr[...])

    gate = gate_ref[...]
    outs = []
    for i in range(hpg):
        r = slice(i * tq, (i + 1) * tq)
        outs.append(gate[:, 3 * i:3 * i + 1] * o_cmp[r] + gate[:, 3 * i + 1:3 * i + 2] * o_sel[r]
                    + gate[:, 3 * i + 2:3 * i + 3] * o_win[r])
    o_ref[...] = jnp.concatenate(outs, axis=1).astype(BF16)


def nsa_attention(q, gates, kca, vca, kst, vs, kwt, vw, bsz, seq, *, tq=256, tk=512):
    g, hpg, dh = B_KV_HEADS, B_GROUP, B_HEAD_DIM
    n_q = seq // tq
    n_half = seq // CMP_STRIDE
    n_cmp = n_half - 1
    n_sel = seq // SEL_BLOCK
    assert n_sel <= LANES // 2 and tq % LANES == 0
    c_start = np.arange(n_half) * CMP_STRIDE
    j_sel = np.arange(LANES)
    ovl = ((c_start[:, None] < (j_sel[None, :] + 1) * SEL_BLOCK)
           & (c_start[:, None] + CMP_BLOCK > j_sel[None, :] * SEL_BLOCK)
           & (np.arange(n_half)[:, None] < n_cmp) & (j_sel[None, :] < n_sel))
    esel = (np.arange(seq)[None, :] // SEL_BLOCK) == j_sel[:, None]
    row = lambda b, j, i: (b * n_q + i, j)
    per_bg = lambda b, j, i: (b, j, 0, 0)
    return pl.pallas_call(
        functools.partial(_nsa_attn_kernel, tq=tq, tk=tk),
        grid=(bsz, g, n_q),
        in_specs=[
            pl.BlockSpec((tq, hpg * LANES), row),
            pl.BlockSpec((tq, LANES), row),
            pl.BlockSpec((n_half, LANES), lambda b, j, i: (b * g + j, 0)),
            pl.BlockSpec((n_half, LANES), lambda b, j, i: (b * g + j, 0)),
            pl.BlockSpec((1, 1, LANES, seq), per_bg),
            pl.BlockSpec((1, 1, seq, LANES), per_bg),
            pl.BlockSpec((1, 1, LANES, seq), per_bg),
            pl.BlockSpec((1, 1, seq, LANES), per_bg),
            _resident((LANES, n_half), lambda b, j, i: (0, 0)),
            _resident((LANES, seq), lambda b, j, i: (0, 0)),
        ],
        out_specs=pl.BlockSpec((tq, hpg * dh), row),
        out_shape=jax.ShapeDtypeStruct((bsz * seq, B_HEADS * dh), BF16),
        scratch_shapes=[pltpu.SMEM((seq // tk,), jnp.int32), pltpu.VMEM((hpg * tq, LANES), F32),
                        pltpu.VMEM((hpg * tq, LANES), F32)],
        compiler_params=_cparams(3),
        name="nsa_attention",
    )(q, gates, kca, vca, kst, vs, kwt, vw, jnp.asarray(ovl.T, BF16), jnp.asarray(esel, BF16))


def _out_proj_kernel(h_ref, o_ref, w_ref, out_ref):
    out_ref[...] = h_ref[...] + _dot(o_ref[...], w_ref[...])


def out_proj_residual(h2d, o, w_out, *, tm=512):
    t_total, d = h2d.shape
    k = o.shape[1]
    return pl.pallas_call(
        _out_proj_kernel,
        grid=(t_total // tm,),
        in_specs=[pl.BlockSpec((tm, d), lambda i: (i, 0)), pl.BlockSpec((tm, k), lambda i: (i, 0)),
                  _resident((k, d), lambda i: (0, 0))],
        out_specs=pl.BlockSpec((tm, d), lambda i: (i, 0)),
        out_shape=jax.ShapeDtypeStruct((t_total, d), F32),
        compiler_params=_cparams(1),
        name="out_proj_residual",
    )(h2d, o, w_out.astype(BF16))


def nsa_shared_kv(h2d, bsz, seq, kv_norm, kv_w, cmp_pos_k, cmp_w1_k, cmp_w2_k, cmp_pos_v, cmp_w1_v, cmp_w2_v, k_norm):
    kst, kwt, vs, vw, kc_raw, vc_raw = nsa_kv_proj(h2d, bsz, seq, kv_norm, kv_w, k_norm)
    kct, vca = nsa_compress(kc_raw, vc_raw, bsz, seq, cmp_pos_k, cmp_w1_k, cmp_w2_k,
                            cmp_pos_v, cmp_w1_v, cmp_w2_v, k_norm[0])
    return kct, vca, kst, vs, kwt, vw


def nsa_mixer(h2d, bsz, seq, g_norm, w_in, w_out, q_norm, kv):
    q, gates = nsa_q_proj(h2d, g_norm, w_in, q_norm)
    o = nsa_attention(q, gates, *kv, bsz, seq)
    return out_proj_residual(h2d, o, w_out)


def kernel(x, norm_mix, norm_ffn, hgrn_w_in, hgrn_w_out, hgrn_g_norm, hgrn_lb_logits, kv_norm, kv_w, cmp_pos_k, cmp_w1_k, cmp_w2_k, cmp_pos_v, cmp_w1_v, cmp_w2_v, k_norm, nsa_w_in, nsa_w_out, nsa_q_norm, ffn_w_gu, ffn_w_down, moe_router, moe_w_gu, moe_w_down):
    bsz, seq, d = x.shape
    lb = jnp.cumsum(jax.nn.softmax(hgrn_lb_logits.astype(F32), axis=0), axis=0)
    lb = lb - lb[0:1]
    n_a = hgrn_w_in.shape[0]
    depth = norm_mix.shape[0]
    h = x.reshape(bsz * seq, d)
    kv = None
    for layer in range(depth):
        if layer < n_a:
            h = hgrn_mixer(h, seq, norm_mix[layer], hgrn_w_in[layer], hgrn_w_out[layer], hgrn_g_norm[layer],
                           lb[layer])
        else:
            if kv is None:
                kv = nsa_shared_kv(h, bsz, seq, kv_norm, kv_w, cmp_pos_k, cmp_w1_k, cmp_w2_k,
                                   cmp_pos_v, cmp_w1_v, cmp_w2_v, k_norm)
            j = layer - n_a
            h = nsa_mixer(h, bsz, seq, norm_mix[layer], nsa_w_in[j], nsa_w_out[j], nsa_q_norm[j], kv)
        if layer % 2 == 0:
            h = ffn_swiglu(h, norm_ffn[layer], ffn_w_gu[layer // 2], ffn_w_down[layer // 2])
        else:
            h = moe_swiglu_top2(h, norm_ffn[layer], moe_router[layer // 2], moe_w_gu[layer // 2],
                                moe_w_down[layer // 2])
    return h.reshape(bsz, seq, d)
```

```python
import functools

import numpy as np
import jax
import jax.numpy as jnp
from jax import lax
from jax.experimental import pallas as pl
from jax.experimental.pallas import tpu as pltpu

F32 = jnp.float32
BF16 = jnp.bfloat16

NORM_EPS = 1e-6
VMEM_LIMIT_BYTES = 56 * 1024 * 1024

A_HEADS = 8
A_HEAD_DIM = 128
SCAN_ROWS = 128
SUBLANES = 8


def _cparams(n_axes):
    return pltpu.CompilerParams(dimension_semantics=("arbitrary",) * n_axes,
                                vmem_limit_bytes=VMEM_LIMIT_BYTES)


def _dot(a, b):
    return jnp.dot(a, b, preferred_element_type=F32)


def _dot_nt(a, b):
    return lax.dot_general(a, b, (((1,), (1,)), ((), ())), preferred_element_type=F32)


def _dot_tn(a, b):
    return lax.dot_general(a, b, (((0,), (0,)), ((), ())), preferred_element_type=F32)


def _sigmoid(x):
    return 1.0 / (1.0 + jnp.exp(-x))


def _rms(x, g):
    return x * lax.rsqrt(jnp.mean(x * x, axis=-1, keepdims=True) + NORM_EPS) * g


def _split_bf16(x):
    hi = x.astype(BF16)
    lo = (x - hi.astype(F32)).astype(BF16)
    return hi, lo


def _hgrn_constants():
    n = SCAN_ROWS
    t = np.arange(n)
    r = np.arange(n)[None, :]
    sums, masks = [], []
    m = n // 2
    while m >= 1:
        grp = t // (2 * m)
        mid = grp * 2 * m + m
        upper = (t % (2 * m)) >= m
        if m < SUBLANES:
            sums.append(np.where(upper[:, None], (r >= mid[:, None]) & (r <= t[:, None]),
                                 (r > t[:, None]) & (r < mid[:, None])))
        masks.append((grp[:, None] == grp[None, :]) & upper[:, None] & (~upper[None, :]))
        m //= 2
    masks.append(np.eye(n, dtype=bool))
    sums.append(r <= t[:, None])
    w = np.concatenate(sums, 0).astype(np.float32)
    return np.concatenate([w, w], 1), np.stack(masks).astype(np.float32)


def _hgrn_kernel(x_ref, gn_ref, win_ref, lb_ref, ghn_ref, wout_ref, wsum_ref, msk_ref,
                 out_ref, xn_scr, st_scr, og_scr, *, tm):
    sblk = pl.program_id(1)
    h = pl.program_id(2)
    n = SCAN_ROWS
    n_lvl = msk_ref.shape[0] - 1

    @pl.when(h == 0)
    def _():
        xn_scr[...] = _rms(x_ref[...], gn_ref[...]).astype(BF16)

    @pl.when(sblk == 0)
    def _():
        st_scr[h] = jnp.zeros((A_HEAD_DIM, A_HEAD_DIM), F32)

    proj = _dot(xn_scr[...], win_ref[...])
    q = proj[:, 0:128]
    f = proj[:, 128:256]
    v = proj[:, 256:384]
    g = proj[:, 384:512]
    lb = lb_ref[0:1, :]
    log_lb = lb_ref[1:2, :]
    log_1m_lb = lb_ref[2:3, :]
    qs = q * _sigmoid(q)
    log_sig = jnp.minimum(f, 0.0) - jnp.log1p(jnp.exp(-jnp.abs(f)))
    b = log_1m_lb + log_sig
    logf = jnp.maximum(log_lb, b) + jnp.log1p(jnp.exp(-jnp.abs(log_lb - b)))
    key = (1.0 - lb) * _sigmoid(-f)

    wide = [n >> (l + 1) for l in range(n_lvl) if n >> (l + 1) >= SUBLANES]
    row_id = lax.broadcasted_iota(jnp.int32, (n, A_HEAD_DIM), 0)
    in_upper_half = [(row_id % (2 * m)) >= m for m in wide]

    st = st_scr[h]
    outs = []
    for r in range(tm // n):
        rows = slice(r * n, (r + 1) * n)
        hi, lo = _split_bf16(logf[rows])
        sums = _dot(wsum_ref[...], jnp.concatenate([hi, lo], axis=0))
        cum = sums[(n_lvl - len(wide)) * n:]
        e_lvl = []
        for m, upper in zip(wide, in_upper_half):
            bound = jnp.concatenate([jnp.broadcast_to(cum[i + m - 1:i + m], (2 * m, A_HEAD_DIM))
                                     for i in range(0, n, 2 * m)], axis=0)
            e_lvl.append(jnp.exp(jnp.where(upper, cum - bound, bound - cum)))
        e_sums = jnp.exp(sums)
        e_lvl += [e_sums[i * n:(i + 1) * n] for i in range(n_lvl - len(wide))]
        e_prefix = e_sums[(n_lvl - len(wide)) * n:]
        e_suffix = jnp.exp(cum[n - 1:n] - cum)
        qb = qs[rows]
        kb = key[rows]
        vb = v[rows].astype(BF16)
        s = msk_ref[n_lvl] * _dot_nt(qb.astype(BF16), kb.astype(BF16))
        for l in range(n_lvl):
            s = s + msk_ref[l] * _dot_nt((qb * e_lvl[l]).astype(BF16), (kb * e_lvl[l]).astype(BF16))
        o = _dot(s.astype(BF16), vb) + _dot_nt((qb * e_prefix).astype(BF16), st.astype(BF16))
        st = st * e_prefix[n - 1:n, :] + _dot_tn(vb, (kb * e_suffix).astype(BF16))
        outs.append(o)
    st_scr[h] = st
    o = jnp.concatenate(outs, axis=0)

    og_scr[:, pl.ds(pl.multiple_of(h * A_HEAD_DIM, A_HEAD_DIM), A_HEAD_DIM)] = (
        _rms(o, ghn_ref[...]) * (g * _sigmoid(g))).astype(BF16)

    @pl.when(h == A_HEADS - 1)
    def _():
        out_ref[...] = x_ref[...] + _dot(og_scr[...], wout_ref[...])


def hgrn_mixer(h2d, seq, g_norm_in, w_in, w_out, g_head, lb, *, tm=512):
    t_total, d = h2d.shape
    n_sblk = seq // tm
    hd = A_HEAD_DIM
    w_perm = w_in.reshape(d, 4, A_HEADS, hd).transpose(0, 2, 1, 3).reshape(d, 4 * A_HEADS * hd).astype(BF16)
    lb_rows = jnp.zeros((8, A_HEADS * hd), F32)
    lb_rows = lb_rows.at[0].set(lb).at[1].set(jnp.log(lb)).at[2].set(jnp.log1p(-lb))
    wsum, masks = _hgrn_constants()
    n_blk = wsum.shape[0]
    grid = (t_total // seq, n_sblk, A_HEADS)
    row = lambda b, s, h: (b * n_sblk + s, 0)
    const2 = lambda b, s, h: (0, 0)
    return pl.pallas_call(
        functools.partial(_hgrn_kernel, tm=tm),
        grid=grid,
        in_specs=[
            pl.BlockSpec((tm, d), row),
            pl.BlockSpec((1, d), const2),
            pl.BlockSpec((d, 4 * hd), lambda b, s, h: (0, h)),
            pl.BlockSpec((8, hd), lambda b, s, h: (0, h)),
            pl.BlockSpec((1, hd), const2),
            _resident((A_HEADS * hd, d), const2),
            _resident((n_blk, 2 * SCAN_ROWS), const2),
            _resident(masks.shape, lambda b, s, h: (0, 0, 0)),
        ],
        out_specs=pl.BlockSpec((tm, d), row),
        out_shape=jax.ShapeDtypeStruct((t_total, d), F32),
        scratch_shapes=[pltpu.VMEM((tm, d), BF16), pltpu.VMEM((A_HEADS, hd, hd), F32),
                        pltpu.VMEM((tm, A_HEADS * hd), BF16)],
        compiler_params=_cparams(3),
        name="hgrn_mixer",
    )(h2d, g_norm_in.reshape(1, d), w_perm, lb_rows, g_head.reshape(1, hd), w_out.astype(BF16),
      jnp.asarray(wsum, BF16), jnp.asarray(masks, F32))


def _resident(block_shape, index_map):
    return pl.BlockSpec(block_shape, index_map, pipeline_mode=pl.Buffered(1))


MXU_TILE = 256


def _swiglu(xn, wg_ref, wu_ref, wd_ref):
    acc = None
    for c in range(wg_ref.shape[1] // MXU_TILE):
        cols = slice(c * MXU_TILE, (c + 1) * MXU_TILE)
        gate = _dot(xn, wg_ref[:, cols])
        up = _dot(xn, wu_ref[:, cols])
        part = _dot((gate * _sigmoid(gate) * up).astype(BF16), wd_ref[cols, :])
        acc = part if acc is None else acc + part
    return acc


def _ffn_kernel(x_ref, gn_ref, wg_ref, wu_ref, wd_ref, out_ref):
    x = x_ref[...]
    xn = _rms(x, gn_ref[...]).astype(BF16)
    out_ref[...] = x + _swiglu(xn, wg_ref, wu_ref, wd_ref)


def ffn_swiglu(h2d, g_norm, w_gu, w_down, *, tm=512):
    t_total, d = h2d.shape
    ff = w_down.shape[0]
    w_gu = w_gu.astype(BF16)
    const2 = lambda i: (0, 0)
    return pl.pallas_call(
        _ffn_kernel,
        grid=(t_total // tm,),
        in_specs=[
            pl.BlockSpec((tm, d), lambda i: (i, 0)),
            _resident((1, d), const2),
            _resident((d, ff), const2),
            _resident((d, ff), lambda i: (0, 1)),
            _resident((ff, d), const2),
        ],
        out_specs=pl.BlockSpec((tm, d), lambda i: (i, 0)),
        out_shape=jax.ShapeDtypeStruct((t_total, d), F32),
        compiler_params=_cparams(1),
        name="ffn_swiglu",
    )(h2d, g_norm.reshape(1, d), w_gu, w_gu, w_down.astype(BF16))


N_EXPERTS = 8
LANES = 128
NEG = -1e30


def _router_kernel(x_ref, gn_ref, wh_ref, wl_ref, sel_ref):
    xn = _rms(x_ref[...], gn_ref[...])
    xh, xl = _split_bf16(xn)
    logits = _dot(xh, wh_ref[...]) + (_dot(xl, wh_ref[...]) + _dot(xh, wl_ref[...]))
    lane = lax.broadcasted_iota(jnp.int32, logits.shape, 1).astype(F32)
    logits = jnp.where(lane < N_EXPERTS, logits, NEG)
    m1 = jnp.max(logits, axis=-1, keepdims=True)
    i1 = jnp.min(jnp.where(logits == m1, lane, float(LANES)), axis=-1, keepdims=True)
    rest = jnp.where(lane == i1, NEG, logits)
    m2 = jnp.max(rest, axis=-1, keepdims=True)
    i2 = jnp.min(jnp.where(rest == m2, lane, float(LANES)), axis=-1, keepdims=True)
    e2 = jnp.exp(m2 - m1)
    denom = 1.0 + e2
    sel_ref[...] = jnp.where(lane == 0, i1, jnp.where(lane == 1, i2, jnp.where(
        lane == 2, 1.0 / denom, jnp.where(lane == 3, e2 / denom, 0.0))))


def moe_route(h2d, g_norm, w_router, *, tm=512):
    t_total, d = h2d.shape
    w_pad = jnp.zeros((d, LANES), F32).at[:, :N_EXPERTS].set(w_router)
    w_hi = w_pad.astype(BF16)
    w_lo = (w_pad - w_hi.astype(F32)).astype(BF16)
    const2 = lambda i: (0, 0)
    return pl.pallas_call(
        _router_kernel,
        grid=(t_total // tm,),
        in_specs=[
            pl.BlockSpec((tm, d), lambda i: (i, 0)),
            _resident((1, d), const2),
            _resident((d, LANES), const2),
            _resident((d, LANES), const2),
        ],
        out_specs=pl.BlockSpec((tm, LANES), lambda i: (i, 0)),
        out_shape=jax.ShapeDtypeStruct((t_total, LANES), F32),
        compiler_params=_cparams(1),
        name="moe_router",
    )(h2d, g_norm.reshape(1, d), w_hi, w_lo)


MOE_TILE = 1024
GATHER_UNROLL = 8


def _moe_plan(sel, tm):
    t_total = sel.shape[0]
    e = sel[:, :2].astype(jnp.int32)
    onehot = (e[:, :, None] == jnp.arange(N_EXPERTS, dtype=jnp.int32)[None, None, :]).astype(jnp.int32).sum(1)
    csum = jnp.cumsum(onehot, axis=0)
    rank = csum - onehot
    padded = (csum[-1] + tm - 1) // tm * tm
    ends = jnp.cumsum(padded)
    pos = (ends - padded)[e] + jnp.take_along_axis(rank, e, axis=1)
    n_slots = 2 * t_total + N_EXPERTS * tm
    tile_start = jnp.arange(n_slots // tm, dtype=jnp.int32) * tm
    tile_expert = jnp.minimum(jnp.searchsorted(ends, tile_start, side="right"), N_EXPERTS - 1).astype(jnp.int32)
    tile_used = (tile_start < ends[-1]).astype(jnp.int32)
    return n_slots, tile_expert, tile_used, pos[:, 0], pos[:, 1]


def _load_indices(idx_hbm, idx_smem, sem, tile, tm):
    return pltpu.make_async_copy(idx_hbm.at[pl.ds(tile * tm, tm)], idx_smem, sem)


def _moe_dispatch_kernel(pa_hbm, pb_hbm, h_ref, zeros_hbm, xs_hbm, ia_smem, ib_smem, row_sem, idx_sem, *, tm):
    del zeros_hbm
    i = pl.program_id(0)
    ca = _load_indices(pa_hbm, ia_smem, idx_sem.at[0], i, tm)
    cb = _load_indices(pb_hbm, ib_smem, idx_sem.at[1], i, tm)
    ca.start()
    cb.start()
    ca.wait()
    cb.wait()

    def body(r, carry):
        src = h_ref.at[pl.ds(r, 1), :]
        pltpu.make_async_copy(src, xs_hbm.at[pl.ds(ia_smem[r], 1), :], row_sem.at[0]).start()
        pltpu.make_async_copy(src, xs_hbm.at[pl.ds(ib_smem[r], 1), :], row_sem.at[1]).start()
        return carry
    lax.fori_loop(0, tm, body, 0, unroll=GATHER_UNROLL)
    pltpu.make_async_copy(h_ref, xs_hbm.at[pl.ds(0, tm), :], row_sem.at[0]).wait()
    pltpu.make_async_copy(h_ref, xs_hbm.at[pl.ds(0, tm), :], row_sem.at[1]).wait()


def _moe_group_kernel(te_ref, used_ref, xs_ref, gn_ref, wg_ref, wu_ref, wd_ref, y_ref, xn_scr):
    i = pl.program_id(0)
    j = pl.program_id(1)

    @pl.when(used_ref[i] == 0)
    def _():
        y_ref[...] = jnp.zeros(y_ref.shape, F32)

    @pl.when(used_ref[i] != 0)
    def _():
        @pl.when(j == 0)
        def _():
            xn_scr[...] = _rms(xs_ref[...], gn_ref[...]).astype(BF16)

        contrib = _swiglu(xn_scr[...], wg_ref.at[0], wu_ref.at[0], wd_ref.at[0])

        @pl.when(j == 0)
        def _():
            y_ref[...] = contrib

        @pl.when(j > 0)
        def _():
            y_ref[...] += contrib


def _moe_combine_kernel(pa_hbm, pb_hbm, y_hbm, h_ref, sel_ref, out_ref, ia_smem, ib_smem, buf_a, buf_b,
                        row_sem, idx_sem, *, tm):
    i = pl.program_id(0)
    ca = _load_indices(pa_hbm, ia_smem, idx_sem.at[0], i, tm)
    cb = _load_indices(pb_hbm, ib_smem, idx_sem.at[1], i, tm)
    ca.start()
    cb.start()
    ca.wait()
    cb.wait()

    def body(r, carry):
        pltpu.make_async_copy(y_hbm.at[pl.ds(ia_smem[r], 1), :], buf_a.at[pl.ds(r, 1), :], row_sem.at[0]).start()
        pltpu.make_async_copy(y_hbm.at[pl.ds(ib_smem[r], 1), :], buf_b.at[pl.ds(r, 1), :], row_sem.at[1]).start()
        return carry
    lax.fori_loop(0, tm, body, 0, unroll=GATHER_UNROLL)
    pltpu.make_async_copy(buf_a, buf_a, row_sem.at[0]).wait()
    pltpu.make_async_copy(buf_b, buf_b, row_sem.at[1]).wait()
    sel = sel_ref[...]
    out_ref[...] = h_ref[...] + (sel[:, 2:3] * buf_a[...] + sel[:, 3:4] * buf_b[...])


def moe_swiglu_top2(h2d, g_norm, w_router, w_gu, w_down, *, n_chunks=2):
    t_total, d = h2d.shape
    tm = MOE_TILE
    ff = w_down.shape[1]
    tf = ff // n_chunks
    sel = moe_route(h2d, g_norm, w_router)
    n_slots, tile_expert, tile_used, pos_a, pos_b = _moe_plan(sel, tm)
    any_spec = pl.BlockSpec(memory_space=pl.ANY)
    index_scratch = [pltpu.SMEM((tm,), jnp.int32), pltpu.SMEM((tm,), jnp.int32)]
    sems = [pltpu.SemaphoreType.DMA((2,)), pltpu.SemaphoreType.DMA((2,))]
    xs = pl.pallas_call(
        functools.partial(_moe_dispatch_kernel, tm=tm),
        grid=(t_total // tm,),
        in_specs=[any_spec, any_spec, pl.BlockSpec((tm, d), lambda i: (i, 0)), any_spec],
        out_specs=any_spec,
        out_shape=jax.ShapeDtypeStruct((n_slots, d), F32),
        scratch_shapes=index_scratch + sems,
        input_output_aliases={3: 0},
        compiler_params=_cparams(1),
        name="moe_dispatch",
    )(pos_a, pos_b, h2d, jnp.zeros((n_slots, d), F32))
    w_gu = w_gu.astype(BF16)
    row = lambda i, j, te, used: (i, 0)
    y = pl.pallas_call(
        _moe_group_kernel,
        grid_spec=pltpu.PrefetchScalarGridSpec(
            num_scalar_prefetch=2,
            grid=(n_slots // tm, n_chunks),
            in_specs=[
                pl.BlockSpec((tm, d), row),
                pl.BlockSpec((1, d), lambda i, j, te, used: (0, 0)),
                pl.BlockSpec((1, d, tf), lambda i, j, te, used: (te[i], 0, j)),
                pl.BlockSpec((1, d, tf), lambda i, j, te, used: (te[i], 0, j + n_chunks)),
                pl.BlockSpec((1, tf, d), lambda i, j, te, used: (te[i], j, 0)),
            ],
            out_specs=pl.BlockSpec((tm, d), row),
            scratch_shapes=[pltpu.VMEM((tm, d), BF16)],
        ),
        out_shape=jax.ShapeDtypeStruct((n_slots, d), F32),
        compiler_params=_cparams(2),
        name="moe_group_ffn",
    )(tile_expert, tile_used, xs, g_norm.reshape(1, d), w_gu, w_gu, w_down.astype(BF16))
    return pl.pallas_call(
        functools.partial(_moe_combine_kernel, tm=tm),
        grid=(t_total // tm,),
        in_specs=[any_spec, any_spec, any_spec, pl.BlockSpec((tm, d), lambda i: (i, 0)),
                  pl.BlockSpec((tm, LANES), lambda i: (i, 0))],
        out_specs=pl.BlockSpec((tm, d), lambda i: (i, 0)),
        out_shape=jax.ShapeDtypeStruct((t_total, d), F32),
        scratch_shapes=index_scratch + [pltpu.VMEM((tm, d), F32), pltpu.VMEM((tm, d), F32)] + sems,
        compiler_params=_cparams(1),
        name="moe_combine",
    )(pos_a, pos_b, y, h2d, sel)


B_HEADS = 16
B_KV_HEADS = 4
B_GROUP = 4
B_HEAD_DIM = 64
CMP_STRIDE = 16
CMP_BLOCK = 32
SEL_BLOCK = 32
N_SELECT = 8
WINDOW = 512
SEL_BIG = 1e9
TINY = 1e-30
N_KV_KINDS = 6

LOG2E = 1.4426950408889634
POS_RADIX = 64
N_SLOPE_PARTS = 3


def _alibi_key_part(pos, idx):
    hi = (pos // POS_RADIX * POS_RADIX).astype(F32)
    lo = (pos % POS_RADIX).astype(F32)
    return jnp.where(idx < N_SLOPE_PARTS, hi, jnp.where(idx < 2 * N_SLOPE_PARTS, lo, 0.0))


def _alibi_query_part(slopes):
    parts, rest = [], slopes.astype(F32)
    for _ in range(N_SLOPE_PARTS):
        p = rest.astype(BF16)
        parts.append(p)
        rest = rest - p.astype(F32)
    return jnp.stack(parts + parts, axis=-1)


def _kv_proj_kernel(x_ref, gn_ref, wk_ref, kgain_ref, wvt_ref, wc_ref,
                    ks_ref, kw_ref, vst_ref, vwt_ref, kc_ref, vc_ref):
    g = B_KV_HEADS
    hn = _rms(x_ref[...], gn_ref[...]).astype(BF16)
    k = _dot(hn, wk_ref[...])
    tm = k.shape[0]
    pos = pl.program_id(1) * tm + lax.broadcasted_iota(jnp.int32, (tm, LANES), 0)
    lane = lax.broadcasted_iota(jnp.int32, (tm, LANES), 1)
    pos_lanes = jnp.where(lane >= B_HEAD_DIM, _alibi_key_part(pos, lane - B_HEAD_DIM), 0.0)
    for j in range(2 * g):
        kj = k[:, j * LANES:(j + 1) * LANES]
        ms = jnp.sum(kj * kj, axis=-1, keepdims=True) * (1.0 / B_HEAD_DIM)
        kj = (kj * lax.rsqrt(ms + NORM_EPS) * kgain_ref[:, j * LANES:(j + 1) * LANES] + pos_lanes).astype(BF16)
        if j < g:
            ks_ref[0, j] = kj
        else:
            kw_ref[0, j - g] = kj
    vt = _dot_nt(wvt_ref[...], hn)
    row = lax.broadcasted_iota(jnp.int32, vt.shape, 0)
    vt = jnp.where(row % LANES >= B_HEAD_DIM, 1.0, vt).astype(BF16)
    for j in range(g):
        vst_ref[0, j] = vt[j * LANES:(j + 1) * LANES]
        vwt_ref[0, j] = vt[(g + j) * LANES:(g + j + 1) * LANES]
    c = _dot(hn, wc_ref[...])
    half = g * B_HEAD_DIM
    kc_ref[...] = c[:, :half].astype(BF16)
    vc_ref[...] = c[:, half:].astype(BF16)


def nsa_kv_proj(h2d, bsz, seq, kv_norm, kv_w, k_norm, *, tm=512):
    t_total, d = h2d.shape
    g, dh = B_KV_HEADS, B_HEAD_DIM
    n_sblk = seq // tm
    w = kv_w.reshape(d, N_KV_KINDS, g, dh)
    def head_slabs(kinds):
        cols = jnp.stack([w[:, kind] for kind in kinds], axis=1)
        return jnp.pad(cols, ((0, 0), (0, 0), (0, 0), (0, LANES - dh))).reshape(d, len(kinds) * g * LANES)

    wk = head_slabs((2, 4)).astype(BF16)
    wvt = head_slabs((3, 5)).T.astype(BF16)
    kgain = jnp.pad(jnp.stack([jnp.tile(k_norm[1], (g, 1)), jnp.tile(k_norm[2], (g, 1))]),
                    ((0, 0), (0, 0), (0, LANES - dh))).reshape(1, 2 * g * LANES)
    wc = jnp.concatenate([w[:, 0].reshape(d, g * dh), w[:, 1].reshape(d, g * dh)], axis=1).astype(BF16)
    const2 = lambda b, s: (0, 0)
    row = lambda b, s: (b * n_sblk + s, 0)
    k_spec = pl.BlockSpec((1, g, tm, LANES), lambda b, s: (b, 0, s, 0))
    vt_spec = pl.BlockSpec((1, g, LANES, tm), lambda b, s: (b, 0, 0, s))
    k_shape = jax.ShapeDtypeStruct((bsz, g, seq, LANES), BF16)
    vt_shape = jax.ShapeDtypeStruct((bsz, g, LANES, seq), BF16)
    c_shape = jax.ShapeDtypeStruct((t_total, g * dh), BF16)
    return pl.pallas_call(
        _kv_proj_kernel,
        grid=(bsz, n_sblk),
        in_specs=[
            pl.BlockSpec((tm, d), row),
            _resident((1, d), const2),
            _resident((d, 2 * g * LANES), const2),
            _resident((1, 2 * g * LANES), const2),
            _resident((2 * g * LANES, d), const2),
            _resident((d, 2 * g * dh), const2),
        ],
        out_specs=[k_spec, k_spec, vt_spec, vt_spec,
                   pl.BlockSpec((tm, g * dh), row), pl.BlockSpec((tm, g * dh), row)],
        out_shape=[k_shape, k_shape, vt_shape, vt_shape, c_shape, c_shape],
        compiler_params=_cparams(2),
        name="nsa_kv_proj",
    )(h2d, kv_norm.reshape(1, d), wk, kgain, wvt, wc)


def _compress_kernel(ak_ref, av_ref, posk_ref, posv_ref, w1k_ref, w1v_ref, w2k_ref, w2vt_ref, kgain_ref,
                     kca_ref, vct_ref, *, n_half):
    half_w = ak_ref.shape[1]

    def hidden(a_ref, pos_ref, w1_ref):
        a = a_ref[...]
        top = _dot(a, w1_ref[:half_w, :])
        bot = _dot(a, w1_ref[half_w:, :])
        bias = _dot(pos_ref[...], w1_ref[...])[0:1, :]
        pre = top + pltpu.roll(bot, shift=a.shape[0] - 1, axis=0) + bias
        return (pre * _sigmoid(pre)).astype(BF16)

    act_k = hidden(ak_ref, posk_ref, w1k_ref)
    kc = _dot(act_k, w2k_ref[...])
    ms = jnp.sum(kc * kc, axis=-1, keepdims=True) * (1.0 / B_HEAD_DIM)
    kc = kc * lax.rsqrt(ms + NORM_EPS) * kgain_ref[...]
    blk = lax.broadcasted_iota(jnp.int32, kc.shape, 0) % n_half
    lane = lax.broadcasted_iota(jnp.int32, kc.shape, 1)
    c_end = blk * CMP_STRIDE + (CMP_BLOCK - 1)
    kca_ref[...] = (kc + jnp.where(lane >= B_HEAD_DIM, _alibi_key_part(c_end, lane - B_HEAD_DIM), 0.0)).astype(BF16)
    act_v = hidden(av_ref, posv_ref, w1v_ref)
    vct = _dot_nt(w2vt_ref[...], act_v).astype(BF16)
    for j in range(vct.shape[1] // n_half):
        vct_ref[j] = vct[:, j * n_half:(j + 1) * n_half]


def nsa_compress(kc_raw, vc_raw, bsz, seq, pos_k, w1_k, w2_k, pos_v, w1_v, w2_v, k_gain, *, groups_per_step=4):
    g, dh = B_KV_HEADS, B_HEAD_DIM
    n_half = seq // CMP_STRIDE
    feat = CMP_STRIDE * dh

    def to_half_blocks(a):
        a = a.reshape(bsz, n_half, CMP_STRIDE, g, dh).transpose(0, 3, 1, 2, 4)
        return a.reshape(bsz * g * n_half, feat)

    def pos_rows(pos):
        return jnp.zeros((8, 2 * feat), F32).at[0].set(pos.reshape(-1)).astype(BF16)

    hid = w1_k.shape[1]
    pad_lanes = lambda a: jnp.pad(a, ((0, 0), (0, LANES - dh)))
    rows = groups_per_step * n_half
    const2 = lambda i: (0, 0)
    return pl.pallas_call(
        functools.partial(_compress_kernel, n_half=n_half),
        grid=(bsz * g // groups_per_step,),
        in_specs=[
            pl.BlockSpec((rows, feat), lambda i: (i, 0)),
            pl.BlockSpec((rows, feat), lambda i: (i, 0)),
            _resident((8, 2 * feat), const2),
            _resident((8, 2 * feat), const2),
            _resident((2 * feat, hid), const2),
            _resident((2 * feat, hid), const2),
            _resident((hid, LANES), const2),
            _resident((LANES, hid), const2),
            _resident((1, LANES), const2),
        ],
        out_specs=[pl.BlockSpec((rows, LANES), lambda i: (i, 0)),
                   pl.BlockSpec((groups_per_step, LANES, n_half), lambda i: (i, 0, 0))],
        out_shape=[jax.ShapeDtypeStruct((bsz * g * n_half, LANES), BF16),
                   jax.ShapeDtypeStruct((bsz * g, LANES, n_half), BF16)],
        compiler_params=_cparams(1),
        name="nsa_compress",
    )(to_half_blocks(kc_raw), to_half_blocks(vc_raw), pos_rows(pos_k), pos_rows(pos_v),
      w1_k.astype(BF16), w1_v.astype(BF16), pad_lanes(w2_k).astype(BF16), pad_lanes(w2_v).T.astype(BF16),
      pad_lanes(k_gain.reshape(1, dh)))


def _q_proj_kernel(x_ref, gn_ref, w_ref, qgain_ref, qconst_ref, q_ref, gate_ref):
    d_q = q_ref.shape[1]
    xn = _rms(x_ref[...], gn_ref[...]).astype(BF16)
    proj = _dot(xn, w_ref[...])
    ones = jnp.ones((LANES, LANES), BF16)
    for h in range(d_q // LANES):
        lanes = slice(h * LANES, (h + 1) * LANES)
        q = proj[:, lanes]
        sq_hi, sq_lo = _split_bf16(q * q)
        ms = (_dot(sq_hi, ones) + _dot(sq_lo, ones)) * (1.0 / B_HEAD_DIM)
        q_ref[:, lanes] = (q * lax.rsqrt(ms + NORM_EPS) * qgain_ref[:, lanes] + qconst_ref[:, lanes]).astype(BF16)
    gate_ref[...] = _sigmoid(proj[:, d_q:])


def nsa_q_proj(h2d, g_norm, w_in, q_norm, *, tm=512):
    t_total, d = h2d.shape
    g, hpg, dh = B_KV_HEADS, B_GROUP, B_HEAD_DIM
    d_q = B_HEADS * LANES
    wq = jnp.pad(w_in[:, :B_HEADS * dh].reshape(d, B_HEADS, dh), ((0, 0), (0, 0), (0, LANES - dh))).reshape(d, d_q)
    wg = w_in[:, B_HEADS * dh:].reshape(d, g, hpg * 3)
    wg = jnp.pad(wg, ((0, 0), (0, 0), (0, LANES - hpg * 3))).reshape(d, g * LANES)
    w = jnp.concatenate([wq, wg], axis=1).astype(BF16)
    qgain = jnp.tile(jnp.pad(q_norm * (dh ** -0.5 * LOG2E), (0, LANES - dh)), B_HEADS).reshape(1, d_q)
    slopes = np.array([2.0 ** (-8.0 * (h + 1) / B_HEADS) for h in range(B_HEADS)], dtype=np.float64) * LOG2E
    feats = _alibi_query_part(jnp.asarray(slopes, F32)).astype(F32)
    qconst = jnp.pad(feats, ((0, 0), (dh, LANES - dh - feats.shape[1]))).reshape(1, d_q)
    const2 = lambda i: (0, 0)
    return pl.pallas_call(
        _q_proj_kernel,
        grid=(t_total // tm,),
        in_specs=[
            pl.BlockSpec((tm, d), lambda i: (i, 0)),
            _resident((1, d), const2),
            _resident((d, d_q + g * LANES), const2),
            _resident((1, d_q), const2),
            _resident((1, d_q), const2),
        ],
        out_specs=[pl.BlockSpec((tm, d_q), lambda i: (i, 0)), pl.BlockSpec((tm, g * LANES), lambda i: (i, 0))],
        out_shape=[jax.ShapeDtypeStruct((t_total, d_q), BF16), jax.ShapeDtypeStruct((t_total, g * LANES), F32)],
        compiler_params=_cparams(1),
        name="nsa_q_proj",
    )(h2d, g_norm.reshape(1, d), w, qgain, qconst)


def _nsa_attn_kernel(q_ref, gate_ref, kca_ref, vct_ref, ks_ref, vst_ref, kw_ref, vwt_ref,
                     ovlt_ref, eselt_ref, o_ref, live_smem, m_scr, acc_scr, *, tq, tk):
    hpg, dh = B_GROUP, B_HEAD_DIM
    t0 = pl.program_id(2) * tq
    rows = hpg * tq
    qb = q_ref[...]
    q_st = jnp.concatenate([qb[:, i * LANES:(i + 1) * LANES] for i in range(hpg)], axis=0)

    n_half = kca_ref.shape[0]
    c_end = lax.broadcasted_iota(jnp.int32, (n_half, 1), 0) * CMP_STRIDE + (CMP_BLOCK - 1)
    t_lane = t0 + lax.broadcasted_iota(jnp.int32, (1, rows), 1) % tq
    seen = t_lane >= c_end
    s = jnp.where(seen, _dot_nt(kca_ref[...], q_st), NEG)
    m = jnp.max(s, axis=0, keepdims=True)
    e = jnp.where(seen, jnp.exp2(s - m), 0.0)
    p = e / jnp.maximum(jnp.sum(e, axis=0, keepdims=True), TINY)
    o_cmp = _dot(vct_ref[0], p.astype(BF16))[:dh]

    p_grp = p[:, 0:tq]
    for i in range(1, hpg):
        p_grp = p_grp + p[:, i * tq:(i + 1) * tq]
    p_hi, p_lo = _split_bf16(p_grp)
    n_blk = LANES // 2
    imp = (_dot(ovlt_ref[...], p_hi) + _dot(ovlt_ref[...], p_lo))[:n_blk]
    blk = lax.broadcasted_iota(jnp.int32, (n_blk, tq), 0)
    t_l = t0 + lax.broadcasted_iota(jnp.int32, (1, tq), 1)
    valid = blk * SEL_BLOCK <= t_l
    cur = t_l // SEL_BLOCK
    forced = valid & ((blk == 0) | (blk == cur) | (blk == cur - 1))
    work = jnp.where(forced, SEL_BIG, jnp.where(valid, imp, -SEL_BIG))
    blk_f = blk.astype(F32)
    chosen = jnp.zeros((n_blk, tq), F32)
    for _ in range(N_SELECT):
        top = jnp.max(work, axis=0, keepdims=True)
        first = jnp.min(jnp.where(work == top, blk_f, float(LANES)), axis=0, keepdims=True)
        hit = blk_f == first
        chosen = jnp.where(hit & (top > -0.5 * SEL_BIG), 1.0, chosen)
        work = jnp.where(hit, -3e38, work)
    blocks_per_tile = tk // SEL_BLOCK
    any_q = jnp.max(chosen, axis=1, keepdims=True)
    for kt in range(n_blk // blocks_per_tile):
        tile_any = jnp.max(any_q[kt * blocks_per_tile:(kt + 1) * blocks_per_tile])
        live_smem[kt] = (tile_any > 0.5).astype(jnp.int32)
    chosen = jnp.concatenate([chosen, jnp.zeros((LANES - n_blk, tq), F32)], axis=0).astype(BF16)

    def normalised(acc):
        return acc[:dh] / acc[dh:2 * dh]

    span = WINDOW + tq
    w0 = pl.multiple_of(jnp.maximum(t0 - WINDOW, 0), LANES)
    dist = t_l - (w0 + lax.broadcasted_iota(jnp.int32, (span, 1), 0))
    pen = jnp.where((dist >= 0) & (dist < WINDOW), 0.0, NEG)
    s = _dot_nt(kw_ref[0, 0, pl.ds(w0, span), :], q_st) + jnp.concatenate([pen] * hpg, axis=1)
    m = jnp.max(s, axis=0, keepdims=True)
    o_win = normalised(_dot(vwt_ref[0, 0, :, pl.ds(w0, span)], jnp.exp2(s - m).astype(BF16)))

    m_scr[...] = jnp.full(m_scr.shape, NEG, F32)
    acc_scr[...] = jnp.zeros(acc_scr.shape, F32)

    def sel_step(kt, carry):
        @pl.when(live_smem[kt] != 0)
        def _():
            k0 = pl.multiple_of(kt * tk, tk)
            pos = k0 + lax.broadcasted_iota(jnp.int32, (tk, 1), 0)
            picked = _dot(eselt_ref[pl.ds(k0, tk), :], chosen)
            pen = jnp.where((picked > 0.5) & (pos <= t_l), 0.0, NEG)
            sc = _dot_nt(ks_ref[0, 0, pl.ds(k0, tk), :], q_st) + jnp.concatenate([pen] * hpg, axis=1)
            m_run = m_scr[...]
            m_new = jnp.maximum(m_run, jnp.broadcast_to(jnp.max(sc, axis=0, keepdims=True), m_run.shape))
            pr = jnp.exp2(sc - m_new[0:1]).astype(BF16)
            acc_scr[...] = (jnp.exp2(m_run - m_new)[0:1] * acc_scr[...]
                            + _dot(vst_ref[0, 0, :, pl.ds(k0, tk)], pr))
            m_scr[...] = m_new
        return carry

    lax.fori_loop(0, (t0 + tq + tk - 1) // tk, sel_step, 0)
    o_sel = normalised(acc_scr[...])

    gate_t = gate_ref[...].T
    outs = []
    for i in range(hpg):
        c = slice(i * tq, (i + 1) * tq)
        outs.append(gate_t[3 * i:3 * i + 1] * o_cmp[:, c] + gate_t[3 * i + 1:3 * i + 2] * o_sel[:, c]
                    + gate_t[3 * i + 2:3 * i + 3] * o_win[:, c])
    o_ref[...] = jnp.concatenate(outs, axis=0).T.astype(BF16)


def nsa_attention(q, gates, kca, vct, ks, vst, kw, vwt, bsz, seq, *, tq=256, tk=512):
    g, hpg, dh = B_KV_HEADS, B_GROUP, B_HEAD_DIM
    n_q = seq // tq
    n_half = seq // CMP_STRIDE
    n_cmp = n_half - 1
    n_sel = seq // SEL_BLOCK
    assert n_sel <= LANES // 2 and tq % LANES == 0
    c_start = np.arange(n_half) * CMP_STRIDE
    j_sel = np.arange(LANES)
    ovl = ((c_start[:, None] < (j_sel[None, :] + 1) * SEL_BLOCK)
           & (c_start[:, None] + CMP_BLOCK > j_sel[None, :] * SEL_BLOCK)
           & (np.arange(n_half)[:, None] < n_cmp) & (j_sel[None, :] < n_sel))
    esel = (np.arange(seq)[None, :] // SEL_BLOCK) == j_sel[:, None]
    row = lambda b, j, i: (b * n_q + i, j)
    per_bg = lambda b, j, i: (b, j, 0, 0)
    return pl.pallas_call(
        functools.partial(_nsa_attn_kernel, tq=tq, tk=tk),
        grid=(bsz, g, n_q),
        in_specs=[
            pl.BlockSpec((tq, hpg * LANES), row),
            pl.BlockSpec((tq, LANES), row),
            pl.BlockSpec((n_half, LANES), lambda b, j, i: (b * g + j, 0)),
            pl.BlockSpec((1, LANES, n_half), lambda b, j, i: (b * g + j, 0, 0)),
            pl.BlockSpec((1, 1, seq, LANES), per_bg),
            pl.BlockSpec((1, 1, LANES, seq), per_bg),
            pl.BlockSpec((1, 1, seq, LANES), per_bg),
            pl.BlockSpec((1, 1, LANES, seq), per_bg),
            _resident((LANES, n_half), lambda b, j, i: (0, 0)),
            _resident((seq, LANES), lambda b, j, i: (0, 0)),
        ],
        out_specs=pl.BlockSpec((tq, hpg * dh), row),
        out_shape=jax.ShapeDtypeStruct((bsz * seq, B_HEADS * dh), BF16),
        scratch_shapes=[pltpu.SMEM((seq // tk,), jnp.int32), pltpu.VMEM((SUBLANES, hpg * tq), F32),
                        pltpu.VMEM((LANES, hpg * tq), F32)],
        compiler_params=_cparams(3),
        name="nsa_attention",
    )(q, gates, kca, vct, ks, vst, kw, vwt, jnp.asarray(ovl.T, BF16), jnp.asarray(esel.T, BF16))


def _out_proj_kernel(h_ref, o_ref, w_ref, out_ref):
    out_ref[...] = h_ref[...] + _dot(o_ref[...], w_ref[...])


def out_proj_residual(h2d, o, w_out, *, tm=512):
    t_total, d = h2d.shape
    k = o.shape[1]
    return pl.pallas_call(
        _out_proj_kernel,
        grid=(t_total // tm,),
        in_specs=[pl.BlockSpec((tm, d), lambda i: (i, 0)), pl.BlockSpec((tm, k), lambda i: (i, 0)),
                  _resident((k, d), lambda i: (0, 0))],
        out_specs=pl.BlockSpec((tm, d), lambda i: (i, 0)),
        out_shape=jax.ShapeDtypeStruct((t_total, d), F32),
        compiler_params=_cparams(1),
        name="out_proj_residual",
    )(h2d, o, w_out.astype(BF16))


def nsa_shared_kv(h2d, bsz, seq, kv_norm, kv_w, cmp_pos_k, cmp_w1_k, cmp_w2_k, cmp_pos_v, cmp_w1_v, cmp_w2_v, k_norm):
    ks, kw, vst, vwt, kc_raw, vc_raw = nsa_kv_proj(h2d, bsz, seq, kv_norm, kv_w, k_norm)
    kca, vct = nsa_compress(kc_raw, vc_raw, bsz, seq, cmp_pos_k, cmp_w1_k, cmp_w2_k,
                            cmp_pos_v, cmp_w1_v, cmp_w2_v, k_norm[0])
    return kca, vct, ks, vst, kw, vwt


def nsa_mixer(h2d, bsz, seq, g_norm, w_in, w_out, q_norm, kv):
    q, gates = nsa_q_proj(h2d, g_norm, w_in, q_norm)
    o = nsa_attention(q, gates, *kv, bsz, seq)
    return out_proj_residual(h2d, o, w_out)


def kernel(x, norm_mix, norm_ffn, hgrn_w_in, hgrn_w_out, hgrn_g_norm, hgrn_lb_logits, kv_norm, kv_w, cmp_pos_k, cmp_w1_k, cmp_w2_k, cmp_pos_v, cmp_w1_v, cmp_w2_v, k_norm, nsa_w_in, nsa_w_out, nsa_q_norm, ffn_w_gu, ffn_w_down, moe_router, moe_w_gu, moe_w_down):
    bsz, seq, d = x.shape
    lb = jnp.cumsum(jax.nn.softmax(hgrn_lb_logits.astype(F32), axis=0), axis=0)
    lb = lb - lb[0:1]
    n_a = hgrn_w_in.shape[0]
    depth = norm_mix.shape[0]
    h = x.reshape(bsz * seq, d)
    kv = None
    for layer in range(depth):
        if layer < n_a:
            h = hgrn_mixer(h, seq, norm_mix[layer], hgrn_w_in[layer], hgrn_w_out[layer], hgrn_g_norm[layer],
                           lb[layer])
        else:
            if kv is None:
                kv = nsa_shared_kv(h, bsz, seq, kv_norm, kv_w, cmp_pos_k, cmp_w1_k, cmp_w2_k,
                                   cmp_pos_v, cmp_w1_v, cmp_w2_v, k_norm)
            j = layer - n_a
            h = nsa_mixer(h, bsz, seq, norm_mix[layer], nsa_w_in[j], nsa_w_out[j], nsa_q_norm[j], kv)
        if layer % 2 == 0:
            h = ffn_swiglu(h, norm_ffn[layer], ffn_w_gu[layer // 2], ffn_w_down[layer // 2])
        else:
            h = moe_swiglu_top2(h, norm_ffn[layer], moe_router[layer // 2], moe_w_gu[layer // 2],
                                moe_w_down[layer // 2])
    return h.reshape(bsz, seq, d)
```

```python
import functools

import numpy as np
import jax
import jax.numpy as jnp
from jax import lax
from jax.experimental import pallas as pl
from jax.experimental.pallas import tpu as pltpu

F32 = jnp.float32
BF16 = jnp.bfloat16

NORM_EPS = 1e-6
VMEM_LIMIT_BYTES = 56 * 1024 * 1024

A_HEADS = 8
A_HEAD_DIM = 128
SCAN_ROWS = 128
SUBLANES = 8


def _cparams(n_axes):
    return pltpu.CompilerParams(dimension_semantics=("arbitrary",) * n_axes,
                                vmem_limit_bytes=VMEM_LIMIT_BYTES)


def _dot(a, b):
    return jnp.dot(a, b, preferred_element_type=F32)


def _dot_nt(a, b):
    return lax.dot_general(a, b, (((1,), (1,)), ((), ())), preferred_element_type=F32)


def _dot_tn(a, b):
    return lax.dot_general(a, b, (((0,), (0,)), ((), ())), preferred_element_type=F32)


def _sigmoid(x):
    return 0.5 * jnp.tanh(0.5 * x) + 0.5


def _rms(x, g):
    return x * lax.rsqrt(jnp.mean(x * x, axis=-1, keepdims=True) + NORM_EPS) * g


def _split_bf16(x):
    hi = x.astype(BF16)
    lo = (x - hi.astype(F32)).astype(BF16)
    return hi, lo


def _hgrn_constants():
    n = SCAN_ROWS
    t = np.arange(n)
    r = np.arange(n)[None, :]
    sums, masks = [], []
    m = n // 2
    while m >= 1:
        grp = t // (2 * m)
        mid = grp * 2 * m + m
        upper = (t % (2 * m)) >= m
        if m < SUBLANES:
            sums.append(np.where(upper[:, None], (r >= mid[:, None]) & (r <= t[:, None]),
                                 (r > t[:, None]) & (r < mid[:, None])))
        masks.append((grp[:, None] == grp[None, :]) & upper[:, None] & (~upper[None, :]))
        m //= 2
    masks.append(np.eye(n, dtype=bool))
    sums.append(r <= t[:, None])
    w = np.concatenate(sums, 0).astype(np.float32)
    return np.concatenate([w, w], 1), np.stack(masks).astype(np.float32)


def _hgrn_kernel(x_ref, gn_ref, win_ref, lb_ref, ghn_ref, wout_ref, wsum_ref, msk_ref,
                 out_ref, xn_scr, st_scr, og_scr, *, tm):
    sblk = pl.program_id(1)
    h = pl.program_id(2)
    n = SCAN_ROWS
    n_lvl = msk_ref.shape[0] - 1

    @pl.when(h == 0)
    def _():
        xn_scr[...] = _rms(x_ref[...], gn_ref[...]).astype(BF16)

    @pl.when(sblk == 0)
    def _():
        st_scr[h] = jnp.zeros((A_HEAD_DIM, A_HEAD_DIM), F32)

    proj = _dot(xn_scr[...], win_ref[...])
    q = proj[:, 0:128]
    f = proj[:, 128:256]
    v = proj[:, 256:384]
    g = proj[:, 384:512]
    lb = lb_ref[0:1, :]
    log_lb = lb_ref[1:2, :]
    log_1m_lb = lb_ref[2:3, :]
    qs = q * _sigmoid(q)
    a = jnp.exp(-jnp.abs(f))
    inv = 1.0 / (1.0 + a)
    log_sig = jnp.minimum(f, 0.0) - jnp.log1p(a)
    b = log_1m_lb + log_sig
    logf = jnp.maximum(log_lb, b) + jnp.log1p(jnp.exp(-jnp.abs(log_lb - b)))
    key = (1.0 - lb) * jnp.where(f >= 0.0, a * inv, inv)

    wide = [n >> (l + 1) for l in range(n_lvl) if n >> (l + 1) >= SUBLANES]
    row_id = lax.broadcasted_iota(jnp.int32, (n, A_HEAD_DIM), 0)
    in_upper_half = [(row_id % (2 * m)) >= m for m in wide]

    blocks = [slice(r * n, (r + 1) * n) for r in range(tm // n)]
    n_narrow = n_lvl - len(wide)
    sums = []
    for rows in blocks:
        hi, lo = _split_bf16(logf[rows])
        sums.append(_dot(wsum_ref[...], jnp.concatenate([hi, lo], axis=0)))
    e_lvl, e_prefix, e_suffix = [], [], []
    for sm in sums:
        cum = sm[n_narrow * n:]
        e_blk = []
        for m, upper in zip(wide, in_upper_half):
            bound = jnp.concatenate([jnp.broadcast_to(cum[i + m - 1:i + m], (2 * m, A_HEAD_DIM))
                                     for i in range(0, n, 2 * m)], axis=0)
            e_blk.append(jnp.exp(jnp.where(upper, cum - bound, bound - cum)))
        e_sm = jnp.exp(sm)
        e_lvl.append(e_blk + [e_sm[i * n:(i + 1) * n] for i in range(n_narrow)])
        e_prefix.append(e_sm[n_narrow * n:])
        e_suffix.append(jnp.exp(cum[n - 1:n] - cum))
    intra, inter_q, kv = [], [], []
    for rows, e_blk, e_pre, e_suf in zip(blocks, e_lvl, e_prefix, e_suffix):
        qb = qs[rows]
        kb = key[rows]
        vb = v[rows].astype(BF16)
        s = msk_ref[n_lvl] * _dot_nt(qb.astype(BF16), kb.astype(BF16))
        for l in range(n_lvl):
            s = s + msk_ref[l] * _dot_nt((qb * e_blk[l]).astype(BF16), (kb * e_blk[l]).astype(BF16))
        intra.append(_dot(s.astype(BF16), vb))
        inter_q.append((qb * e_pre).astype(BF16))
        kv.append(_dot_tn(vb, (kb * e_suf).astype(BF16)))
    st = st_scr[h]
    outs = []
    for o_intra, q_pre, e_pre, kv_blk in zip(intra, inter_q, e_prefix, kv):
        outs.append(o_intra + _dot_nt(q_pre, st.astype(BF16)))
        st = st * e_pre[n - 1:n, :] + kv_blk
    st_scr[h] = st
    o = jnp.concatenate(outs, axis=0)

    og_scr[:, pl.ds(pl.multiple_of(h * A_HEAD_DIM, A_HEAD_DIM), A_HEAD_DIM)] = (
        _rms(o, ghn_ref[...]) * (g * _sigmoid(g))).astype(BF16)

    @pl.when(h == A_HEADS - 1)
    def _():
        out_ref[...] = x_ref[...] + _dot(og_scr[...], wout_ref[...])


def hgrn_mixer(h2d, seq, g_norm_in, w_in, w_out, g_head, lb, *, tm=512):
    t_total, d = h2d.shape
    n_sblk = seq // tm
    hd = A_HEAD_DIM
    w_perm = w_in.reshape(d, 4, A_HEADS, hd).transpose(0, 2, 1, 3).reshape(d, 4 * A_HEADS * hd).astype(BF16)
    lb_rows = jnp.zeros((8, A_HEADS * hd), F32)
    lb_rows = lb_rows.at[0].set(lb).at[1].set(jnp.log(lb)).at[2].set(jnp.log1p(-lb))
    wsum, masks = _hgrn_constants()
    n_blk = wsum.shape[0]
    grid = (t_total // seq, n_sblk, A_HEADS)
    row = lambda b, s, h: (b * n_sblk + s, 0)
    const2 = lambda b, s, h: (0, 0)
    return pl.pallas_call(
        functools.partial(_hgrn_kernel, tm=tm),
        grid=grid,
        in_specs=[
            pl.BlockSpec((tm, d), row),
            pl.BlockSpec((1, d), const2),
            pl.BlockSpec((d, 4 * hd), lambda b, s, h: (0, h)),
            pl.BlockSpec((8, hd), lambda b, s, h: (0, h)),
            pl.BlockSpec((1, hd), const2),
            _resident((A_HEADS * hd, d), const2),
            _resident((n_blk, 2 * SCAN_ROWS), const2),
            _resident(masks.shape, lambda b, s, h: (0, 0, 0)),
        ],
        out_specs=pl.BlockSpec((tm, d), row),
        out_shape=jax.ShapeDtypeStruct((t_total, d), F32),
        scratch_shapes=[pltpu.VMEM((tm, d), BF16), pltpu.VMEM((A_HEADS, hd, hd), F32),
                        pltpu.VMEM((tm, A_HEADS * hd), BF16)],
        compiler_params=_cparams(3),
        name="hgrn_mixer",
    )(h2d, g_norm_in.reshape(1, d), w_perm, lb_rows, g_head.reshape(1, hd), w_out.astype(BF16),
      jnp.asarray(wsum, BF16), jnp.asarray(masks, F32))


def _resident(block_shape, index_map):
    return pl.BlockSpec(block_shape, index_map, pipeline_mode=pl.Buffered(1))


MXU_TILE = 256


def _swiglu(xn, wg_ref, wu_ref, wd_ref):
    acc = None
    for c in range(wg_ref.shape[1] // MXU_TILE):
        cols = slice(c * MXU_TILE, (c + 1) * MXU_TILE)
        gate = _dot(xn, wg_ref[:, cols])
        up = _dot(xn, wu_ref[:, cols])
        part = _dot((gate * _sigmoid(gate) * up).astype(BF16), wd_ref[cols, :])
        acc = part if acc is None else acc + part
    return acc


def _ffn_kernel(x_ref, gn_ref, wg_ref, wu_ref, wd_ref, out_ref):
    x = x_ref[...]
    xn = _rms(x, gn_ref[...]).astype(BF16)
    out_ref[...] = x + _swiglu(xn, wg_ref, wu_ref, wd_ref)


def ffn_swiglu(h2d, g_norm, w_gu, w_down, *, tm=512):
    t_total, d = h2d.shape
    ff = w_down.shape[0]
    w_gu = w_gu.astype(BF16)
    const2 = lambda i: (0, 0)
    return pl.pallas_call(
        _ffn_kernel,
        grid=(t_total // tm,),
        in_specs=[
            pl.BlockSpec((tm, d), lambda i: (i, 0)),
            _resident((1, d), const2),
            _resident((d, ff), const2),
            _resident((d, ff), lambda i: (0, 1)),
            _resident((ff, d), const2),
        ],
        out_specs=pl.BlockSpec((tm, d), lambda i: (i, 0)),
        out_shape=jax.ShapeDtypeStruct((t_total, d), F32),
        compiler_params=_cparams(1),
        name="ffn_swiglu",
    )(h2d, g_norm.reshape(1, d), w_gu, w_gu, w_down.astype(BF16))


N_EXPERTS = 8
LANES = 128
NEG = -1e30


def _router_kernel(x_ref, gn_ref, wh_ref, wl_ref, sel_ref):
    xn = _rms(x_ref[...], gn_ref[...])
    xh, xl = _split_bf16(xn)
    logits = _dot(xh, wh_ref[...]) + (_dot(xl, wh_ref[...]) + _dot(xh, wl_ref[...]))
    lane = lax.broadcasted_iota(jnp.int32, logits.shape, 1).astype(F32)
    logits = jnp.where(lane < N_EXPERTS, logits, NEG)
    m1 = jnp.max(logits, axis=-1, keepdims=True)
    i1 = jnp.min(jnp.where(logits == m1, lane, float(LANES)), axis=-1, keepdims=True)
    rest = jnp.where(lane == i1, NEG, logits)
    m2 = jnp.max(rest, axis=-1, keepdims=True)
    i2 = jnp.min(jnp.where(rest == m2, lane, float(LANES)), axis=-1, keepdims=True)
    e2 = jnp.exp(m2 - m1)
    denom = 1.0 + e2
    sel_ref[...] = jnp.where(lane == 0, i1, jnp.where(lane == 1, i2, jnp.where(
        lane == 2, 1.0 / denom, jnp.where(lane == 3, e2 / denom, 0.0))))


def moe_route(h2d, g_norm, w_router, *, tm=512):
    t_total, d = h2d.shape
    w_pad = jnp.zeros((d, LANES), F32).at[:, :N_EXPERTS].set(w_router)
    w_hi = w_pad.astype(BF16)
    w_lo = (w_pad - w_hi.astype(F32)).astype(BF16)
    const2 = lambda i: (0, 0)
    return pl.pallas_call(
        _router_kernel,
        grid=(t_total // tm,),
        in_specs=[
            pl.BlockSpec((tm, d), lambda i: (i, 0)),
            _resident((1, d), const2),
            _resident((d, LANES), const2),
            _resident((d, LANES), const2),
        ],
        out_specs=pl.BlockSpec((tm, LANES), lambda i: (i, 0)),
        out_shape=jax.ShapeDtypeStruct((t_total, LANES), F32),
        compiler_params=_cparams(1),
        name="moe_router",
    )(h2d, g_norm.reshape(1, d), w_hi, w_lo)


MOE_TILE = 1024
GATHER_UNROLL = 8


def _moe_plan(sel, tm):
    t_total = sel.shape[0]
    e = sel[:, :2].astype(jnp.int32)
    onehot = (e[:, :, None] == jnp.arange(N_EXPERTS, dtype=jnp.int32)[None, None, :]).astype(jnp.int32).sum(1)
    csum = jnp.cumsum(onehot, axis=0)
    rank = csum - onehot
    padded = (csum[-1] + tm - 1) // tm * tm
    ends = jnp.cumsum(padded)
    pos = (ends - padded)[e] + jnp.take_along_axis(rank, e, axis=1)
    n_slots = 2 * t_total + N_EXPERTS * tm
    tile_start = jnp.arange(n_slots // tm, dtype=jnp.int32) * tm
    tile_expert = jnp.minimum(jnp.searchsorted(ends, tile_start, side="right"), N_EXPERTS - 1).astype(jnp.int32)
    tile_used = (tile_start < ends[-1]).astype(jnp.int32)
    return n_slots, tile_expert, tile_used, pos[:, 0], pos[:, 1]


def _load_indices(idx_hbm, idx_smem, sem, tile, tm):
    return pltpu.make_async_copy(idx_hbm.at[pl.ds(tile * tm, tm)], idx_smem, sem)


def _moe_dispatch_kernel(pa_hbm, pb_hbm, h_ref, zeros_hbm, xs_hbm, ia_smem, ib_smem, row_sem, idx_sem, *, tm):
    del zeros_hbm
    i = pl.program_id(0)
    ca = _load_indices(pa_hbm, ia_smem, idx_sem.at[0], i, tm)
    cb = _load_indices(pb_hbm, ib_smem, idx_sem.at[1], i, tm)
    ca.start()
    cb.start()
    ca.wait()
    cb.wait()

    def body(r, carry):
        src = h_ref.at[pl.ds(r, 1), :]
        pltpu.make_async_copy(src, xs_hbm.at[pl.ds(ia_smem[r], 1), :], row_sem.at[0]).start()
        pltpu.make_async_copy(src, xs_hbm.at[pl.ds(ib_smem[r], 1), :], row_sem.at[1]).start()
        return carry
    lax.fori_loop(0, tm, body, 0, unroll=GATHER_UNROLL)
    pltpu.make_async_copy(h_ref, xs_hbm.at[pl.ds(0, tm), :], row_sem.at[0]).wait()
    pltpu.make_async_copy(h_ref, xs_hbm.at[pl.ds(0, tm), :], row_sem.at[1]).wait()


def _moe_group_kernel(te_ref, used_ref, xs_ref, gn_ref, wg_ref, wu_ref, wd_ref, y_ref, xn_scr):
    i = pl.program_id(0)
    j = pl.program_id(1)

    @pl.when(used_ref[i] == 0)
    def _():
        y_ref[...] = jnp.zeros(y_ref.shape, F32)

    @pl.when(used_ref[i] != 0)
    def _():
        @pl.when(j == 0)
        def _():
            xn_scr[...] = _rms(xs_ref[...], gn_ref[...]).astype(BF16)

        contrib = _swiglu(xn_scr[...], wg_ref.at[0], wu_ref.at[0], wd_ref.at[0])

        @pl.when(j == 0)
        def _():
            y_ref[...] = contrib

        @pl.when(j > 0)
        def _():
            y_ref[...] += contrib


def _moe_combine_kernel(pa_hbm, pb_hbm, y_hbm, h_ref, sel_ref, out_ref, ia_smem, ib_smem, buf_a, buf_b,
                        row_sem, idx_sem, *, tm):
    i = pl.program_id(0)
    ca = _load_indices(pa_hbm, ia_smem, idx_sem.at[0], i, tm)
    cb = _load_indices(pb_hbm, ib_smem, idx_sem.at[1], i, tm)
    ca.start()
    cb.start()
    ca.wait()
    cb.wait()

    def body(r, carry):
        pltpu.make_async_copy(y_hbm.at[pl.ds(ia_smem[r], 1), :], buf_a.at[pl.ds(r, 1), :], row_sem.at[0]).start()
        pltpu.make_async_copy(y_hbm.at[pl.ds(ib_smem[r], 1), :], buf_b.at[pl.ds(r, 1), :], row_sem.at[1]).start()
        return carry
    lax.fori_loop(0, tm, body, 0, unroll=GATHER_UNROLL)
    pltpu.make_async_copy(buf_a, buf_a, row_sem.at[0]).wait()
    pltpu.make_async_copy(buf_b, buf_b, row_sem.at[1]).wait()
    sel = sel_ref[...]
    out_ref[...] = h_ref[...] + (sel[:, 2:3] * buf_a[...] + sel[:, 3:4] * buf_b[...])


def moe_swiglu_top2(h2d, g_norm, w_router, w_gu, w_down, *, n_chunks=2):
    t_total, d = h2d.shape
    tm = MOE_TILE
    ff = w_down.shape[1]
    tf = ff // n_chunks
    sel = moe_route(h2d, g_norm, w_router)
    n_slots, tile_expert, tile_used, pos_a, pos_b = _moe_plan(sel, tm)
    any_spec = pl.BlockSpec(memory_space=pl.ANY)
    index_scratch = [pltpu.SMEM((tm,), jnp.int32), pltpu.SMEM((tm,), jnp.int32)]
    sems = [pltpu.SemaphoreType.DMA((2,)), pltpu.SemaphoreType.DMA((2,))]
    xs = pl.pallas_call(
        functools.partial(_moe_dispatch_kernel, tm=tm),
        grid=(t_total // tm,),
        in_specs=[any_spec, any_spec, pl.BlockSpec((tm, d), lambda i: (i, 0)), any_spec],
        out_specs=any_spec,
        out_shape=jax.ShapeDtypeStruct((n_slots, d), F32),
        scratch_shapes=index_scratch + sems,
        input_output_aliases={3: 0},
        compiler_params=_cparams(1),
        name="moe_dispatch",
    )(pos_a, pos_b, h2d, jnp.zeros((n_slots, d), F32))
    w_gu = w_gu.astype(BF16)
    row = lambda i, j, te, used: (i, 0)
    y = pl.pallas_call(
        _moe_group_kernel,
        grid_spec=pltpu.PrefetchScalarGridSpec(
            num_scalar_prefetch=2,
            grid=(n_slots // tm, n_chunks),
            in_specs=[
                pl.BlockSpec((tm, d), row),
                pl.BlockSpec((1, d), lambda i, j, te, used: (0, 0)),
                pl.BlockSpec((1, d, tf), lambda i, j, te, used: (te[i], 0, j)),
                pl.BlockSpec((1, d, tf), lambda i, j, te, used: (te[i], 0, j + n_chunks)),
                pl.BlockSpec((1, tf, d), lambda i, j, te, used: (te[i], j, 0)),
            ],
            out_specs=pl.BlockSpec((tm, d), row),
            scratch_shapes=[pltpu.VMEM((tm, d), BF16)],
        ),
        out_shape=jax.ShapeDtypeStruct((n_slots, d), F32),
        compiler_params=_cparams(2),
        name="moe_group_ffn",
    )(tile_expert, tile_used, xs, g_norm.reshape(1, d), w_gu, w_gu, w_down.astype(BF16))
    return pl.pallas_call(
        functools.partial(_moe_combine_kernel, tm=tm),
        grid=(t_total // tm,),
        in_specs=[any_spec, any_spec, any_spec, pl.BlockSpec((tm, d), lambda i: (i, 0)),
                  pl.BlockSpec((tm, LANES), lambda i: (i, 0))],
        out_specs=pl.BlockSpec((tm, d), lambda i: (i, 0)),
        out_shape=jax.ShapeDtypeStruct((t_total, d), F32),
        scratch_shapes=index_scratch + [pltpu.VMEM((tm, d), F32), pltpu.VMEM((tm, d), F32)] + sems,
        compiler_params=_cparams(1),
        name="moe_combine",
    )(pos_a, pos_b, y, h2d, sel)


B_HEADS = 16
B_KV_HEADS = 4
B_GROUP = 4
B_HEAD_DIM = 64
CMP_STRIDE = 16
CMP_BLOCK = 32
SEL_BLOCK = 32
N_SELECT = 8
WINDOW = 512
SEL_BIG = 1e9
TINY = 1e-30
N_KV_KINDS = 6

LOG2E = 1.4426950408889634
POS_RADIX = 64
N_SLOPE_PARTS = 3


def _alibi_key_part(pos, idx):
    hi = (pos // POS_RADIX * POS_RADIX).astype(F32)
    lo = (pos % POS_RADIX).astype(F32)
    return jnp.where(idx < N_SLOPE_PARTS, hi, jnp.where(idx < 2 * N_SLOPE_PARTS, lo, 0.0))


def _alibi_query_part(slopes):
    parts, rest = [], slopes.astype(F32)
    for _ in range(N_SLOPE_PARTS):
        p = rest.astype(BF16)
        parts.append(p)
        rest = rest - p.astype(F32)
    return jnp.stack(parts + parts, axis=-1)


def _kv_proj_kernel(x_ref, gn_ref, wk_ref, kgain_ref, wvt_ref, wc_ref,
                    ks_ref, kw_ref, vst_ref, vwt_ref, kc_ref, vc_ref):
    g = B_KV_HEADS
    hn = _rms(x_ref[...], gn_ref[...]).astype(BF16)
    k = _dot(hn, wk_ref[...])
    tm = k.shape[0]
    pos = pl.program_id(1) * tm + lax.broadcasted_iota(jnp.int32, (tm, LANES), 0)
    lane = lax.broadcasted_iota(jnp.int32, (tm, LANES), 1)
    pos_lanes = jnp.where(lane >= B_HEAD_DIM, _alibi_key_part(pos, lane - B_HEAD_DIM), 0.0)
    for j in range(2 * g):
        kj = k[:, j * LANES:(j + 1) * LANES]
        ms = jnp.sum(kj * kj, axis=-1, keepdims=True) * (1.0 / B_HEAD_DIM)
        kj = (kj * lax.rsqrt(ms + NORM_EPS) * kgain_ref[:, j * LANES:(j + 1) * LANES] + pos_lanes).astype(BF16)
        if j < g:
            ks_ref[0, j] = kj
        else:
            kw_ref[0, j - g] = kj
    vt = _dot_nt(wvt_ref[...], hn)
    row = lax.broadcasted_iota(jnp.int32, vt.shape, 0)
    vt = jnp.where(row % LANES >= B_HEAD_DIM, 1.0, vt).astype(BF16)
    for j in range(g):
        vst_ref[0, j] = vt[j * LANES:(j + 1) * LANES]
        vwt_ref[0, j] = vt[(g + j) * LANES:(g + j + 1) * LANES]
    c = _dot(hn, wc_ref[...])
    half = g * B_HEAD_DIM
    kc_ref[...] = c[:, :half].astype(BF16)
    vc_ref[...] = c[:, half:].astype(BF16)


def nsa_kv_proj(h2d, bsz, seq, kv_norm, kv_w, k_norm, *, tm=512):
    t_total, d = h2d.shape
    g, dh = B_KV_HEADS, B_HEAD_DIM
    n_sblk = seq // tm
    w = kv_w.reshape(d, N_KV_KINDS, g, dh)
    def head_slabs(kinds):
        cols = jnp.stack([w[:, kind] for kind in kinds], axis=1)
        return jnp.pad(cols, ((0, 0), (0, 0), (0, 0), (0, LANES - dh))).reshape(d, len(kinds) * g * LANES)

    wk = head_slabs((2, 4)).astype(BF16)
    wvt = head_slabs((3, 5)).T.astype(BF16)
    kgain = jnp.pad(jnp.stack([jnp.tile(k_norm[1], (g, 1)), jnp.tile(k_norm[2], (g, 1))]),
                    ((0, 0), (0, 0), (0, LANES - dh))).reshape(1, 2 * g * LANES)
    wc = jnp.concatenate([w[:, 0].reshape(d, g * dh), w[:, 1].reshape(d, g * dh)], axis=1).astype(BF16)
    const2 = lambda b, s: (0, 0)
    row = lambda b, s: (b * n_sblk + s, 0)
    k_spec = pl.BlockSpec((1, g, tm, LANES), lambda b, s: (b, 0, s, 0))
    vt_spec = pl.BlockSpec((1, g, LANES, tm), lambda b, s: (b, 0, 0, s))
    k_shape = jax.ShapeDtypeStruct((bsz, g, seq, LANES), BF16)
    vt_shape = jax.ShapeDtypeStruct((bsz, g, LANES, seq), BF16)
    c_shape = jax.ShapeDtypeStruct((t_total, g * dh), BF16)
    return pl.pallas_call(
        _kv_proj_kernel,
        grid=(bsz, n_sblk),
        in_specs=[
            pl.BlockSpec((tm, d), row),
            _resident((1, d), const2),
            _resident((d, 2 * g * LANES), const2),
            _resident((1, 2 * g * LANES), const2),
            _resident((2 * g * LANES, d), const2),
            _resident((d, 2 * g * dh), const2),
        ],
        out_specs=[k_spec, k_spec, vt_spec, vt_spec,
                   pl.BlockSpec((tm, g * dh), row), pl.BlockSpec((tm, g * dh), row)],
        out_shape=[k_shape, k_shape, vt_shape, vt_shape, c_shape, c_shape],
        compiler_params=_cparams(2),
        name="nsa_kv_proj",
    )(h2d, kv_norm.reshape(1, d), wk, kgain, wvt, wc)


def _compress_kernel(ak_ref, av_ref, posk_ref, posv_ref, w1k_ref, w1v_ref, w2k_ref, w2vt_ref, kgain_ref,
                     kca_ref, vct_ref, *, n_half):
    half_w = ak_ref.shape[1]

    def hidden(a_ref, pos_ref, w1_ref):
        a = a_ref[...]
        top = _dot(a, w1_ref[:half_w, :])
        bot = _dot(a, w1_ref[half_w:, :])
        bias = _dot(pos_ref[...], w1_ref[...])[0:1, :]
        pre = top + pltpu.roll(bot, shift=a.shape[0] - 1, axis=0) + bias
        return (pre * _sigmoid(pre)).astype(BF16)

    act_k = hidden(ak_ref, posk_ref, w1k_ref)
    kc = _dot(act_k, w2k_ref[...])
    ms = jnp.sum(kc * kc, axis=-1, keepdims=True) * (1.0 / B_HEAD_DIM)
    kc = kc * lax.rsqrt(ms + NORM_EPS) * kgain_ref[...]
    blk = lax.broadcasted_iota(jnp.int32, kc.shape, 0) % n_half
    lane = lax.broadcasted_iota(jnp.int32, kc.shape, 1)
    c_end = blk * CMP_STRIDE + (CMP_BLOCK - 1)
    kca_ref[...] = (kc + jnp.where(lane >= B_HEAD_DIM, _alibi_key_part(c_end, lane - B_HEAD_DIM), 0.0)).astype(BF16)
    act_v = hidden(av_ref, posv_ref, w1v_ref)
    vct = _dot_nt(w2vt_ref[...], act_v).astype(BF16)
    for j in range(vct.shape[1] // n_half):
        vct_ref[j] = vct[:, j * n_half:(j + 1) * n_half]


def nsa_compress(kc_raw, vc_raw, bsz, seq, pos_k, w1_k, w2_k, pos_v, w1_v, w2_v, k_gain, *, groups_per_step=4):
    g, dh = B_KV_HEADS, B_HEAD_DIM
    n_half = seq // CMP_STRIDE
    feat = CMP_STRIDE * dh

    def to_half_blocks(a):
        a = a.reshape(bsz, n_half, CMP_STRIDE, g, dh).transpose(0, 3, 1, 2, 4)
        return a.reshape(bsz * g * n_half, feat)

    def pos_rows(pos):
        return jnp.zeros((8, 2 * feat), F32).at[0].set(pos.reshape(-1)).astype(BF16)

    hid = w1_k.shape[1]
    pad_lanes = lambda a: jnp.pad(a, ((0, 0), (0, LANES - dh)))
    rows = groups_per_step * n_half
    const2 = lambda i: (0, 0)
    return pl.pallas_call(
        functools.partial(_compress_kernel, n_half=n_half),
        grid=(bsz * g // groups_per_step,),
        in_specs=[
            pl.BlockSpec((rows, feat), lambda i: (i, 0)),
            pl.BlockSpec((rows, feat), lambda i: (i, 0)),
            _resident((8, 2 * feat), const2),
            _resident((8, 2 * feat), const2),
            _resident((2 * feat, hid), const2),
            _resident((2 * feat, hid), const2),
            _resident((hid, LANES), const2),
            _resident((LANES, hid), const2),
            _resident((1, LANES), const2),
        ],
        out_specs=[pl.BlockSpec((rows, LANES), lambda i: (i, 0)),
                   pl.BlockSpec((groups_per_step, LANES, n_half), lambda i: (i, 0, 0))],
        out_shape=[jax.ShapeDtypeStruct((bsz * g * n_half, LANES), BF16),
                   jax.ShapeDtypeStruct((bsz * g, LANES, n_half), BF16)],
        compiler_params=_cparams(1),
        name="nsa_compress",
    )(to_half_blocks(kc_raw), to_half_blocks(vc_raw), pos_rows(pos_k), pos_rows(pos_v),
      w1_k.astype(BF16), w1_v.astype(BF16), pad_lanes(w2_k).astype(BF16), pad_lanes(w2_v).T.astype(BF16),
      pad_lanes(k_gain.reshape(1, dh)))


def _q_proj_kernel(x_ref, gn_ref, w_ref, qgain_ref, qconst_ref, q_ref, gate_ref):
    d_q = q_ref.shape[1]
    xn = _rms(x_ref[...], gn_ref[...]).astype(BF16)
    proj = _dot(xn, w_ref[...])
    ones = jnp.ones((LANES, LANES), BF16)
    heads = [slice(h * LANES, (h + 1) * LANES) for h in range(d_q // LANES)]
    splits = [_split_bf16(proj[:, lanes] * proj[:, lanes]) for lanes in heads]
    sums = [_dot(hi, ones) + _dot(lo, ones) for hi, lo in splits]
    for lanes, ss in zip(heads, sums):
        q_ref[:, lanes] = (proj[:, lanes] * lax.rsqrt(ss * (1.0 / B_HEAD_DIM) + NORM_EPS) * qgain_ref[:, lanes]
                           + qconst_ref[:, lanes]).astype(BF16)
    gate_ref[...] = _sigmoid(proj[:, d_q:])


def nsa_q_proj(h2d, g_norm, w_in, q_norm, *, tm=512):
    t_total, d = h2d.shape
    g, hpg, dh = B_KV_HEADS, B_GROUP, B_HEAD_DIM
    d_q = B_HEADS * LANES
    wq = jnp.pad(w_in[:, :B_HEADS * dh].reshape(d, B_HEADS, dh), ((0, 0), (0, 0), (0, LANES - dh))).reshape(d, d_q)
    wg = w_in[:, B_HEADS * dh:].reshape(d, g, hpg * 3)
    wg = jnp.pad(wg, ((0, 0), (0, 0), (0, LANES - hpg * 3))).reshape(d, g * LANES)
    w = jnp.concatenate([wq, wg], axis=1).astype(BF16)
    qgain = jnp.tile(jnp.pad(q_norm * (dh ** -0.5 * LOG2E), (0, LANES - dh)), B_HEADS).reshape(1, d_q)
    slopes = np.array([2.0 ** (-8.0 * (h + 1) / B_HEADS) for h in range(B_HEADS)], dtype=np.float64) * LOG2E
    feats = _alibi_query_part(jnp.asarray(slopes, F32)).astype(F32)
    qconst = jnp.pad(feats, ((0, 0), (dh, LANES - dh - feats.shape[1]))).reshape(1, d_q)
    const2 = lambda i: (0, 0)
    return pl.pallas_call(
        _q_proj_kernel,
        grid=(t_total // tm,),
        in_specs=[
            pl.BlockSpec((tm, d), lambda i: (i, 0)),
            _resident((1, d), const2),
            _resident((d, d_q + g * LANES), const2),
            _resident((1, d_q), const2),
            _resident((1, d_q), const2),
        ],
        out_specs=[pl.BlockSpec((tm, d_q), lambda i: (i, 0)), pl.BlockSpec((tm, g * LANES), lambda i: (i, 0))],
        out_shape=[jax.ShapeDtypeStruct((t_total, d_q), BF16), jax.ShapeDtypeStruct((t_total, g * LANES), F32)],
        compiler_params=_cparams(1),
        name="nsa_q_proj",
    )(h2d, g_norm.reshape(1, d), w, qgain, qconst)


def _nsa_attn_kernel(q_ref, gate_ref, kca_ref, vct_ref, ks_ref, vst_ref, kw_ref, vwt_ref,
                     ovlt_ref, eselt_ref, o_ref, live_smem, m_scr, acc_scr, *, tq, tk):
    hpg, dh = B_GROUP, B_HEAD_DIM
    t0 = pl.program_id(2) * tq
    rows = hpg * tq
    qb = q_ref[...]
    q_st = jnp.concatenate([qb[:, i * LANES:(i + 1) * LANES] for i in range(hpg)], axis=0)

    n_half = kca_ref.shape[0]
    c_end = lax.broadcasted_iota(jnp.int32, (n_half, 1), 0) * CMP_STRIDE + (CMP_BLOCK - 1)
    t_lane = t0 + lax.broadcasted_iota(jnp.int32, (1, rows), 1) % tq
    seen = t_lane >= c_end
    s = jnp.where(seen, _dot_nt(kca_ref[...], q_st), NEG)
    m = jnp.max(s, axis=0, keepdims=True)
    e = jnp.where(seen, jnp.exp2(s - m), 0.0)
    p = e / jnp.maximum(jnp.sum(e, axis=0, keepdims=True), TINY)
    o_cmp = _dot(vct_ref[0], p.astype(BF16))[:dh]

    p_grp = p[:, 0:tq]
    for i in range(1, hpg):
        p_grp = p_grp + p[:, i * tq:(i + 1) * tq]
    p_hi, p_lo = _split_bf16(p_grp)
    n_blk = LANES // 2
    imp = (_dot(ovlt_ref[...], p_hi) + _dot(ovlt_ref[...], p_lo))[:n_blk]
    blk = lax.broadcasted_iota(jnp.int32, (n_blk, tq), 0)
    t_l = t0 + lax.broadcasted_iota(jnp.int32, (1, tq), 1)
    valid = blk * SEL_BLOCK <= t_l
    cur = t_l // SEL_BLOCK
    forced = valid & ((blk == 0) | (blk == cur) | (blk == cur - 1))
    work = jnp.where(forced, SEL_BIG, jnp.where(valid, imp, -SEL_BIG))
    blk_f = blk.astype(F32)
    chosen = jnp.zeros((n_blk, tq), F32)
    for _ in range(N_SELECT):
        top = jnp.max(work, axis=0, keepdims=True)
        first = jnp.min(jnp.where(work == top, blk_f, float(LANES)), axis=0, keepdims=True)
        hit = blk_f == first
        chosen = jnp.where(hit & (top > -0.5 * SEL_BIG), 1.0, chosen)
        work = jnp.where(hit, -3e38, work)
    blocks_per_tile = tk // SEL_BLOCK
    any_q = jnp.max(chosen, axis=1, keepdims=True)
    for kt in range(n_blk // blocks_per_tile):
        tile_any = jnp.max(any_q[kt * blocks_per_tile:(kt + 1) * blocks_per_tile])
        live_smem[kt] = (tile_any > 0.5).astype(jnp.int32)
    chosen = jnp.concatenate([chosen, jnp.zeros((LANES - n_blk, tq), F32)], axis=0).astype(BF16)

    def normalised(acc):
        return acc[:dh] / acc[dh:2 * dh]

    span = WINDOW + tq
    w0 = pl.multiple_of(jnp.maximum(t0 - WINDOW, 0), LANES)
    dist = t_l - (w0 + lax.broadcasted_iota(jnp.int32, (span, 1), 0))
    pen = jnp.where((dist >= 0) & (dist < WINDOW), 0.0, NEG)
    s = _dot_nt(kw_ref[0, 0, pl.ds(w0, span), :], q_st) + jnp.concatenate([pen] * hpg, axis=1)
    m = jnp.max(s, axis=0, keepdims=True)
    o_win = normalised(_dot(vwt_ref[0, 0, :, pl.ds(w0, span)], jnp.exp2(s - m).astype(BF16)))

    m_scr[...] = jnp.full(m_scr.shape, NEG, F32)
    acc_scr[...] = jnp.zeros(acc_scr.shape, F32)

    def sel_step(kt, carry):
        @pl.when(live_smem[kt] != 0)
        def _():
            k0 = pl.multiple_of(kt * tk, tk)
            pos = k0 + lax.broadcasted_iota(jnp.int32, (tk, 1), 0)
            picked = _dot(eselt_ref[pl.ds(k0, tk), :], chosen)
            pen = jnp.where((picked > 0.5) & (pos <= t_l), 0.0, NEG)
            sc = _dot_nt(ks_ref[0, 0, pl.ds(k0, tk), :], q_st) + jnp.concatenate([pen] * hpg, axis=1)
            m_run = m_scr[...]
            m_new = jnp.maximum(m_run, jnp.broadcast_to(jnp.max(sc, axis=0, keepdims=True), m_run.shape))
            pr = jnp.exp2(sc - m_new[0:1]).astype(BF16)
            acc_scr[...] = (jnp.exp2(m_run - m_new)[0:1] * acc_scr[...]
                            + _dot(vst_ref[0, 0, :, pl.ds(k0, tk)], pr))
            m_scr[...] = m_new
        return carry

    lax.fori_loop(0, (t0 + tq + tk - 1) // tk, sel_step, 0)
    o_sel = normalised(acc_scr[...])

    gate_t = gate_ref[...].T
    outs = []
    for i in range(hpg):
        c = slice(i * tq, (i + 1) * tq)
        outs.append(gate_t[3 * i:3 * i + 1] * o_cmp[:, c] + gate_t[3 * i + 1:3 * i + 2] * o_sel[:, c]
                    + gate_t[3 * i + 2:3 * i + 3] * o_win[:, c])
    o_ref[...] = jnp.concatenate(outs, axis=0).T.astype(BF16)


def nsa_attention(q, gates, kca, vct, ks, vst, kw, vwt, bsz, seq, *, tq=256, tk=512):
    g, hpg, dh = B_KV_HEADS, B_GROUP, B_HEAD_DIM
    n_q = seq // tq
    n_half = seq // CMP_STRIDE
    n_cmp = n_half - 1
    n_sel = seq // SEL_BLOCK
    assert n_sel <= LANES // 2 and tq % LANES == 0
    c_start = np.arange(n_half) * CMP_STRIDE
    j_sel = np.arange(LANES)
    ovl = ((c_start[:, None] < (j_sel[None, :] + 1) * SEL_BLOCK)
           & (c_start[:, None] + CMP_BLOCK > j_sel[None, :] * SEL_BLOCK)
           & (np.arange(n_half)[:, None] < n_cmp) & (j_sel[None, :] < n_sel))
    esel = (np.arange(seq)[None, :] // SEL_BLOCK) == j_sel[:, None]
    row = lambda b, j, i: (b * n_q + i, j)
    per_bg = lambda b, j, i: (b, j, 0, 0)
    return pl.pallas_call(
        functools.partial(_nsa_attn_kernel, tq=tq, tk=tk),
        grid=(bsz, g, n_q),
        in_specs=[
            pl.BlockSpec((tq, hpg * LANES), row),
            pl.BlockSpec((tq, LANES), row),
            pl.BlockSpec((n_half, LANES), lambda b, j, i: (b * g + j, 0)),
            pl.BlockSpec((1, LANES, n_half), lambda b, j, i: (b * g + j, 0, 0)),
            pl.BlockSpec((1, 1, seq, LANES), per_bg),
            pl.BlockSpec((1, 1, LANES, seq), per_bg),
            pl.BlockSpec((1, 1, seq, LANES), per_bg),
            pl.BlockSpec((1, 1, LANES, seq), per_bg),
            _resident((LANES, n_half), lambda b, j, i: (0, 0)),
            _resident((seq, LANES), lambda b, j, i: (0, 0)),
        ],
        out_specs=pl.BlockSpec((tq, hpg * dh), row),
        out_shape=jax.ShapeDtypeStruct((bsz * seq, B_HEADS * dh), BF16),
        scratch_shapes=[pltpu.SMEM((seq // tk,), jnp.int32), pltpu.VMEM((SUBLANES, hpg * tq), F32),
                        pltpu.VMEM((LANES, hpg * tq), F32)],
        compiler_params=_cparams(3),
        name="nsa_attention",
    )(q, gates, kca, vct, ks, vst, kw, vwt, jnp.asarray(ovl.T, BF16), jnp.asarray(esel.T, BF16))


def _out_proj_kernel(h_ref, o_ref, w_ref, out_ref):
    out_ref[...] = h_ref[...] + _dot(o_ref[...], w_ref[...])


def out_proj_residual(h2d, o, w_out, *, tm=512):
    t_total, d = h2d.shape
    k = o.shape[1]
    return pl.pallas_call(
        _out_proj_kernel,
        grid=(t_total // tm,),
        in_specs=[pl.BlockSpec((tm, d), lambda i: (i, 0)), pl.BlockSpec((tm, k), lambda i: (i, 0)),
                  _resident((k, d), lambda i: (0, 0))],
        out_specs=pl.BlockSpec((tm, d), lambda i: (i, 0)),
        out_shape=jax.ShapeDtypeStruct((t_total, d), F32),
        compiler_params=_cparams(1),
        name="out_proj_residual",
    )(h2d, o, w_out.astype(BF16))


def nsa_shared_kv(h2d, bsz, seq, kv_norm, kv_w, cmp_pos_k, cmp_w1_k, cmp_w2_k, cmp_pos_v, cmp_w1_v, cmp_w2_v, k_norm):
    ks, kw, vst, vwt, kc_raw, vc_raw = nsa_kv_proj(h2d, bsz, seq, kv_norm, kv_w, k_norm)
    kca, vct = nsa_compress(kc_raw, vc_raw, bsz, seq, cmp_pos_k, cmp_w1_k, cmp_w2_k,
                            cmp_pos_v, cmp_w1_v, cmp_w2_v, k_norm[0])
    return kca, vct, ks, vst, kw, vwt


def nsa_mixer(h2d, bsz, seq, g_norm, w_in, w_out, q_norm, kv):
    q, gates = nsa_q_proj(h2d, g_norm, w_in, q_norm)
    o = nsa_attention(q, gates, *kv, bsz, seq)
    return out_proj_residual(h2d, o, w_out)


def kernel(x, norm_mix, norm_ffn, hgrn_w_in, hgrn_w_out, hgrn_g_norm, hgrn_lb_logits, kv_norm, kv_w, cmp_pos_k, cmp_w1_k, cmp_w2_k, cmp_pos_v, cmp_w1_v, cmp_w2_v, k_norm, nsa_w_in, nsa_w_out, nsa_q_norm, ffn_w_gu, ffn_w_down, moe_router, moe_w_gu, moe_w_down):
    bsz, seq, d = x.shape
    lb = jnp.cumsum(jax.nn.softmax(hgrn_lb_logits.astype(F32), axis=0), axis=0)
    lb = lb - lb[0:1]
    n_a = hgrn_w_in.shape[0]
    depth = norm_mix.shape[0]
    h = x.reshape(bsz * seq, d)
    kv = None
    for layer in range(depth):
        if layer < n_a:
            h = hgrn_mixer(h, seq, norm_mix[layer], hgrn_w_in[layer], hgrn_w_out[layer], hgrn_g_norm[layer],
                           lb[layer])
        else:
            if kv is None:
                kv = nsa_shared_kv(h, bsz, seq, kv_norm, kv_w, cmp_pos_k, cmp_w1_k, cmp_w2_k,
                                   cmp_pos_v, cmp_w1_v, cmp_w2_v, k_norm)
            j = layer - n_a
            h = nsa_mixer(h, bsz, seq, norm_mix[layer], nsa_w_in[j], nsa_w_out[j], nsa_q_norm[j], kv)
        if layer % 2 == 0:
            h = ffn_swiglu(h, norm_ffn[layer], ffn_w_gu[layer // 2], ffn_w_down[layer // 2])
        else:
            h = moe_swiglu_top2(h, norm_ffn[layer], moe_router[layer // 2], moe_w_gu[layer // 2],
                                moe_w_down[layer // 2])
    return h.reshape(bsz, seq, d)
```

```python
import functools

import numpy as np
import jax
import jax.numpy as jnp
from jax import lax
from jax.experimental import pallas as pl
from jax.experimental.pallas import tpu as pltpu

F32 = jnp.float32
BF16 = jnp.bfloat16

NORM_EPS = 1e-6
VMEM_LIMIT_BYTES = 56 * 1024 * 1024

A_HEADS = 8
A_HEAD_DIM = 128
SCAN_ROWS = 128
SUBLANES = 8


def _cparams(n_axes):
    return pltpu.CompilerParams(dimension_semantics=("arbitrary",) * n_axes,
                                vmem_limit_bytes=VMEM_LIMIT_BYTES)


def _dot(a, b):
    return jnp.dot(a, b, preferred_element_type=F32)


def _dot_nt(a, b):
    return lax.dot_general(a, b, (((1,), (1,)), ((), ())), preferred_element_type=F32)


def _dot_tn(a, b):
    return lax.dot_general(a, b, (((0,), (0,)), ((), ())), preferred_element_type=F32)


def _sigmoid(x):
    return 0.5 * jnp.tanh(0.5 * x) + 0.5


def _rms(x, g):
    return x * lax.rsqrt(jnp.mean(x * x, axis=-1, keepdims=True) + NORM_EPS) * g


def _split_bf16(x):
    hi = x.astype(BF16)
    lo = (x - hi.astype(F32)).astype(BF16)
    return hi, lo


def _hgrn_constants():
    n = SCAN_ROWS
    t = np.arange(n)
    r = np.arange(n)[None, :]
    sums, masks = [], []
    m = n // 2
    while m >= 1:
        grp = t // (2 * m)
        mid = grp * 2 * m + m
        upper = (t % (2 * m)) >= m
        if m < SUBLANES:
            sums.append(np.where(upper[:, None], (r >= mid[:, None]) & (r <= t[:, None]),
                                 (r > t[:, None]) & (r < mid[:, None])))
        masks.append((grp[:, None] == grp[None, :]) & upper[:, None] & (~upper[None, :]))
        m //= 2
    masks.append(np.eye(n, dtype=bool))
    sums.append(r <= t[:, None])
    w = np.concatenate(sums, 0).astype(np.float32)
    return np.concatenate([w, w], 1), np.stack(masks).astype(np.float32)


def _hgrn_kernel(x_ref, gn_ref, win_ref, lb_ref, ghn_ref, wout_ref, wsum_ref, msk_ref,
                 out_ref, xn_scr, st_scr, og_scr, *, tm):
    sblk = pl.program_id(1)
    h = pl.program_id(2)
    n = SCAN_ROWS
    n_lvl = msk_ref.shape[0] - 1

    @pl.when(h == 0)
    def _():
        xn_scr[...] = _rms(x_ref[...], gn_ref[...]).astype(BF16)

    @pl.when(sblk == 0)
    def _():
        st_scr[h] = jnp.zeros((A_HEAD_DIM, A_HEAD_DIM), F32)

    proj = _dot(xn_scr[...], win_ref[...])
    q = proj[:, 0:128]
    f = proj[:, 128:256]
    v = proj[:, 256:384]
    g = proj[:, 384:512]
    lb = lb_ref[0:1, :]
    log_lb = lb_ref[1:2, :]
    log_1m_lb = lb_ref[2:3, :]
    qs = q * _sigmoid(q)
    a = jnp.exp(-jnp.abs(f))
    inv = 1.0 / (1.0 + a)
    log_sig = jnp.minimum(f, 0.0) - jnp.log1p(a)
    b = log_1m_lb + log_sig
    logf = jnp.maximum(log_lb, b) + jnp.log1p(jnp.exp(-jnp.abs(log_lb - b)))
    key = (1.0 - lb) * jnp.where(f >= 0.0, a * inv, inv)

    wide = [n >> (l + 1) for l in range(n_lvl) if n >> (l + 1) >= SUBLANES]
    row_id = lax.broadcasted_iota(jnp.int32, (n, A_HEAD_DIM), 0)
    in_upper_half = [(row_id % (2 * m)) >= m for m in wide]

    blocks = [slice(r * n, (r + 1) * n) for r in range(tm // n)]
    n_narrow = n_lvl - len(wide)
    sums = []
    for rows in blocks:
        hi, lo = _split_bf16(logf[rows])
        sums.append(_dot(wsum_ref[...], jnp.concatenate([hi, lo], axis=0)))
    e_lvl, e_prefix, e_suffix = [], [], []
    for sm in sums:
        cum = sm[n_narrow * n:]
        e_blk = []
        for m, upper in zip(wide, in_upper_half):
            bound = jnp.concatenate([jnp.broadcast_to(cum[i + m - 1:i + m], (2 * m, A_HEAD_DIM))
                                     for i in range(0, n, 2 * m)], axis=0)
            e_blk.append(jnp.exp(jnp.where(upper, cum - bound, bound - cum)))
        e_sm = jnp.exp(sm)
        e_lvl.append(e_blk + [e_sm[i * n:(i + 1) * n] for i in range(n_narrow)])
        e_prefix.append(e_sm[n_narrow * n:])
        e_suffix.append(jnp.exp(cum[n - 1:n] - cum))
    intra, inter_q, kv = [], [], []
    for rows, e_blk, e_pre, e_suf in zip(blocks, e_lvl, e_prefix, e_suffix):
        qb = qs[rows]
        kb = key[rows]
        vb = v[rows].astype(BF16)
        s = msk_ref[n_lvl] * _dot_nt(qb.astype(BF16), kb.astype(BF16))
        for l in range(n_lvl):
            s = s + msk_ref[l] * _dot_nt((qb * e_blk[l]).astype(BF16), (kb * e_blk[l]).astype(BF16))
        intra.append(_dot(s.astype(BF16), vb))
        inter_q.append((qb * e_pre).astype(BF16))
        kv.append(_dot_tn(vb, (kb * e_suf).astype(BF16)))
    st = st_scr[h]
    outs = []
    for o_intra, q_pre, e_pre, kv_blk in zip(intra, inter_q, e_prefix, kv):
        outs.append(o_intra + _dot_nt(q_pre, st.astype(BF16)))
        st = st * e_pre[n - 1:n, :] + kv_blk
    st_scr[h] = st
    o = jnp.concatenate(outs, axis=0)

    og_scr[:, pl.ds(pl.multiple_of(h * A_HEAD_DIM, A_HEAD_DIM), A_HEAD_DIM)] = (
        _rms(o, ghn_ref[...]) * (g * _sigmoid(g))).astype(BF16)

    @pl.when(h == A_HEADS - 1)
    def _():
        out_ref[...] = x_ref[...] + _dot(og_scr[...], wout_ref[...])


def hgrn_mixer(h2d, seq, g_norm_in, w_in, w_out, g_head, lb, *, tm=512):
    t_total, d = h2d.shape
    n_sblk = seq // tm
    hd = A_HEAD_DIM
    w_perm = w_in.reshape(d, 4, A_HEADS, hd).transpose(0, 2, 1, 3).reshape(d, 4 * A_HEADS * hd).astype(BF16)
    lb_rows = jnp.zeros((8, A_HEADS * hd), F32)
    lb_rows = lb_rows.at[0].set(lb).at[1].set(jnp.log(lb)).at[2].set(jnp.log1p(-lb))
    wsum, masks = _hgrn_constants()
    n_blk = wsum.shape[0]
    grid = (t_total // seq, n_sblk, A_HEADS)
    row = lambda b, s, h: (b * n_sblk + s, 0)
    const2 = lambda b, s, h: (0, 0)
    return pl.pallas_call(
        functools.partial(_hgrn_kernel, tm=tm),
        grid=grid,
        in_specs=[
            pl.BlockSpec((tm, d), row),
            pl.BlockSpec((1, d), const2),
            pl.BlockSpec((d, 4 * hd), lambda b, s, h: (0, h)),
            pl.BlockSpec((8, hd), lambda b, s, h: (0, h)),
            pl.BlockSpec((1, hd), const2),
            _resident((A_HEADS * hd, d), const2),
            _resident((n_blk, 2 * SCAN_ROWS), const2),
            _resident(masks.shape, lambda b, s, h: (0, 0, 0)),
        ],
        out_specs=pl.BlockSpec((tm, d), row),
        out_shape=jax.ShapeDtypeStruct((t_total, d), F32),
        scratch_shapes=[pltpu.VMEM((tm, d), BF16), pltpu.VMEM((A_HEADS, hd, hd), F32),
                        pltpu.VMEM((tm, A_HEADS * hd), BF16)],
        compiler_params=_cparams(3),
        name="hgrn_mixer",
    )(h2d, g_norm_in.reshape(1, d), w_perm, lb_rows, g_head.reshape(1, hd), w_out.astype(BF16),
      jnp.asarray(wsum, BF16), jnp.asarray(masks, F32))


def _resident(block_shape, index_map):
    return pl.BlockSpec(block_shape, index_map, pipeline_mode=pl.Buffered(1))


MXU_TILE = 256


def _swiglu(xn, wg_ref, wu_ref, wd_ref):
    acc = None
    for c in range(wg_ref.shape[1] // MXU_TILE):
        cols = slice(c * MXU_TILE, (c + 1) * MXU_TILE)
        gate = _dot(xn, wg_ref[:, cols])
        up = _dot(xn, wu_ref[:, cols])
        part = _dot((gate * _sigmoid(gate) * up).astype(BF16), wd_ref[cols, :])
        acc = part if acc is None else acc + part
    return acc


def _ffn_kernel(x_ref, gn_ref, wg_ref, wu_ref, wd_ref, out_ref):
    x = x_ref[...]
    xn = _rms(x, gn_ref[...]).astype(BF16)
    out_ref[...] = x + _swiglu(xn, wg_ref, wu_ref, wd_ref)


def ffn_swiglu(h2d, g_norm, w_gu, w_down, *, tm=512):
    t_total, d = h2d.shape
    ff = w_down.shape[0]
    w_gu = w_gu.astype(BF16)
    const2 = lambda i: (0, 0)
    return pl.pallas_call(
        _ffn_kernel,
        grid=(t_total // tm,),
        in_specs=[
            pl.BlockSpec((tm, d), lambda i: (i, 0)),
            _resident((1, d), const2),
            _resident((d, ff), const2),
            _resident((d, ff), lambda i: (0, 1)),
            _resident((ff, d), const2),
        ],
        out_specs=pl.BlockSpec((tm, d), lambda i: (i, 0)),
        out_shape=jax.ShapeDtypeStruct((t_total, d), F32),
        compiler_params=_cparams(1),
        name="ffn_swiglu",
    )(h2d, g_norm.reshape(1, d), w_gu, w_gu, w_down.astype(BF16))


N_EXPERTS = 8
LANES = 128
NEG = -1e30


def _router_kernel(x_ref, gn_ref, wh_ref, wl_ref, sel_ref):
    xn = _rms(x_ref[...], gn_ref[...])
    xh, xl = _split_bf16(xn)
    logits = _dot(xh, wh_ref[...]) + (_dot(xl, wh_ref[...]) + _dot(xh, wl_ref[...]))
    lane = lax.broadcasted_iota(jnp.int32, logits.shape, 1).astype(F32)
    logits = jnp.where(lane < N_EXPERTS, logits, NEG)
    m1 = jnp.max(logits, axis=-1, keepdims=True)
    i1 = jnp.min(jnp.where(logits == m1, lane, float(LANES)), axis=-1, keepdims=True)
    rest = jnp.where(lane == i1, NEG, logits)
    m2 = jnp.max(rest, axis=-1, keepdims=True)
    i2 = jnp.min(jnp.where(rest == m2, lane, float(LANES)), axis=-1, keepdims=True)
    e2 = jnp.exp(m2 - m1)
    denom = 1.0 + e2
    sel_ref[...] = jnp.where(lane == 0, i1, jnp.where(lane == 1, i2, jnp.where(
        lane == 2, 1.0 / denom, jnp.where(lane == 3, e2 / denom, 0.0))))


def moe_route(h2d, g_norm, w_router, *, tm=512):
    t_total, d = h2d.shape
    w_pad = jnp.zeros((d, LANES), F32).at[:, :N_EXPERTS].set(w_router)
    w_hi = w_pad.astype(BF16)
    w_lo = (w_pad - w_hi.astype(F32)).astype(BF16)
    const2 = lambda i: (0, 0)
    return pl.pallas_call(
        _router_kernel,
        grid=(t_total // tm,),
        in_specs=[
            pl.BlockSpec((tm, d), lambda i: (i, 0)),
            _resident((1, d), const2),
            _resident((d, LANES), const2),
            _resident((d, LANES), const2),
        ],
        out_specs=pl.BlockSpec((tm, LANES), lambda i: (i, 0)),
        out_shape=jax.ShapeDtypeStruct((t_total, LANES), F32),
        compiler_params=_cparams(1),
        name="moe_router",
    )(h2d, g_norm.reshape(1, d), w_hi, w_lo)


MOE_TILE = 1024
GATHER_UNROLL = 8


def _moe_plan(sel, tm):
    t_total = sel.shape[0]
    e = sel[:, :2].astype(jnp.int32)
    onehot = (e[:, :, None] == jnp.arange(N_EXPERTS, dtype=jnp.int32)[None, None, :]).astype(jnp.int32).sum(1)
    csum = jnp.cumsum(onehot, axis=0)
    rank = csum - onehot
    padded = (csum[-1] + tm - 1) // tm * tm
    ends = jnp.cumsum(padded)
    pos = (ends - padded)[e] + jnp.take_along_axis(rank, e, axis=1)
    n_slots = 2 * t_total + N_EXPERTS * tm
    tile_start = jnp.arange(n_slots // tm, dtype=jnp.int32) * tm
    tile_expert = jnp.minimum(jnp.searchsorted(ends, tile_start, side="right"), N_EXPERTS - 1).astype(jnp.int32)
    tile_used = (tile_start < ends[-1]).astype(jnp.int32)
    return n_slots, tile_expert, tile_used, pos[:, 0], pos[:, 1]


def _load_indices(idx_hbm, idx_smem, sem, tile, tm):
    return pltpu.make_async_copy(idx_hbm.at[pl.ds(tile * tm, tm)], idx_smem, sem)


def _rows_to_tiles(x, dst_ref):
    n = x.shape[0]
    for s in range(SUBLANES):
        dst_ref[pl.ds(s, n, stride=SUBLANES), :] = x[:, s * LANES:(s + 1) * LANES]


def _tiles_to_rows(src_ref, n):
    return jnp.concatenate([src_ref[pl.ds(s, n, stride=SUBLANES), :] for s in range(SUBLANES)], axis=1)


def _tile_of(ref, r):
    return ref.at[pl.ds(pl.multiple_of(r * SUBLANES, SUBLANES), SUBLANES), :]


def _moe_dispatch_kernel(pa_hbm, pb_hbm, h_ref, zeros_hbm, xs_hbm, ia_smem, ib_smem, tiles, row_sem, idx_sem, *, tm):
    del zeros_hbm
    i = pl.program_id(0)
    ca = _load_indices(pa_hbm, ia_smem, idx_sem.at[0], i, tm)
    cb = _load_indices(pb_hbm, ib_smem, idx_sem.at[1], i, tm)
    ca.start()
    cb.start()
    _rows_to_tiles(h_ref[...], tiles)
    ca.wait()
    cb.wait()

    def body(r, carry):
        src = _tile_of(tiles, r)
        pltpu.make_async_copy(src, _tile_of(xs_hbm, ia_smem[r]), row_sem.at[0]).start()
        pltpu.make_async_copy(src, _tile_of(xs_hbm, ib_smem[r]), row_sem.at[1]).start()
        return carry
    lax.fori_loop(0, tm, body, 0, unroll=GATHER_UNROLL)
    everything = xs_hbm.at[pl.ds(0, tm * SUBLANES), :]
    pltpu.make_async_copy(tiles, everything, row_sem.at[0]).wait()
    pltpu.make_async_copy(tiles, everything, row_sem.at[1]).wait()


def _moe_group_kernel(te_ref, used_ref, xs_ref, gn_ref, wg_ref, wu_ref, wd_ref, y_ref, xn_scr, acc_scr):
    i = pl.program_id(0)
    j = pl.program_id(1)
    tm = xn_scr.shape[0]

    @pl.when(used_ref[i] == 0)
    def _():
        y_ref[...] = jnp.zeros(y_ref.shape, F32)

    @pl.when(used_ref[i] != 0)
    def _():
        @pl.when(j == 0)
        def _():
            xn_scr[...] = _rms(_tiles_to_rows(xs_ref, tm), gn_ref[...]).astype(BF16)

        contrib = _swiglu(xn_scr[...], wg_ref.at[0], wu_ref.at[0], wd_ref.at[0])
        last = pl.num_programs(1) - 1

        @pl.when(j == 0)
        def _():
            acc_scr[...] = contrib

        @pl.when((j > 0) & (j < last))
        def _():
            acc_scr[...] += contrib

        @pl.when(j == last)
        def _():
            _rows_to_tiles(acc_scr[...] + contrib, y_ref)


def _moe_combine_kernel(pa_hbm, pb_hbm, y_hbm, h_ref, sel_ref, out_ref, ia_smem, ib_smem, buf_a, buf_b,
                        row_sem, idx_sem, *, tm):
    i = pl.program_id(0)
    ca = _load_indices(pa_hbm, ia_smem, idx_sem.at[0], i, tm)
    cb = _load_indices(pb_hbm, ib_smem, idx_sem.at[1], i, tm)
    ca.start()
    cb.start()
    ca.wait()
    cb.wait()

    def body(r, carry):
        pltpu.make_async_copy(_tile_of(y_hbm, ia_smem[r]), _tile_of(buf_a, r), row_sem.at[0]).start()
        pltpu.make_async_copy(_tile_of(y_hbm, ib_smem[r]), _tile_of(buf_b, r), row_sem.at[1]).start()
        return carry
    lax.fori_loop(0, tm, body, 0, unroll=GATHER_UNROLL)
    pltpu.make_async_copy(buf_a, buf_a, row_sem.at[0]).wait()
    pltpu.make_async_copy(buf_b, buf_b, row_sem.at[1]).wait()
    sel = sel_ref[...]
    out_ref[...] = h_ref[...] + (sel[:, 2:3] * _tiles_to_rows(buf_a, tm) + sel[:, 3:4] * _tiles_to_rows(buf_b, tm))


def moe_swiglu_top2(h2d, g_norm, w_router, w_gu, w_down, *, n_chunks=2):
    t_total, d = h2d.shape
    assert d == SUBLANES * LANES and n_chunks >= 2
    tm = MOE_TILE
    ff = w_down.shape[1]
    tf = ff // n_chunks
    sel = moe_route(h2d, g_norm, w_router)
    n_slots, tile_expert, tile_used, pos_a, pos_b = _moe_plan(sel, tm)
    any_spec = pl.BlockSpec(memory_space=pl.ANY)
    index_scratch = [pltpu.SMEM((tm,), jnp.int32), pltpu.SMEM((tm,), jnp.int32)]
    tile_buffer = pltpu.VMEM((tm * SUBLANES, LANES), F32)
    sems = [pltpu.SemaphoreType.DMA((2,)), pltpu.SemaphoreType.DMA((2,))]
    slot_tiles = jax.ShapeDtypeStruct((n_slots * SUBLANES, LANES), F32)
    xs = pl.pallas_call(
        functools.partial(_moe_dispatch_kernel, tm=tm),
        grid=(t_total // tm,),
        in_specs=[any_spec, any_spec, pl.BlockSpec((tm, d), lambda i: (i, 0)), any_spec],
        out_specs=any_spec,
        out_shape=slot_tiles,
        scratch_shapes=index_scratch + [tile_buffer] + sems,
        input_output_aliases={3: 0},
        compiler_params=_cparams(1),
        name="moe_dispatch",
    )(pos_a, pos_b, h2d, jnp.zeros(slot_tiles.shape, F32))
    w_gu = w_gu.astype(BF16)
    row = lambda i, j, te, used: (i, 0)
    y = pl.pallas_call(
        _moe_group_kernel,
        grid_spec=pltpu.PrefetchScalarGridSpec(
            num_scalar_prefetch=2,
            grid=(n_slots // tm, n_chunks),
            in_specs=[
                pl.BlockSpec((tm * SUBLANES, LANES), row),
                pl.BlockSpec((1, d), lambda i, j, te, used: (0, 0)),
                pl.BlockSpec((1, d, tf), lambda i, j, te, used: (te[i], 0, j)),
                pl.BlockSpec((1, d, tf), lambda i, j, te, used: (te[i], 0, j + n_chunks)),
                pl.BlockSpec((1, tf, d), lambda i, j, te, used: (te[i], j, 0)),
            ],
            out_specs=pl.BlockSpec((tm * SUBLANES, LANES), row),
            scratch_shapes=[pltpu.VMEM((tm, d), BF16), pltpu.VMEM((tm, d), F32)],
        ),
        out_shape=slot_tiles,
        compiler_params=_cparams(2),
        name="moe_group_ffn",
    )(tile_expert, tile_used, xs, g_norm.reshape(1, d), w_gu, w_gu, w_down.astype(BF16))
    return pl.pallas_call(
        functools.partial(_moe_combine_kernel, tm=tm),
        grid=(t_total // tm,),
        in_specs=[any_spec, any_spec, any_spec, pl.BlockSpec((tm, d), lambda i: (i, 0)),
                  pl.BlockSpec((tm, LANES), lambda i: (i, 0))],
        out_specs=pl.BlockSpec((tm, d), lambda i: (i, 0)),
        out_shape=jax.ShapeDtypeStruct((t_total, d), F32),
        scratch_shapes=index_scratch + [tile_buffer, tile_buffer] + sems,
        compiler_params=_cparams(1),
        name="moe_combine",
    )(pos_a, pos_b, y, h2d, sel)


B_HEADS = 16
B_KV_HEADS = 4
B_GROUP = 4
B_HEAD_DIM = 64
CMP_STRIDE = 16
CMP_BLOCK = 32
SEL_BLOCK = 32
N_SELECT = 8
WINDOW = 512
SEL_BIG = 1e9
TINY = 1e-30
N_KV_KINDS = 6

LOG2E = 1.4426950408889634
POS_RADIX = 64
N_SLOPE_PARTS = 3


def _alibi_key_part(pos, idx):
    hi = (pos // POS_RADIX * POS_RADIX).astype(F32)
    lo = (pos % POS_RADIX).astype(F32)
    return jnp.where(idx < N_SLOPE_PARTS, hi, jnp.where(idx < 2 * N_SLOPE_PARTS, lo, 0.0))


def _alibi_query_part(slopes):
    parts, rest = [], slopes.astype(F32)
    for _ in range(N_SLOPE_PARTS):
        p = rest.astype(BF16)
        parts.append(p)
        rest = rest - p.astype(F32)
    return jnp.stack(parts + parts, axis=-1)


def _kv_proj_kernel(x_ref, gn_ref, wk_ref, kgain_ref, wvt_ref, wc_ref,
                    ks_ref, kw_ref, vst_ref, vwt_ref, kc_ref, vc_ref):
    g = B_KV_HEADS
    hn = _rms(x_ref[...], gn_ref[...]).astype(BF16)
    k = _dot(hn, wk_ref[...])
    tm = k.shape[0]
    pos = pl.program_id(1) * tm + lax.broadcasted_iota(jnp.int32, (tm, LANES), 0)
    lane = lax.broadcasted_iota(jnp.int32, (tm, LANES), 1)
    pos_lanes = jnp.where(lane >= B_HEAD_DIM, _alibi_key_part(pos, lane - B_HEAD_DIM), 0.0)
    for j in range(2 * g):
        kj = k[:, j * LANES:(j + 1) * LANES]
        ms = jnp.sum(kj * kj, axis=-1, keepdims=True) * (1.0 / B_HEAD_DIM)
        kj = (kj * lax.rsqrt(ms + NORM_EPS) * kgain_ref[:, j * LANES:(j + 1) * LANES] + pos_lanes).astype(BF16)
        if j < g:
            ks_ref[0, j] = kj
        else:
            kw_ref[0, j - g] = kj
    vt = _dot_nt(wvt_ref[...], hn)
    row = lax.broadcasted_iota(jnp.int32, vt.shape, 0)
    vt = jnp.where(row % LANES >= B_HEAD_DIM, 1.0, vt).astype(BF16)
    for j in range(g):
        vst_ref[0, j] = vt[j * LANES:(j + 1) * LANES]
        vwt_ref[0, j] = vt[(g + j) * LANES:(g + j + 1) * LANES]
    c = _dot(hn, wc_ref[...])
    half = g * B_HEAD_DIM
    kc_ref[...] = c[:, :half].astype(BF16)
    vc_ref[...] = c[:, half:].astype(BF16)


def nsa_kv_proj(h2d, bsz, seq, kv_norm, kv_w, k_norm, *, tm=512):
    t_total, d = h2d.shape
    g, dh = B_KV_HEADS, B_HEAD_DIM
    n_sblk = seq // tm
    w = kv_w.reshape(d, N_KV_KINDS, g, dh)
    def head_slabs(kinds):
        cols = jnp.stack([w[:, kind] for kind in kinds], axis=1)
        return jnp.pad(cols, ((0, 0), (0, 0), (0, 0), (0, LANES - dh))).reshape(d, len(kinds) * g * LANES)

    wk = head_slabs((2, 4)).astype(BF16)
    wvt = head_slabs((3, 5)).T.astype(BF16)
    kgain = jnp.pad(jnp.stack([jnp.tile(k_norm[1], (g, 1)), jnp.tile(k_norm[2], (g, 1))]),
                    ((0, 0), (0, 0), (0, LANES - dh))).reshape(1, 2 * g * LANES)
    wc = jnp.concatenate([w[:, 0].reshape(d, g * dh), w[:, 1].reshape(d, g * dh)], axis=1).astype(BF16)
    const2 = lambda b, s: (0, 0)
    row = lambda b, s: (b * n_sblk + s, 0)
    k_spec = pl.BlockSpec((1, g, tm, LANES), lambda b, s: (b, 0, s, 0))
    vt_spec = pl.BlockSpec((1, g, LANES, tm), lambda b, s: (b, 0, 0, s))
    k_shape = jax.ShapeDtypeStruct((bsz, g, seq, LANES), BF16)
    vt_shape = jax.ShapeDtypeStruct((bsz, g, LANES, seq), BF16)
    c_shape = jax.ShapeDtypeStruct((t_total, g * dh), BF16)
    return pl.pallas_call(
        _kv_proj_kernel,
        grid=(bsz, n_sblk),
        in_specs=[
            pl.BlockSpec((tm, d), row),
            _resident((1, d), const2),
            _resident((d, 2 * g * LANES), const2),
            _resident((1, 2 * g * LANES), const2),
            _resident((2 * g * LANES, d), const2),
            _resident((d, 2 * g * dh), const2),
        ],
        out_specs=[k_spec, k_spec, vt_spec, vt_spec,
                   pl.BlockSpec((tm, g * dh), row), pl.BlockSpec((tm, g * dh), row)],
        out_shape=[k_shape, k_shape, vt_shape, vt_shape, c_shape, c_shape],
        compiler_params=_cparams(2),
        name="nsa_kv_proj",
    )(h2d, kv_norm.reshape(1, d), wk, kgain, wvt, wc)


def _compress_kernel(ak_ref, av_ref, posk_ref, posv_ref, w1k_ref, w1v_ref, w2k_ref, w2vt_ref, kgain_ref,
                     kca_ref, vct_ref, *, n_half):
    half_w = ak_ref.shape[1]

    def hidden(a_ref, pos_ref, w1_ref):
        a = a_ref[...]
        top = _dot(a, w1_ref[:half_w, :])
        bot = _dot(a, w1_ref[half_w:, :])
        bias = _dot(pos_ref[...], w1_ref[...])[0:1, :]
        pre = top + pltpu.roll(bot, shift=a.shape[0] - 1, axis=0) + bias
        return (pre * _sigmoid(pre)).astype(BF16)

    act_k = hidden(ak_ref, posk_ref, w1k_ref)
    kc = _dot(act_k, w2k_ref[...])
    ms = jnp.sum(kc * kc, axis=-1, keepdims=True) * (1.0 / B_HEAD_DIM)
    kc = kc * lax.rsqrt(ms + NORM_EPS) * kgain_ref[...]
    blk = lax.broadcasted_iota(jnp.int32, kc.shape, 0) % n_half
    lane = lax.broadcasted_iota(jnp.int32, kc.shape, 1)
    c_end = blk * CMP_STRIDE + (CMP_BLOCK - 1)
    kca_ref[...] = (kc + jnp.where(lane >= B_HEAD_DIM, _alibi_key_part(c_end, lane - B_HEAD_DIM), 0.0)).astype(BF16)
    act_v = hidden(av_ref, posv_ref, w1v_ref)
    vct = _dot_nt(w2vt_ref[...], act_v).astype(BF16)
    for j in range(vct.shape[1] // n_half):
        vct_ref[j] = vct[:, j * n_half:(j + 1) * n_half]


def nsa_compress(kc_raw, vc_raw, bsz, seq, pos_k, w1_k, w2_k, pos_v, w1_v, w2_v, k_gain, *, groups_per_step=4):
    g, dh = B_KV_HEADS, B_HEAD_DIM
    n_half = seq // CMP_STRIDE
    feat = CMP_STRIDE * dh

    def to_half_blocks(a):
        a = a.reshape(bsz, n_half, CMP_STRIDE, g, dh).transpose(0, 3, 1, 2, 4)
        return a.reshape(bsz * g * n_half, feat)

    def pos_rows(pos):
        return jnp.zeros((8, 2 * feat), F32).at[0].set(pos.reshape(-1)).astype(BF16)

    hid = w1_k.shape[1]
    pad_lanes = lambda a: jnp.pad(a, ((0, 0), (0, LANES - dh)))
    rows = groups_per_step * n_half
    const2 = lambda i: (0, 0)
    return pl.pallas_call(
        functools.partial(_compress_kernel, n_half=n_half),
        grid=(bsz * g // groups_per_step,),
        in_specs=[
            pl.BlockSpec((rows, feat), lambda i: (i, 0)),
            pl.BlockSpec((rows, feat), lambda i: (i, 0)),
            _resident((8, 2 * feat), const2),
            _resident((8, 2 * feat), const2),
            _resident((2 * feat, hid), const2),
            _resident((2 * feat, hid), const2),
            _resident((hid, LANES), const2),
            _resident((LANES, hid), const2),
            _resident((1, LANES), const2),
        ],
        out_specs=[pl.BlockSpec((rows, LANES), lambda i: (i, 0)),
                   pl.BlockSpec((groups_per_step, LANES, n_half), lambda i: (i, 0, 0))],
        out_shape=[jax.ShapeDtypeStruct((bsz * g * n_half, LANES), BF16),
                   jax.ShapeDtypeStruct((bsz * g, LANES, n_half), BF16)],
        compiler_params=_cparams(1),
        name="nsa_compress",
    )(to_half_blocks(kc_raw), to_half_blocks(vc_raw), pos_rows(pos_k), pos_rows(pos_v),
      w1_k.astype(BF16), w1_v.astype(BF16), pad_lanes(w2_k).astype(BF16), pad_lanes(w2_v).T.astype(BF16),
      pad_lanes(k_gain.reshape(1, dh)))


def _q_proj_kernel(x_ref, gn_ref, w_ref, qgain_ref, qconst_ref, q_ref, gate_ref):
    d_q = q_ref.shape[1]
    xn = _rms(x_ref[...], gn_ref[...]).astype(BF16)
    proj = _dot(xn, w_ref[...])
    ones = jnp.ones((LANES, LANES), BF16)
    heads = [slice(h * LANES, (h + 1) * LANES) for h in range(d_q // LANES)]
    splits = [_split_bf16(proj[:, lanes] * proj[:, lanes]) for lanes in heads]
    sums = [_dot(hi, ones) + _dot(lo, ones) for hi, lo in splits]
    for lanes, ss in zip(heads, sums):
        q_ref[:, lanes] = (proj[:, lanes] * lax.rsqrt(ss * (1.0 / B_HEAD_DIM) + NORM_EPS) * qgain_ref[:, lanes]
                           + qconst_ref[:, lanes]).astype(BF16)
    gate_ref[...] = _sigmoid(proj[:, d_q:])


def nsa_q_proj(h2d, g_norm, w_in, q_norm, *, tm=512):
    t_total, d = h2d.shape
    g, hpg, dh = B_KV_HEADS, B_GROUP, B_HEAD_DIM
    d_q = B_HEADS * LANES
    wq = jnp.pad(w_in[:, :B_HEADS * dh].reshape(d, B_HEADS, dh), ((0, 0), (0, 0), (0, LANES - dh))).reshape(d, d_q)
    wg = w_in[:, B_HEADS * dh:].reshape(d, g, hpg * 3)
    wg = jnp.pad(wg, ((0, 0), (0, 0), (0, LANES - hpg * 3))).reshape(d, g * LANES)
    w = jnp.concatenate([wq, wg], axis=1).astype(BF16)
    qgain = jnp.tile(jnp.pad(q_norm * (dh ** -0.5 * LOG2E), (0, LANES - dh)), B_HEADS).reshape(1, d_q)
    slopes = np.array([2.0 ** (-8.0 * (h + 1) / B_HEADS) for h in range(B_HEADS)], dtype=np.float64) * LOG2E
    feats = _alibi_query_part(jnp.asarray(slopes, F32)).astype(F32)
    qconst = jnp.pad(feats, ((0, 0), (dh, LANES - dh - feats.shape[1]))).reshape(1, d_q)
    const2 = lambda i: (0, 0)
    return pl.pallas_call(
        _q_proj_kernel,
        grid=(t_total // tm,),
        in_specs=[
            pl.BlockSpec((tm, d), lambda i: (i, 0)),
            _resident((1, d), const2),
            _resident((d, d_q + g * LANES), const2),
            _resident((1, d_q), const2),
            _resident((1, d_q), const2),
        ],
        out_specs=[pl.BlockSpec((tm, d_q), lambda i: (i, 0)), pl.BlockSpec((tm, g * LANES), lambda i: (i, 0))],
        out_shape=[jax.ShapeDtypeStruct((t_total, d_q), BF16), jax.ShapeDtypeStruct((t_total, g * LANES), F32)],
        compiler_params=_cparams(1),
        name="nsa_q_proj",
    )(h2d, g_norm.reshape(1, d), w, qgain, qconst)


def _nsa_attn_kernel(q_ref, gate_ref, kca_ref, vct_ref, ks_ref, vst_ref, kw_ref, vwt_ref,
                     ovlt_ref, eselt_ref, o_ref, live_smem, m_scr, acc_scr, *, tq, tk):
    hpg, dh = B_GROUP, B_HEAD_DIM
    t0 = pl.program_id(2) * tq
    rows = hpg * tq
    qb = q_ref[...]
    q_st = jnp.concatenate([qb[:, i * LANES:(i + 1) * LANES] for i in range(hpg)], axis=0)

    n_half = kca_ref.shape[0]
    c_end = lax.broadcasted_iota(jnp.int32, (n_half, 1), 0) * CMP_STRIDE + (CMP_BLOCK - 1)
    t_lane = t0 + lax.broadcasted_iota(jnp.int32, (1, rows), 1) % tq
    seen = t_lane >= c_end
    s = jnp.where(seen, _dot_nt(kca_ref[...], q_st), NEG)
    m = jnp.max(s, axis=0, keepdims=True)
    e = jnp.where(seen, jnp.exp2(s - m), 0.0)
    p = e / jnp.maximum(jnp.sum(e, axis=0, keepdims=True), TINY)
    o_cmp = _dot(vct_ref[0], p.astype(BF16))[:dh]

    p_grp = p[:, 0:tq]
    for i in range(1, hpg):
        p_grp = p_grp + p[:, i * tq:(i + 1) * tq]
    p_hi, p_lo = _split_bf16(p_grp)
    n_blk = LANES // 2
    imp = (_dot(ovlt_ref[...], p_hi) + _dot(ovlt_ref[...], p_lo))[:n_blk]
    blk = lax.broadcasted_iota(jnp.int32, (n_blk, tq), 0)
    t_l = t0 + lax.broadcasted_iota(jnp.int32, (1, tq), 1)
    valid = blk * SEL_BLOCK <= t_l
    cur = t_l // SEL_BLOCK
    forced = valid & ((blk == 0) | (blk == cur) | (blk == cur - 1))
    work = jnp.where(forced, SEL_BIG, jnp.where(valid, imp, -SEL_BIG))
    blk_f = blk.astype(F32)
    chosen = jnp.zeros((n_blk, tq), F32)
    for _ in range(N_SELECT):
        top = jnp.max(work, axis=0, keepdims=True)
        first = jnp.min(jnp.where(work == top, blk_f, float(LANES)), axis=0, keepdims=True)
        hit = blk_f == first
        chosen = jnp.where(hit & (top > -0.5 * SEL_BIG), 1.0, chosen)
        work = jnp.where(hit, -3e38, work)
    blocks_per_tile = tk // SEL_BLOCK
    any_q = jnp.max(chosen, axis=1, keepdims=True)
    for kt in range(n_blk // blocks_per_tile):
        tile_any = jnp.max(any_q[kt * blocks_per_tile:(kt + 1) * blocks_per_tile])
        live_smem[kt] = (tile_any > 0.5).astype(jnp.int32)
    chosen = jnp.concatenate([chosen, jnp.zeros((LANES - n_blk, tq), F32)], axis=0).astype(BF16)

    def normalised(acc):
        return acc[:dh] / acc[dh:2 * dh]

    span = WINDOW + tq
    w0 = pl.multiple_of(jnp.maximum(t0 - WINDOW, 0), LANES)
    dist = t_l - (w0 + lax.broadcasted_iota(jnp.int32, (span, 1), 0))
    pen = jnp.where((dist >= 0) & (dist < WINDOW), 0.0, NEG)
    s = _dot_nt(kw_ref[0, 0, pl.ds(w0, span), :], q_st) + jnp.concatenate([pen] * hpg, axis=1)
    m = jnp.max(s, axis=0, keepdims=True)
    o_win = normalised(_dot(vwt_ref[0, 0, :, pl.ds(w0, span)], jnp.exp2(s - m).astype(BF16)))

    m_scr[...] = jnp.full(m_scr.shape, NEG, F32)
    acc_scr[...] = jnp.zeros(acc_scr.shape, F32)

    def sel_step(kt, carry):
        @pl.when(live_smem[kt] != 0)
        def _():
            k0 = pl.multiple_of(kt * tk, tk)
            pos = k0 + lax.broadcasted_iota(jnp.int32, (tk, 1), 0)
            picked = _dot(eselt_ref[pl.ds(k0, tk), :], chosen)
            pen = jnp.where((picked > 0.5) & (pos <= t_l), 0.0, NEG)
            sc = _dot_nt(ks_ref[0, 0, pl.ds(k0, tk), :], q_st) + jnp.concatenate([pen] * hpg, axis=1)
            m_run = m_scr[...]
            m_new = jnp.maximum(m_run, jnp.broadcast_to(jnp.max(sc, axis=0, keepdims=True), m_run.shape))
            pr = jnp.exp2(sc - m_new[0:1]).astype(BF16)
            acc_scr[...] = (jnp.exp2(m_run - m_new)[0:1] * acc_scr[...]
                            + _dot(vst_ref[0, 0, :, pl.ds(k0, tk)], pr))
            m_scr[...] = m_new
        return carry

    lax.fori_loop(0, (t0 + tq + tk - 1) // tk, sel_step, 0)
    o_sel = normalised(acc_scr[...])

    gate_t = gate_ref[...].T
    outs = []
    for i in range(hpg):
        c = slice(i * tq, (i + 1) * tq)
        outs.append(gate_t[3 * i:3 * i + 1] * o_cmp[:, c] + gate_t[3 * i + 1:3 * i + 2] * o_sel[:, c]
                    + gate_t[3 * i + 2:3 * i + 3] * o_win[:, c])
    o_ref[...] = jnp.concatenate(outs, axis=0).T.astype(BF16)


def nsa_attention(q, gates, kca, vct, ks, vst, kw, vwt, bsz, seq, *, tq=256, tk=512):
    g, hpg, dh = B_KV_HEADS, B_GROUP, B_HEAD_DIM
    n_q = seq // tq
    n_half = seq // CMP_STRIDE
    n_cmp = n_half - 1
    n_sel = seq // SEL_BLOCK
    assert n_sel <= LANES // 2 and tq % LANES == 0
    c_start = np.arange(n_half) * CMP_STRIDE
    j_sel = np.arange(LANES)
    ovl = ((c_start[:, None] < (j_sel[None, :] + 1) * SEL_BLOCK)
           & (c_start[:, None] + CMP_BLOCK > j_sel[None, :] * SEL_BLOCK)
           & (np.arange(n_half)[:, None] < n_cmp) & (j_sel[None, :] < n_sel))
    esel = (np.arange(seq)[None, :] // SEL_BLOCK) == j_sel[:, None]
    row = lambda b, j, i: (b * n_q + i, j)
    per_bg = lambda b, j, i: (b, j, 0, 0)
    return pl.pallas_call(
        functools.partial(_nsa_attn_kernel, tq=tq, tk=tk),
        grid=(bsz, g, n_q),
        in_specs=[
            pl.BlockSpec((tq, hpg * LANES), row),
            pl.BlockSpec((tq, LANES), row),
            pl.BlockSpec((n_half, LANES), lambda b, j, i: (b * g + j, 0)),
            pl.BlockSpec((1, LANES, n_half), lambda b, j, i: (b * g + j, 0, 0)),
            pl.BlockSpec((1, 1, seq, LANES), per_bg),
            pl.BlockSpec((1, 1, LANES, seq), per_bg),
            pl.BlockSpec((1, 1, seq, LANES), per_bg),
            pl.BlockSpec((1, 1, LANES, seq), per_bg),
            _resident((LANES, n_half), lambda b, j, i: (0, 0)),
            _resident((seq, LANES), lambda b, j, i: (0, 0)),
        ],
        out_specs=pl.BlockSpec((tq, hpg * dh), row),
        out_shape=jax.ShapeDtypeStruct((bsz * seq, B_HEADS * dh), BF16),
        scratch_shapes=[pltpu.SMEM((seq // tk,), jnp.int32), pltpu.VMEM((SUBLANES, hpg * tq), F32),
                        pltpu.VMEM((LANES, hpg * tq), F32)],
        compiler_params=_cparams(3),
        name="nsa_attention",
    )(q, gates, kca, vct, ks, vst, kw, vwt, jnp.asarray(ovl.T, BF16), jnp.asarray(esel.T, BF16))


def _out_proj_kernel(h_ref, o_ref, w_ref, out_ref):
    out_ref[...] = h_ref[...] + _dot(o_ref[...], w_ref[...])


def out_proj_residual(h2d, o, w_out, *, tm=512):
    t_total, d = h2d.shape
    k = o.shape[1]
    return pl.pallas_call(
        _out_proj_kernel,
        grid=(t_total // tm,),
        in_specs=[pl.BlockSpec((tm, d), lambda i: (i, 0)), pl.BlockSpec((tm, k), lambda i: (i, 0)),
                  _resident((k, d), lambda i: (0, 0))],
        out_specs=pl.BlockSpec((tm, d), lambda i: (i, 0)),
        out_shape=jax.ShapeDtypeStruct((t_total, d), F32),
        compiler_params=_cparams(1),
        name="out_proj_residual",
    )(h2d, o, w_out.astype(BF16))


def nsa_shared_kv(h2d, bsz, seq, kv_norm, kv_w, cmp_pos_k, cmp_w1_k, cmp_w2_k, cmp_pos_v, cmp_w1_v, cmp_w2_v, k_norm):
    ks, kw, vst, vwt, kc_raw, vc_raw = nsa_kv_proj(h2d, bsz, seq, kv_norm, kv_w, k_norm)
    kca, vct = nsa_compress(kc_raw, vc_raw, bsz, seq, cmp_pos_k, cmp_w1_k, cmp_w2_k,
                            cmp_pos_v, cmp_w1_v, cmp_w2_v, k_norm[0])
    return kca, vct, ks, vst, kw, vwt


def nsa_mixer(h2d, bsz, seq, g_norm, w_in, w_out, q_norm, kv):
    q, gates = nsa_q_proj(h2d, g_norm, w_in, q_norm)
    o = nsa_attention(q, gates, *kv, bsz, seq)
    return out_proj_residual(h2d, o, w_out)


def kernel(x, norm_mix, norm_ffn, hgrn_w_in, hgrn_w_out, hgrn_g_norm, hgrn_lb_logits, kv_norm, kv_w, cmp_pos_k, cmp_w1_k, cmp_w2_k, cmp_pos_v, cmp_w1_v, cmp_w2_v, k_norm, nsa_w_in, nsa_w_out, nsa_q_norm, ffn_w_gu, ffn_w_down, moe_router, moe_w_gu, moe_w_down):
    bsz, seq, d = x.shape
    lb = jnp.cumsum(jax.nn.softmax(hgrn_lb_logits.astype(F32), axis=0), axis=0)
    lb = lb - lb[0:1]
    n_a = hgrn_w_in.shape[0]
    depth = norm_mix.shape[0]
    h = x.reshape(bsz * seq, d)
    kv = None
    for layer in range(depth):
        if layer < n_a:
            h = hgrn_mixer(h, seq, norm_mix[layer], hgrn_w_in[layer], hgrn_w_out[layer], hgrn_g_norm[layer],
                           lb[layer])
        else:
            if kv is None:
                kv = nsa_shared_kv(h, bsz, seq, kv_norm, kv_w, cmp_pos_k, cmp_w1_k, cmp_w2_k,
                                   cmp_pos_v, cmp_w1_v, cmp_w2_v, k_norm)
            j = layer - n_a
            h = nsa_mixer(h, bsz, seq, norm_mix[layer], nsa_w_in[j], nsa_w_out[j], nsa_q_norm[j], kv)
        if layer % 2 == 0:
            h = ffn_swiglu(h, norm_ffn[layer], ffn_w_gu[layer // 2], ffn_w_down[layer // 2])
        else:
            h = moe_swiglu_top2(h, norm_ffn[layer], moe_router[layer // 2], moe_w_gu[layer // 2],
                                moe_w_down[layer // 2])
    return h.reshape(bsz, seq, d)
```

```python
import functools

import numpy as np
import jax
import jax.numpy as jnp
from jax import lax
from jax.experimental import pallas as pl
from jax.experimental.pallas import tpu as pltpu

F32 = jnp.float32
BF16 = jnp.bfloat16

NORM_EPS = 1e-6
VMEM_LIMIT_BYTES = 56 * 1024 * 1024

A_HEADS = 8
A_HEAD_DIM = 128
SCAN_ROWS = 128
SUBLANES = 8


def _cparams(n_axes):
    return pltpu.CompilerParams(dimension_semantics=("arbitrary",) * n_axes,
                                vmem_limit_bytes=VMEM_LIMIT_BYTES)


def _dot(a, b):
    return jnp.dot(a, b, preferred_element_type=F32)


def _dot_nt(a, b):
    return lax.dot_general(a, b, (((1,), (1,)), ((), ())), preferred_element_type=F32)


def _dot_tn(a, b):
    return lax.dot_general(a, b, (((0,), (0,)), ((), ())), preferred_element_type=F32)


def _sigmoid(x):
    return 0.5 * jnp.tanh(0.5 * x) + 0.5


def _rms(x, g):
    return x * lax.rsqrt(jnp.mean(x * x, axis=-1, keepdims=True) + NORM_EPS) * g


def _split_bf16(x):
    hi = x.astype(BF16)
    lo = (x - hi.astype(F32)).astype(BF16)
    return hi, lo


def _hgrn_constants():
    n = SCAN_ROWS
    t = np.arange(n)
    r = np.arange(n)[None, :]
    sums, masks = [], []
    m = n // 2
    while m >= 1:
        grp = t // (2 * m)
        mid = grp * 2 * m + m
        upper = (t % (2 * m)) >= m
        if m < SUBLANES:
            sums.append(np.where(upper[:, None], (r >= mid[:, None]) & (r <= t[:, None]),
                                 (r > t[:, None]) & (r < mid[:, None])))
        masks.append((grp[:, None] == grp[None, :]) & upper[:, None] & (~upper[None, :]))
        m //= 2
    masks.append(np.eye(n, dtype=bool))
    sums.append(r <= t[:, None])
    w = np.concatenate(sums, 0).astype(np.float32)
    return np.concatenate([w, w], 1), np.stack(masks).astype(np.float32)


def _hgrn_kernel(x_ref, gn_ref, win_ref, lb_ref, ghn_ref, wout_ref, wsum_ref, msk_ref,
                 out_ref, xn_scr, st_scr, og_scr, *, tm):
    sblk = pl.program_id(1)
    h = pl.program_id(2)
    n = SCAN_ROWS
    n_lvl = msk_ref.shape[0] - 1

    @pl.when(h == 0)
    def _():
        xn_scr[...] = _rms(x_ref[...], gn_ref[...]).astype(BF16)

    @pl.when(sblk == 0)
    def _():
        st_scr[h] = jnp.zeros((A_HEAD_DIM, A_HEAD_DIM), F32)

    proj = _dot(xn_scr[...], win_ref[...])
    q = proj[:, 0:128]
    f = proj[:, 128:256]
    v = proj[:, 256:384]
    g = proj[:, 384:512]
    lb = lb_ref[0:1, :]
    log_lb = lb_ref[1:2, :]
    log_1m_lb = lb_ref[2:3, :]
    qs = q * _sigmoid(q)
    a = jnp.exp(-jnp.abs(f))
    inv = 1.0 / (1.0 + a)
    log_sig = jnp.minimum(f, 0.0) - jnp.log1p(a)
    b = log_1m_lb + log_sig
    logf = jnp.maximum(log_lb, b) + jnp.log1p(jnp.exp(-jnp.abs(log_lb - b)))
    key = (1.0 - lb) * jnp.where(f >= 0.0, a * inv, inv)

    wide = [n >> (l + 1) for l in range(n_lvl) if n >> (l + 1) >= SUBLANES]
    row_id = lax.broadcasted_iota(jnp.int32, (n, A_HEAD_DIM), 0)
    in_upper_half = [(row_id % (2 * m)) >= m for m in wide]

    blocks = [slice(r * n, (r + 1) * n) for r in range(tm // n)]
    n_narrow = n_lvl - len(wide)
    sums = []
    for rows in blocks:
        hi, lo = _split_bf16(logf[rows])
        sums.append(_dot(wsum_ref[...], jnp.concatenate([hi, lo], axis=0)))
    e_lvl, e_prefix, e_suffix = [], [], []
    for sm in sums:
        cum = sm[n_narrow * n:]
        e_blk = []
        for m, upper in zip(wide, in_upper_half):
            bound = jnp.concatenate([jnp.broadcast_to(cum[i + m - 1:i + m], (2 * m, A_HEAD_DIM))
                                     for i in range(0, n, 2 * m)], axis=0)
            e_blk.append(jnp.exp(jnp.where(upper, cum - bound, bound - cum)))
        e_sm = jnp.exp(sm)
        e_lvl.append(e_blk + [e_sm[i * n:(i + 1) * n] for i in range(n_narrow)])
        e_prefix.append(e_sm[n_narrow * n:])
        e_suffix.append(jnp.exp(cum[n - 1:n] - cum))
    intra, inter_q, kv = [], [], []
    for rows, e_blk, e_pre, e_suf in zip(blocks, e_lvl, e_prefix, e_suffix):
        qb = qs[rows]
        kb = key[rows]
        vb = v[rows].astype(BF16)
        s = msk_ref[n_lvl] * _dot_nt(qb.astype(BF16), kb.astype(BF16))
        for l in range(n_lvl):
            s = s + msk_ref[l] * _dot_nt((qb * e_blk[l]).astype(BF16), (kb * e_blk[l]).astype(BF16))
        intra.append(_dot(s.astype(BF16), vb))
        inter_q.append((qb * e_pre).astype(BF16))
        kv.append(_dot_tn(vb, (kb * e_suf).astype(BF16)))
    st = st_scr[h]
    outs = []
    for o_intra, q_pre, e_pre, kv_blk in zip(intra, inter_q, e_prefix, kv):
        outs.append(o_intra + _dot_nt(q_pre, st.astype(BF16)))
        st = st * e_pre[n - 1:n, :] + kv_blk
    st_scr[h] = st
    o = jnp.concatenate(outs, axis=0)

    og_scr[:, pl.ds(pl.multiple_of(h * A_HEAD_DIM, A_HEAD_DIM), A_HEAD_DIM)] = (
        _rms(o, ghn_ref[...]) * (g * _sigmoid(g))).astype(BF16)

    @pl.when(h == A_HEADS - 1)
    def _():
        out_ref[...] = x_ref[...] + _dot(og_scr[...], wout_ref[...])


def hgrn_mixer(h2d, seq, g_norm_in, w_in, w_out, g_head, lb, *, tm=1024):
    t_total, d = h2d.shape
    n_sblk = seq // tm
    hd = A_HEAD_DIM
    w_perm = w_in.reshape(d, 4, A_HEADS, hd).transpose(0, 2, 1, 3).reshape(d, 4 * A_HEADS * hd).astype(BF16)
    lb_rows = jnp.zeros((8, A_HEADS * hd), F32)
    lb_rows = lb_rows.at[0].set(lb).at[1].set(jnp.log(lb)).at[2].set(jnp.log1p(-lb))
    wsum, masks = _hgrn_constants()
    n_blk = wsum.shape[0]
    grid = (t_total // seq, n_sblk, A_HEADS)
    row = lambda b, s, h: (b * n_sblk + s, 0)
    const2 = lambda b, s, h: (0, 0)
    return pl.pallas_call(
        functools.partial(_hgrn_kernel, tm=tm),
        grid=grid,
        in_specs=[
            pl.BlockSpec((tm, d), row),
            pl.BlockSpec((1, d), const2),
            pl.BlockSpec((d, 4 * hd), lambda b, s, h: (0, h)),
            pl.BlockSpec((8, hd), lambda b, s, h: (0, h)),
            pl.BlockSpec((1, hd), const2),
            _resident((A_HEADS * hd, d), const2),
            _resident((n_blk, 2 * SCAN_ROWS), const2),
            _resident(masks.shape, lambda b, s, h: (0, 0, 0)),
        ],
        out_specs=pl.BlockSpec((tm, d), row),
        out_shape=jax.ShapeDtypeStruct((t_total, d), F32),
        scratch_shapes=[pltpu.VMEM((tm, d), BF16), pltpu.VMEM((A_HEADS, hd, hd), F32),
                        pltpu.VMEM((tm, A_HEADS * hd), BF16)],
        compiler_params=_cparams(3),
        name="hgrn_mixer",
    )(h2d, g_norm_in.reshape(1, d), w_perm, lb_rows, g_head.reshape(1, hd), w_out.astype(BF16),
      jnp.asarray(wsum, BF16), jnp.asarray(masks, F32))


def _resident(block_shape, index_map):
    return pl.BlockSpec(block_shape, index_map, pipeline_mode=pl.Buffered(1))


MXU_TILE = 256


def _swiglu(xn, wg_ref, wu_ref, wd_ref):
    acc = None
    for c in range(wg_ref.shape[1] // MXU_TILE):
        cols = slice(c * MXU_TILE, (c + 1) * MXU_TILE)
        gate = _dot(xn, wg_ref[:, cols])
        up = _dot(xn, wu_ref[:, cols])
        part = _dot((gate * _sigmoid(gate) * up).astype(BF16), wd_ref[cols, :])
        acc = part if acc is None else acc + part
    return acc


def _ffn_kernel(x_ref, gn_ref, wg_ref, wu_ref, wd_ref, out_ref):
    x = x_ref[...]
    xn = _rms(x, gn_ref[...]).astype(BF16)
    out_ref[...] = x + _swiglu(xn, wg_ref, wu_ref, wd_ref)


def ffn_swiglu(h2d, g_norm, w_gu, w_down, *, tm=512):
    t_total, d = h2d.shape
    ff = w_down.shape[0]
    w_gu = w_gu.astype(BF16)
    const2 = lambda i: (0, 0)
    return pl.pallas_call(
        _ffn_kernel,
        grid=(t_total // tm,),
        in_specs=[
            pl.BlockSpec((tm, d), lambda i: (i, 0)),
            _resident((1, d), const2),
            _resident((d, ff), const2),
            _resident((d, ff), lambda i: (0, 1)),
            _resident((ff, d), const2),
        ],
        out_specs=pl.BlockSpec((tm, d), lambda i: (i, 0)),
        out_shape=jax.ShapeDtypeStruct((t_total, d), F32),
        compiler_params=_cparams(1),
        name="ffn_swiglu",
    )(h2d, g_norm.reshape(1, d), w_gu, w_gu, w_down.astype(BF16))


N_EXPERTS = 8
LANES = 128
NEG = -1e30


def _router_kernel(x_ref, gn_ref, wh_ref, wl_ref, sel_ref):
    xn = _rms(x_ref[...], gn_ref[...])
    xh, xl = _split_bf16(xn)
    logits = _dot(xh, wh_ref[...]) + (_dot(xl, wh_ref[...]) + _dot(xh, wl_ref[...]))
    lane = lax.broadcasted_iota(jnp.int32, logits.shape, 1).astype(F32)
    logits = jnp.where(lane < N_EXPERTS, logits, NEG)
    m1 = jnp.max(logits, axis=-1, keepdims=True)
    i1 = jnp.min(jnp.where(logits == m1, lane, float(LANES)), axis=-1, keepdims=True)
    rest = jnp.where(lane == i1, NEG, logits)
    m2 = jnp.max(rest, axis=-1, keepdims=True)
    i2 = jnp.min(jnp.where(rest == m2, lane, float(LANES)), axis=-1, keepdims=True)
    e2 = jnp.exp(m2 - m1)
    denom = 1.0 + e2
    sel_ref[...] = jnp.where(lane == 0, i1, jnp.where(lane == 1, i2, jnp.where(
        lane == 2, 1.0 / denom, jnp.where(lane == 3, e2 / denom, 0.0))))


def moe_route(h2d, g_norm, w_router, *, tm=512):
    t_total, d = h2d.shape
    w_pad = jnp.zeros((d, LANES), F32).at[:, :N_EXPERTS].set(w_router)
    w_hi = w_pad.astype(BF16)
    w_lo = (w_pad - w_hi.astype(F32)).astype(BF16)
    const2 = lambda i: (0, 0)
    return pl.pallas_call(
        _router_kernel,
        grid=(t_total // tm,),
        in_specs=[
            pl.BlockSpec((tm, d), lambda i: (i, 0)),
            _resident((1, d), const2),
            _resident((d, LANES), const2),
            _resident((d, LANES), const2),
        ],
        out_specs=pl.BlockSpec((tm, LANES), lambda i: (i, 0)),
        out_shape=jax.ShapeDtypeStruct((t_total, LANES), F32),
        compiler_params=_cparams(1),
        name="moe_router",
    )(h2d, g_norm.reshape(1, d), w_hi, w_lo)


MOE_TILE = 1024
GATHER_UNROLL = 8


def _moe_plan(sel, tm):
    t_total = sel.shape[0]
    e = sel[:, :2].astype(jnp.int32)
    onehot = (e[:, :, None] == jnp.arange(N_EXPERTS, dtype=jnp.int32)[None, None, :]).astype(jnp.int32).sum(1)
    csum = jnp.cumsum(onehot, axis=0)
    rank = csum - onehot
    padded = (csum[-1] + tm - 1) // tm * tm
    ends = jnp.cumsum(padded)
    pos = (ends - padded)[e] + jnp.take_along_axis(rank, e, axis=1)
    n_slots = 2 * t_total + N_EXPERTS * tm
    tile_start = jnp.arange(n_slots // tm, dtype=jnp.int32) * tm
    tile_expert = jnp.minimum(jnp.searchsorted(ends, tile_start, side="right"), N_EXPERTS - 1).astype(jnp.int32)
    tile_used = (tile_start < ends[-1]).astype(jnp.int32)
    n_used = (ends[-1:] // tm).astype(jnp.int32)
    last_tile = jnp.maximum(ends - tm, 0).astype(jnp.int32)
    return n_slots, tile_expert, tile_used, n_used, last_tile, (padded > 0).astype(jnp.int32), pos[:, 0], pos[:, 1]


def _load_indices(idx_hbm, idx_smem, sem, tile, tm):
    return pltpu.make_async_copy(idx_hbm.at[pl.ds(tile * tm, tm)], idx_smem, sem)


def _moe_dispatch_kernel(last_tile_ref, has_rows_ref, n_used_ref, pa_hbm, pb_hbm, h_ref, xs_hbm, ia_smem, ib_smem,
                         zero_scr, row_sem, idx_sem, zero_sem, *, tm):
    i = pl.program_id(0)

    @pl.when(i == 0)
    def _():
        zero_scr[...] = jnp.zeros(zero_scr.shape, F32)
        n_tiles = xs_hbm.shape[0] // tm
        for e in range(N_EXPERTS):
            spare = n_used_ref[0] + e
            for start, wanted in ((last_tile_ref[e], has_rows_ref[e] != 0), (spare * tm, spare < n_tiles)):
                fill = pltpu.make_async_copy(zero_scr, xs_hbm.at[pl.ds(pl.multiple_of(start, tm), tm), :],
                                             zero_sem.at[e])

                @pl.when(wanted)
                def _():
                    fill.start()
                    fill.wait()

    ca = _load_indices(pa_hbm, ia_smem, idx_sem.at[0], i, tm)
    cb = _load_indices(pb_hbm, ib_smem, idx_sem.at[1], i, tm)
    ca.start()
    cb.start()
    ca.wait()
    cb.wait()

    def body(r, carry):
        src = h_ref.at[pl.ds(r, 1), :]
        pltpu.make_async_copy(src, xs_hbm.at[pl.ds(ia_smem[r], 1), :], row_sem.at[0]).start()
        pltpu.make_async_copy(src, xs_hbm.at[pl.ds(ib_smem[r], 1), :], row_sem.at[1]).start()
        return carry
    lax.fori_loop(0, tm, body, 0, unroll=GATHER_UNROLL)
    pltpu.make_async_copy(h_ref, xs_hbm.at[pl.ds(0, tm), :], row_sem.at[0]).wait()
    pltpu.make_async_copy(h_ref, xs_hbm.at[pl.ds(0, tm), :], row_sem.at[1]).wait()


def _moe_group_kernel(te_ref, used_ref, xs_ref, gn_ref, wg_ref, wu_ref, wd_ref, y_ref, xn_scr):
    i = pl.program_id(0)
    j = pl.program_id(1)

    @pl.when(used_ref[i] == 0)
    def _():
        y_ref[...] = jnp.zeros(y_ref.shape, F32)

    @pl.when(used_ref[i] != 0)
    def _():
        @pl.when(j == 0)
        def _():
            xn_scr[...] = _rms(xs_ref[...], gn_ref[...]).astype(BF16)

        contrib = _swiglu(xn_scr[...], wg_ref.at[0], wu_ref.at[0], wd_ref.at[0])

        @pl.when(j == 0)
        def _():
            y_ref[...] = contrib

        @pl.when(j > 0)
        def _():
            y_ref[...] += contrib


def _moe_combine_kernel(pa_hbm, pb_hbm, y_hbm, h_ref, sel_ref, out_ref, ia_smem, ib_smem, buf_a, buf_b,
                        row_sem, idx_sem, *, tm):
    i = pl.program_id(0)
    ca = _load_indices(pa_hbm, ia_smem, idx_sem.at[0], i, tm)
    cb = _load_indices(pb_hbm, ib_smem, idx_sem.at[1], i, tm)
    ca.start()
    cb.start()
    ca.wait()
    cb.wait()

    def body(r, carry):
        pltpu.make_async_copy(y_hbm.at[pl.ds(ia_smem[r], 1), :], buf_a.at[pl.ds(r, 1), :], row_sem.at[0]).start()
        pltpu.make_async_copy(y_hbm.at[pl.ds(ib_smem[r], 1), :], buf_b.at[pl.ds(r, 1), :], row_sem.at[1]).start()
        return carry
    lax.fori_loop(0, tm, body, 0, unroll=GATHER_UNROLL)
    pltpu.make_async_copy(buf_a, buf_a, row_sem.at[0]).wait()
    pltpu.make_async_copy(buf_b, buf_b, row_sem.at[1]).wait()
    sel = sel_ref[...]
    out_ref[...] = h_ref[...] + (sel[:, 2:3] * buf_a[...] + sel[:, 3:4] * buf_b[...])


def moe_swiglu_top2(h2d, g_norm, w_router, w_gu, w_down, *, n_chunks=2):
    t_total, d = h2d.shape
    tm = MOE_TILE
    ff = w_down.shape[1]
    tf = ff // n_chunks
    sel = moe_route(h2d, g_norm, w_router)
    n_slots, tile_expert, tile_used, n_used, last_tile, has_rows, pos_a, pos_b = _moe_plan(sel, tm)
    any_spec = pl.BlockSpec(memory_space=pl.ANY)
    index_scratch = [pltpu.SMEM((tm,), jnp.int32), pltpu.SMEM((tm,), jnp.int32)]
    sems = [pltpu.SemaphoreType.DMA((2,)), pltpu.SemaphoreType.DMA((2,))]
    xs = pl.pallas_call(
        functools.partial(_moe_dispatch_kernel, tm=tm),
        grid_spec=pltpu.PrefetchScalarGridSpec(
            num_scalar_prefetch=3,
            grid=(t_total // tm,),
            in_specs=[any_spec, any_spec, pl.BlockSpec((tm, d), lambda i, lt, hr, nu: (i, 0))],
            out_specs=any_spec,
            scratch_shapes=index_scratch + [pltpu.VMEM((tm, d), F32)] + sems
            + [pltpu.SemaphoreType.DMA((N_EXPERTS,))],
        ),
        out_shape=jax.ShapeDtypeStruct((n_slots, d), F32),
        compiler_params=_cparams(1),
        name="moe_dispatch",
    )(last_tile, has_rows, n_used, pos_a, pos_b, h2d)
    w_gu = w_gu.astype(BF16)
    row = lambda i, j, te, used: (i, 0)
    y = pl.pallas_call(
        _moe_group_kernel,
        grid_spec=pltpu.PrefetchScalarGridSpec(
            num_scalar_prefetch=2,
            grid=(n_slots // tm, n_chunks),
            in_specs=[
                pl.BlockSpec((tm, d), row),
                pl.BlockSpec((1, d), lambda i, j, te, used: (0, 0)),
                pl.BlockSpec((1, d, tf), lambda i, j, te, used: (te[i], 0, j)),
                pl.BlockSpec((1, d, tf), lambda i, j, te, used: (te[i], 0, j + n_chunks)),
                pl.BlockSpec((1, tf, d), lambda i, j, te, used: (te[i], j, 0)),
            ],
            out_specs=pl.BlockSpec((tm, d), row),
            scratch_shapes=[pltpu.VMEM((tm, d), BF16)],
        ),
        out_shape=jax.ShapeDtypeStruct((n_slots, d), F32),
        compiler_params=_cparams(2),
        name="moe_group_ffn",
    )(tile_expert, tile_used, xs, g_norm.reshape(1, d), w_gu, w_gu, w_down.astype(BF16))
    return pl.pallas_call(
        functools.partial(_moe_combine_kernel, tm=tm),
        grid=(t_total // tm,),
        in_specs=[any_spec, any_spec, any_spec, pl.BlockSpec((tm, d), lambda i: (i, 0)),
                  pl.BlockSpec((tm, LANES), lambda i: (i, 0))],
        out_specs=pl.BlockSpec((tm, d), lambda i: (i, 0)),
        out_shape=jax.ShapeDtypeStruct((t_total, d), F32),
        scratch_shapes=index_scratch + [pltpu.VMEM((tm, d), F32), pltpu.VMEM((tm, d), F32)] + sems,
        compiler_params=_cparams(1),
        name="moe_combine",
    )(pos_a, pos_b, y, h2d, sel)


B_HEADS = 16
B_KV_HEADS = 4
B_GROUP = 4
B_HEAD_DIM = 64
CMP_STRIDE = 16
CMP_BLOCK = 32
SEL_BLOCK = 32
N_SELECT = 8
WINDOW = 512
SEL_BIG = 1e9
TINY = 1e-30
N_KV_KINDS = 6

LOG2E = 1.4426950408889634
POS_RADIX = 64
N_SLOPE_PARTS = 3


def _alibi_key_part(pos, idx):
    hi = (pos // POS_RADIX * POS_RADIX).astype(F32)
    lo = (pos % POS_RADIX).astype(F32)
    return jnp.where(idx < N_SLOPE_PARTS, hi, jnp.where(idx < 2 * N_SLOPE_PARTS, lo, 0.0))


def _alibi_query_part(slopes):
    parts, rest = [], slopes.astype(F32)
    for _ in range(N_SLOPE_PARTS):
        p = rest.astype(BF16)
        parts.append(p)
        rest = rest - p.astype(F32)
    return jnp.stack(parts + parts, axis=-1)


def _kv_proj_kernel(x_ref, gn_ref, wk_ref, kgain_ref, wvt_ref, wc_ref,
                    ks_ref, kw_ref, vst_ref, vwt_ref, kc_ref, vc_ref):
    g = B_KV_HEADS
    hn = _rms(x_ref[...], gn_ref[...]).astype(BF16)
    k = _dot(hn, wk_ref[...])
    tm = k.shape[0]
    pos = pl.program_id(1) * tm + lax.broadcasted_iota(jnp.int32, (tm, LANES), 0)
    lane = lax.broadcasted_iota(jnp.int32, (tm, LANES), 1)
    pos_lanes = jnp.where(lane >= B_HEAD_DIM, _alibi_key_part(pos, lane - B_HEAD_DIM), 0.0)
    for j in range(2 * g):
        kj = k[:, j * LANES:(j + 1) * LANES]
        ms = jnp.sum(kj * kj, axis=-1, keepdims=True) * (1.0 / B_HEAD_DIM)
        kj = (kj * lax.rsqrt(ms + NORM_EPS) * kgain_ref[:, j * LANES:(j + 1) * LANES] + pos_lanes).astype(BF16)
        if j < g:
            ks_ref[0, j] = kj
        else:
            kw_ref[0, j - g] = kj
    vt = _dot_nt(wvt_ref[...], hn)
    row = lax.broadcasted_iota(jnp.int32, vt.shape, 0)
    vt = jnp.where(row % LANES >= B_HEAD_DIM, 1.0, vt).astype(BF16)
    for j in range(g):
        vst_ref[0, j] = vt[j * LANES:(j + 1) * LANES]
        vwt_ref[0, j] = vt[(g + j) * LANES:(g + j + 1) * LANES]
    c = _dot(hn, wc_ref[...])
    half = g * B_HEAD_DIM
    kc_ref[...] = c[:, :half].astype(BF16)
    vc_ref[...] = c[:, half:].astype(BF16)


def nsa_kv_proj(h2d, bsz, seq, kv_norm, kv_w, k_norm, *, tm=512):
    t_total, d = h2d.shape
    g, dh = B_KV_HEADS, B_HEAD_DIM
    n_sblk = seq // tm
    w = kv_w.reshape(d, N_KV_KINDS, g, dh)
    def head_slabs(kinds):
        cols = jnp.stack([w[:, kind] for kind in kinds], axis=1)
        return jnp.pad(cols, ((0, 0), (0, 0), (0, 0), (0, LANES - dh))).reshape(d, len(kinds) * g * LANES)

    wk = head_slabs((2, 4)).astype(BF16)
    wvt = head_slabs((3, 5)).T.astype(BF16)
    kgain = jnp.pad(jnp.stack([jnp.tile(k_norm[1], (g, 1)), jnp.tile(k_norm[2], (g, 1))]),
                    ((0, 0), (0, 0), (0, LANES - dh))).reshape(1, 2 * g * LANES)
    wc = jnp.concatenate([w[:, 0].reshape(d, g * dh), w[:, 1].reshape(d, g * dh)], axis=1).astype(BF16)
    const2 = lambda b, s: (0, 0)
    row = lambda b, s: (b * n_sblk + s, 0)
    k_spec = pl.BlockSpec((1, g, tm, LANES), lambda b, s: (b, 0, s, 0))
    vt_spec = pl.BlockSpec((1, g, LANES, tm), lambda b, s: (b, 0, 0, s))
    k_shape = jax.ShapeDtypeStruct((bsz, g, seq, LANES), BF16)
    vt_shape = jax.ShapeDtypeStruct((bsz, g, LANES, seq), BF16)
    c_shape = jax.ShapeDtypeStruct((t_total, g * dh), BF16)
    return pl.pallas_call(
        _kv_proj_kernel,
        grid=(bsz, n_sblk),
        in_specs=[
            pl.BlockSpec((tm, d), row),
            _resident((1, d), const2),
            _resident((d, 2 * g * LANES), const2),
            _resident((1, 2 * g * LANES), const2),
            _resident((2 * g * LANES, d), const2),
            _resident((d, 2 * g * dh), const2),
        ],
        out_specs=[k_spec, k_spec, vt_spec, vt_spec,
                   pl.BlockSpec((tm, g * dh), row), pl.BlockSpec((tm, g * dh), row)],
        out_shape=[k_shape, k_shape, vt_shape, vt_shape, c_shape, c_shape],
        compiler_params=_cparams(2),
        name="nsa_kv_proj",
    )(h2d, kv_norm.reshape(1, d), wk, kgain, wvt, wc)


def _compress_kernel(ak_ref, av_ref, posk_ref, posv_ref, w1k_ref, w1v_ref, w2k_ref, w2vt_ref, kgain_ref,
                     kca_ref, vct_ref, *, n_half):
    half_w = ak_ref.shape[1]

    def hidden(a_ref, pos_ref, w1_ref):
        a = a_ref[...]
        top = _dot(a, w1_ref[:half_w, :])
        bot = _dot(a, w1_ref[half_w:, :])
        bias = _dot(pos_ref[...], w1_ref[...])[0:1, :]
        pre = top + pltpu.roll(bot, shift=a.shape[0] - 1, axis=0) + bias
        return (pre * _sigmoid(pre)).astype(BF16)

    act_k = hidden(ak_ref, posk_ref, w1k_ref)
    kc = _dot(act_k, w2k_ref[...])
    ms = jnp.sum(kc * kc, axis=-1, keepdims=True) * (1.0 / B_HEAD_DIM)
    kc = kc * lax.rsqrt(ms + NORM_EPS) * kgain_ref[...]
    blk = lax.broadcasted_iota(jnp.int32, kc.shape, 0) % n_half
    lane = lax.broadcasted_iota(jnp.int32, kc.shape, 1)
    c_end = blk * CMP_STRIDE + (CMP_BLOCK - 1)
    kca_ref[...] = (kc + jnp.where(lane >= B_HEAD_DIM, _alibi_key_part(c_end, lane - B_HEAD_DIM), 0.0)).astype(BF16)
    act_v = hidden(av_ref, posv_ref, w1v_ref)
    vct = _dot_nt(w2vt_ref[...], act_v).astype(BF16)
    for j in range(vct.shape[1] // n_half):
        vct_ref[j] = vct[:, j * n_half:(j + 1) * n_half]


def nsa_compress(kc_raw, vc_raw, bsz, seq, pos_k, w1_k, w2_k, pos_v, w1_v, w2_v, k_gain, *, groups_per_step=4):
    g, dh = B_KV_HEADS, B_HEAD_DIM
    n_half = seq // CMP_STRIDE
    feat = CMP_STRIDE * dh

    def to_half_blocks(a):
        a = a.reshape(bsz, n_half, CMP_STRIDE, g, dh).transpose(0, 3, 1, 2, 4)
        return a.reshape(bsz * g * n_half, feat)

    def pos_rows(pos):
        return jnp.zeros((8, 2 * feat), F32).at[0].set(pos.reshape(-1)).astype(BF16)

    hid = w1_k.shape[1]
    pad_lanes = lambda a: jnp.pad(a, ((0, 0), (0, LANES - dh)))
    rows = groups_per_step * n_half
    const2 = lambda i: (0, 0)
    return pl.pallas_call(
        functools.partial(_compress_kernel, n_half=n_half),
        grid=(bsz * g // groups_per_step,),
        in_specs=[
            pl.BlockSpec((rows, feat), lambda i: (i, 0)),
            pl.BlockSpec((rows, feat), lambda i: (i, 0)),
            _resident((8, 2 * feat), const2),
            _resident((8, 2 * feat), const2),
            _resident((2 * feat, hid), const2),
            _resident((2 * feat, hid), const2),
            _resident((hid, LANES), const2),
            _resident((LANES, hid), const2),
            _resident((1, LANES), const2),
        ],
        out_specs=[pl.BlockSpec((rows, LANES), lambda i: (i, 0)),
                   pl.BlockSpec((groups_per_step, LANES, n_half), lambda i: (i, 0, 0))],
        out_shape=[jax.ShapeDtypeStruct((bsz * g * n_half, LANES), BF16),
                   jax.ShapeDtypeStruct((bsz * g, LANES, n_half), BF16)],
        compiler_params=_cparams(1),
        name="nsa_compress",
    )(to_half_blocks(kc_raw), to_half_blocks(vc_raw), pos_rows(pos_k), pos_rows(pos_v),
      w1_k.astype(BF16), w1_v.astype(BF16), pad_lanes(w2_k).astype(BF16), pad_lanes(w2_v).T.astype(BF16),
      pad_lanes(k_gain.reshape(1, dh)))


def _q_proj_kernel(x_ref, gn_ref, w_ref, qgain_ref, qconst_ref, q_ref, gate_ref):
    d_q = q_ref.shape[1]
    xn = _rms(x_ref[...], gn_ref[...]).astype(BF16)
    proj = _dot(xn, w_ref[...])
    ones = jnp.ones((LANES, LANES), BF16)
    heads = [slice(h * LANES, (h + 1) * LANES) for h in range(d_q // LANES)]
    splits = [_split_bf16(proj[:, lanes] * proj[:, lanes]) for lanes in heads]
    sums = [_dot(hi, ones) + _dot(lo, ones) for hi, lo in splits]
    for lanes, ss in zip(heads, sums):
        q_ref[:, lanes] = (proj[:, lanes] * lax.rsqrt(ss * (1.0 / B_HEAD_DIM) + NORM_EPS) * qgain_ref[:, lanes]
                           + qconst_ref[:, lanes]).astype(BF16)
    gate_ref[...] = _sigmoid(proj[:, d_q:])


def nsa_q_proj(h2d, g_norm, w_in, q_norm, *, tm=512):
    t_total, d = h2d.shape
    g, hpg, dh = B_KV_HEADS, B_GROUP, B_HEAD_DIM
    d_q = B_HEADS * LANES
    wq = jnp.pad(w_in[:, :B_HEADS * dh].reshape(d, B_HEADS, dh), ((0, 0), (0, 0), (0, LANES - dh))).reshape(d, d_q)
    wg = w_in[:, B_HEADS * dh:].reshape(d, g, hpg * 3)
    wg = jnp.pad(wg, ((0, 0), (0, 0), (0, LANES - hpg * 3))).reshape(d, g * LANES)
    w = jnp.concatenate([wq, wg], axis=1).astype(BF16)
    qgain = jnp.tile(jnp.pad(q_norm * (dh ** -0.5 * LOG2E), (0, LANES - dh)), B_HEADS).reshape(1, d_q)
    slopes = np.array([2.0 ** (-8.0 * (h + 1) / B_HEADS) for h in range(B_HEADS)], dtype=np.float64) * LOG2E
    feats = _alibi_query_part(jnp.asarray(slopes, F32)).astype(F32)
    qconst = jnp.pad(feats, ((0, 0), (dh, LANES - dh - feats.shape[1]))).reshape(1, d_q)
    const2 = lambda i: (0, 0)
    return pl.pallas_call(
        _q_proj_kernel,
        grid=(t_total // tm,),
        in_specs=[
            pl.BlockSpec((tm, d), lambda i: (i, 0)),
            _resident((1, d), const2),
            _resident((d, d_q + g * LANES), const2),
            _resident((1, d_q), const2),
            _resident((1, d_q), const2),
        ],
        out_specs=[pl.BlockSpec((tm, d_q), lambda i: (i, 0)), pl.BlockSpec((tm, g * LANES), lambda i: (i, 0))],
        out_shape=[jax.ShapeDtypeStruct((t_total, d_q), BF16), jax.ShapeDtypeStruct((t_total, g * LANES), F32)],
        compiler_params=_cparams(1),
        name="nsa_q_proj",
    )(h2d, g_norm.reshape(1, d), w, qgain, qconst)


def _nsa_attn_kernel(q_ref, gate_ref, kca_ref, vct_ref, ks_ref, vst_ref, kw_ref, vwt_ref,
                     ovlt_ref, eselt_ref, o_ref, live_smem, m_scr, acc_scr, *, tq, tk):
    hpg, dh = B_GROUP, B_HEAD_DIM
    t0 = pl.program_id(2) * tq
    rows = hpg * tq
    qb = q_ref[...]
    q_st = jnp.concatenate([qb[:, i * LANES:(i + 1) * LANES] for i in range(hpg)], axis=0)

    t_l = t0 + lax.broadcasted_iota(jnp.int32, (1, tq), 1)

    def normalised(acc):
        return acc[:dh] / acc[dh:2 * dh]

    span = WINDOW + tq
    w0 = pl.multiple_of(jnp.maximum(t0 - WINDOW, 0), LANES)
    dist = t_l - (w0 + lax.broadcasted_iota(jnp.int32, (span, 1), 0))
    pen = jnp.where((dist >= 0) & (dist < WINDOW), 0.0, NEG)
    s = _dot_nt(kw_ref[0, 0, pl.ds(w0, span), :], q_st) + jnp.concatenate([pen] * hpg, axis=1)
    m = jnp.max(s, axis=0, keepdims=True)
    o_win = normalised(_dot(vwt_ref[0, 0, :, pl.ds(w0, span)], jnp.exp2(s - m).astype(BF16)))

    n_half = kca_ref.shape[0]
    c_end = lax.broadcasted_iota(jnp.int32, (n_half, 1), 0) * CMP_STRIDE + (CMP_BLOCK - 1)
    t_lane = t0 + lax.broadcasted_iota(jnp.int32, (1, rows), 1) % tq
    seen = t_lane >= c_end
    s = jnp.where(seen, _dot_nt(kca_ref[...], q_st), NEG)
    m = jnp.max(s, axis=0, keepdims=True)
    e = jnp.where(seen, jnp.exp2(s - m), 0.0)
    p = e / jnp.maximum(jnp.sum(e, axis=0, keepdims=True), TINY)
    o_cmp = _dot(vct_ref[0], p.astype(BF16))[:dh]

    p_grp = p[:, 0:tq]
    for i in range(1, hpg):
        p_grp = p_grp + p[:, i * tq:(i + 1) * tq]
    p_hi, p_lo = _split_bf16(p_grp)
    n_blk = LANES // 2
    imp = (_dot(ovlt_ref[...], p_hi) + _dot(ovlt_ref[...], p_lo))[:n_blk]
    blk = lax.broadcasted_iota(jnp.int32, (n_blk, tq), 0)
    valid = blk * SEL_BLOCK <= t_l
    cur = t_l // SEL_BLOCK
    forced = valid & ((blk == 0) | (blk == cur) | (blk == cur - 1))
    work = jnp.where(forced, SEL_BIG, jnp.where(valid, imp, -SEL_BIG))
    blk_f = blk.astype(F32)
    chosen = jnp.zeros((n_blk, tq), F32)
    for _ in range(N_SELECT):
        top = jnp.max(work, axis=0, keepdims=True)
        first = jnp.min(jnp.where(work == top, blk_f, float(LANES)), axis=0, keepdims=True)
        hit = blk_f == first
        chosen = jnp.where(hit & (top > -0.5 * SEL_BIG), 1.0, chosen)
        work = jnp.where(hit, -3e38, work)
    blocks_per_tile = tk // SEL_BLOCK
    any_q = jnp.max(chosen, axis=1, keepdims=True)
    for kt in range(n_blk // blocks_per_tile):
        tile_any = jnp.max(any_q[kt * blocks_per_tile:(kt + 1) * blocks_per_tile])
        live_smem[kt] = (tile_any > 0.5).astype(jnp.int32)
    chosen = jnp.concatenate([chosen, jnp.zeros((LANES - n_blk, tq), F32)], axis=0).astype(BF16)

    m_scr[...] = jnp.full(m_scr.shape, NEG, F32)
    acc_scr[...] = jnp.zeros(acc_scr.shape, F32)

    def sel_step(kt, carry):
        @pl.when(live_smem[kt] != 0)
        def _():
            k0 = pl.multiple_of(kt * tk, tk)
            pos = k0 + lax.broadcasted_iota(jnp.int32, (tk, 1), 0)
            picked = _dot(eselt_ref[pl.ds(k0, tk), :], chosen)
            pen = jnp.where((picked > 0.5) & (pos <= t_l), 0.0, NEG)
            sc = _dot_nt(ks_ref[0, 0, pl.ds(k0, tk), :], q_st) + jnp.concatenate([pen] * hpg, axis=1)
            m_run = m_scr[...]
            m_new = jnp.maximum(m_run, jnp.broadcast_to(jnp.max(sc, axis=0, keepdims=True), m_run.shape))
            pr = jnp.exp2(sc - m_new[0:1]).astype(BF16)
            acc_scr[...] = (jnp.exp2(m_run - m_new)[0:1] * acc_scr[...]
                            + _dot(vst_ref[0, 0, :, pl.ds(k0, tk)], pr))
            m_scr[...] = m_new
        return carry

    lax.fori_loop(0, (t0 + tq + tk - 1) // tk, sel_step, 0)
    o_sel = normalised(acc_scr[...])

    gate_t = gate_ref[...].T
    outs = []
    for i in range(hpg):
        c = slice(i * tq, (i + 1) * tq)
        outs.append(gate_t[3 * i:3 * i + 1] * o_cmp[:, c] + gate_t[3 * i + 1:3 * i + 2] * o_sel[:, c]
                    + gate_t[3 * i + 2:3 * i + 3] * o_win[:, c])
    o_ref[...] = jnp.concatenate(outs, axis=0).T.astype(BF16)


def nsa_attention(q, gates, kca, vct, ks, vst, kw, vwt, bsz, seq, *, tq=256, tk=512):
    g, hpg, dh = B_KV_HEADS, B_GROUP, B_HEAD_DIM
    n_q = seq // tq
    n_half = seq // CMP_STRIDE
    n_cmp = n_half - 1
    n_sel = seq // SEL_BLOCK
    assert n_sel <= LANES // 2 and tq % LANES == 0
    c_start = np.arange(n_half) * CMP_STRIDE
    j_sel = np.arange(LANES)
    ovl = ((c_start[:, None] < (j_sel[None, :] + 1) * SEL_BLOCK)
           & (c_start[:, None] + CMP_BLOCK > j_sel[None, :] * SEL_BLOCK)
           & (np.arange(n_half)[:, None] < n_cmp) & (j_sel[None, :] < n_sel))
    esel = (np.arange(seq)[None, :] // SEL_BLOCK) == j_sel[:, None]
    row = lambda b, j, i: (b * n_q + i, j)
    per_bg = lambda b, j, i: (b, j, 0, 0)
    return pl.pallas_call(
        functools.partial(_nsa_attn_kernel, tq=tq, tk=tk),
        grid=(bsz, g, n_q),
        in_specs=[
            pl.BlockSpec((tq, hpg * LANES), row),
            pl.BlockSpec((tq, LANES), row),
            pl.BlockSpec((n_half, LANES), lambda b, j, i: (b * g + j, 0)),
            pl.BlockSpec((1, LANES, n_half), lambda b, j, i: (b * g + j, 0, 0)),
            pl.BlockSpec((1, 1, seq, LANES), per_bg),
            pl.BlockSpec((1, 1, LANES, seq), per_bg),
            pl.BlockSpec((1, 1, seq, LANES), per_bg),
            pl.BlockSpec((1, 1, LANES, seq), per_bg),
            _resident((LANES, n_half), lambda b, j, i: (0, 0)),
            _resident((seq, LANES), lambda b, j, i: (0, 0)),
        ],
        out_specs=pl.BlockSpec((tq, hpg * dh), row),
        out_shape=jax.ShapeDtypeStruct((bsz * seq, B_HEADS * dh), BF16),
        scratch_shapes=[pltpu.SMEM((seq // tk,), jnp.int32), pltpu.VMEM((SUBLANES, hpg * tq), F32),
                        pltpu.VMEM((LANES, hpg * tq), F32)],
        compiler_params=_cparams(3),
        name="nsa_attention",
    )(q, gates, kca, vct, ks, vst, kw, vwt, jnp.asarray(ovl.T, BF16), jnp.asarray(esel.T, BF16))


def _out_proj_kernel(h_ref, o_ref, w_ref, out_ref):
    out_ref[...] = h_ref[...] + _dot(o_ref[...], w_ref[...])


def out_proj_residual(h2d, o, w_out, *, tm=512):
    t_total, d = h2d.shape
    k = o.shape[1]
    return pl.pallas_call(
        _out_proj_kernel,
        grid=(t_total // tm,),
        in_specs=[pl.BlockSpec((tm, d), lambda i: (i, 0)), pl.BlockSpec((tm, k), lambda i: (i, 0)),
                  _resident((k, d), lambda i: (0, 0))],
        out_specs=pl.BlockSpec((tm, d), lambda i: (i, 0)),
        out_shape=jax.ShapeDtypeStruct((t_total, d), F32),
        compiler_params=_cparams(1),
        name="out_proj_residual",
    )(h2d, o, w_out.astype(BF16))


def nsa_shared_kv(h2d, bsz, seq, kv_norm, kv_w, cmp_pos_k, cmp_w1_k, cmp_w2_k, cmp_pos_v, cmp_w1_v, cmp_w2_v, k_norm):
    ks, kw, vst, vwt, kc_raw, vc_raw = nsa_kv_proj(h2d, bsz, seq, kv_norm, kv_w, k_norm)
    kca, vct = nsa_compress(kc_raw, vc_raw, bsz, seq, cmp_pos_k, cmp_w1_k, cmp_w2_k,
                            cmp_pos_v, cmp_w1_v, cmp_w2_v, k_norm[0])
    return kca, vct, ks, vst, kw, vwt


def nsa_mixer(h2d, bsz, seq, g_norm, w_in, w_out, q_norm, kv):
    q, gates = nsa_q_proj(h2d, g_norm, w_in, q_norm)
    o = nsa_attention(q, gates, *kv, bsz, seq)
    return out_proj_residual(h2d, o, w_out)


def kernel(x, norm_mix, norm_ffn, hgrn_w_in, hgrn_w_out, hgrn_g_norm, hgrn_lb_logits, kv_norm, kv_w, cmp_pos_k, cmp_w1_k, cmp_w2_k, cmp_pos_v, cmp_w1_v, cmp_w2_v, k_norm, nsa_w_in, nsa_w_out, nsa_q_norm, ffn_w_gu, ffn_w_down, moe_router, moe_w_gu, moe_w_down):
    bsz, seq, d = x.shape
    lb = jnp.cumsum(jax.nn.softmax(hgrn_lb_logits.astype(F32), axis=0), axis=0)
    lb = lb - lb[0:1]
    n_a = hgrn_w_in.shape[0]
    depth = norm_mix.shape[0]
    h = x.reshape(bsz * seq, d)
    kv = None
    for layer in range(depth):
        if layer < n_a:
            h = hgrn_mixer(h, seq, norm_mix[layer], hgrn_w_in[layer], hgrn_w_out[layer], hgrn_g_norm[layer],
                           lb[layer])
        else:
            if kv is None:
                kv = nsa_shared_kv(h, bsz, seq, kv_norm, kv_w, cmp_pos_k, cmp_w1_k, cmp_w2_k,
                                   cmp_pos_v, cmp_w1_v, cmp_w2_v, k_norm)
            j = layer - n_a
            h = nsa_mixer(h, bsz, seq, norm_mix[layer], nsa_w_in[j], nsa_w_out[j], nsa_q_norm[j], kv)
        if layer % 2 == 0:
            h = ffn_swiglu(h, norm_ffn[layer], ffn_w_gu[layer // 2], ffn_w_down[layer // 2])
        else:
            h = moe_swiglu_top2(h, norm_ffn[layer], moe_router[layer // 2], moe_w_gu[layer // 2],
                                moe_w_down[layer // 2])
    return h.reshape(bsz, seq, d)
```

```python
import functools

import numpy as np
import jax
import jax.numpy as jnp
from jax import lax
from jax.experimental import pallas as pl
from jax.experimental.pallas import tpu as pltpu

F32 = jnp.float32
BF16 = jnp.bfloat16

NORM_EPS = 1e-6
VMEM_LIMIT_BYTES = 56 * 1024 * 1024

A_HEADS = 8
A_HEAD_DIM = 128
SCAN_ROWS = 128
SUBLANES = 8


def _cparams(n_axes):
    return pltpu.CompilerParams(dimension_semantics=("arbitrary",) * n_axes,
                                vmem_limit_bytes=VMEM_LIMIT_BYTES)


def _dot(a, b):
    return jnp.dot(a, b, preferred_element_type=F32)


def _dot_nt(a, b):
    return lax.dot_general(a, b, (((1,), (1,)), ((), ())), preferred_element_type=F32)


def _dot_tn(a, b):
    return lax.dot_general(a, b, (((0,), (0,)), ((), ())), preferred_element_type=F32)


def _sigmoid(x):
    return 0.5 * jnp.tanh(0.5 * x) + 0.5


def _rms(x, g):
    return x * lax.rsqrt(jnp.mean(x * x, axis=-1, keepdims=True) + NORM_EPS) * g


def _split_bf16(x):
    hi = x.astype(BF16)
    lo = (x - hi.astype(F32)).astype(BF16)
    return hi, lo


def _hgrn_constants():
    n = SCAN_ROWS
    t = np.arange(n)
    r = np.arange(n)[None, :]
    sums, masks = [], []
    m = n // 2
    while m >= 1:
        grp = t // (2 * m)
        mid = grp * 2 * m + m
        upper = (t % (2 * m)) >= m
        if m < SUBLANES:
            sums.append(np.where(upper[:, None], (r >= mid[:, None]) & (r <= t[:, None]),
                                 (r > t[:, None]) & (r < mid[:, None])))
        masks.append((grp[:, None] == grp[None, :]) & upper[:, None] & (~upper[None, :]))
        m //= 2
    masks.append(np.eye(n, dtype=bool))
    sums.append(r <= t[:, None])
    w = np.concatenate(sums, 0).astype(np.float32)
    return np.concatenate([w, w], 1), np.stack(masks).astype(np.float32)


def _hgrn_kernel(x_ref, gn_ref, win_ref, lb_ref, ghn_ref, wout_ref, wsum_ref, msk_ref,
                 out_ref, xn_scr, st_scr, og_scr, *, tm):
    sblk = pl.program_id(1)
    h = pl.program_id(2)
    n = SCAN_ROWS
    n_lvl = msk_ref.shape[0] - 1

    @pl.when(h == 0)
    def _():
        xn_scr[...] = _rms(x_ref[...], gn_ref[...]).astype(BF16)

    @pl.when(sblk == 0)
    def _():
        st_scr[h] = jnp.zeros((A_HEAD_DIM, A_HEAD_DIM), F32)

    proj = _dot(xn_scr[...], win_ref[...])
    q = proj[:, 0:128]
    f = proj[:, 128:256]
    v = proj[:, 256:384]
    g = proj[:, 384:512]
    lb = lb_ref[0:1, :]
    log_lb = lb_ref[1:2, :]
    log_1m_lb = lb_ref[2:3, :]
    qs = q * _sigmoid(q)
    a = jnp.exp(-jnp.abs(f))
    inv = 1.0 / (1.0 + a)
    log_sig = jnp.minimum(f, 0.0) - jnp.log1p(a)
    b = log_1m_lb + log_sig
    logf = jnp.maximum(log_lb, b) + jnp.log1p(jnp.exp(-jnp.abs(log_lb - b)))
    key = (1.0 - lb) * jnp.where(f >= 0.0, a * inv, inv)

    wide = [n >> (l + 1) for l in range(n_lvl) if n >> (l + 1) >= SUBLANES]
    row_id = lax.broadcasted_iota(jnp.int32, (n, A_HEAD_DIM), 0)
    in_upper_half = [(row_id % (2 * m)) >= m for m in wide]

    blocks = [slice(r * n, (r + 1) * n) for r in range(tm // n)]
    n_narrow = n_lvl - len(wide)
    sums = []
    for rows in blocks:
        hi, lo = _split_bf16(logf[rows])
        sums.append(_dot(wsum_ref[...], jnp.concatenate([hi, lo], axis=0)))
    e_lvl, e_prefix, e_suffix = [], [], []
    for sm in sums:
        cum = sm[n_narrow * n:]
        e_blk = []
        for m, upper in zip(wide, in_upper_half):
            bound = jnp.concatenate([jnp.broadcast_to(cum[i + m - 1:i + m], (2 * m, A_HEAD_DIM))
                                     for i in range(0, n, 2 * m)], axis=0)
            e_blk.append(jnp.exp(jnp.where(upper, cum - bound, bound - cum)))
        e_sm = jnp.exp(sm)
        e_lvl.append(e_blk + [e_sm[i * n:(i + 1) * n] for i in range(n_narrow)])
        e_prefix.append(e_sm[n_narrow * n:])
        e_suffix.append(jnp.exp(cum[n - 1:n] - cum))
    intra, inter_q, kv = [], [], []
    for rows, e_blk, e_pre, e_suf in zip(blocks, e_lvl, e_prefix, e_suffix):
        qb = qs[rows]
        kb = key[rows]
        vb = v[rows].astype(BF16)
        s = msk_ref[n_lvl] * _dot_nt(qb.astype(BF16), kb.astype(BF16))
        for l in range(n_lvl):
            s = s + msk_ref[l] * _dot_nt((qb * e_blk[l]).astype(BF16), (kb * e_blk[l]).astype(BF16))
        intra.append(_dot(s.astype(BF16), vb))
        inter_q.append((qb * e_pre).astype(BF16))
        kv.append(_dot_tn(vb, (kb * e_suf).astype(BF16)))
    st = st_scr[h]
    outs = []
    for o_intra, q_pre, e_pre, kv_blk in zip(intra, inter_q, e_prefix, kv):
        outs.append(o_intra + _dot_nt(q_pre, st.astype(BF16)))
        st = st * e_pre[n - 1:n, :] + kv_blk
    st_scr[h] = st
    o = jnp.concatenate(outs, axis=0)

    og_scr[:, pl.ds(pl.multiple_of(h * A_HEAD_DIM, A_HEAD_DIM), A_HEAD_DIM)] = (
        _rms(o, ghn_ref[...]) * (g * _sigmoid(g))).astype(BF16)

    @pl.when(h == A_HEADS - 1)
    def _():
        out_ref[...] = x_ref[...] + _dot(og_scr[...], wout_ref[...])


def hgrn_mixer(h2d, seq, g_norm_in, w_in, w_out, g_head, lb, *, tm=1024):
    t_total, d = h2d.shape
    n_sblk = seq // tm
    hd = A_HEAD_DIM
    w_perm = w_in.reshape(d, 4, A_HEADS, hd).transpose(0, 2, 1, 3).reshape(d, 4 * A_HEADS * hd).astype(BF16)
    lb_rows = jnp.zeros((8, A_HEADS * hd), F32)
    lb_rows = lb_rows.at[0].set(lb).at[1].set(jnp.log(lb)).at[2].set(jnp.log1p(-lb))
    wsum, masks = _hgrn_constants()
    n_blk = wsum.shape[0]
    grid = (t_total // seq, n_sblk, A_HEADS)
    row = lambda b, s, h: (b * n_sblk + s, 0)
    const2 = lambda b, s, h: (0, 0)
    return pl.pallas_call(
        functools.partial(_hgrn_kernel, tm=tm),
        grid=grid,
        in_specs=[
            pl.BlockSpec((tm, d), row),
            pl.BlockSpec((1, d), const2),
            pl.BlockSpec((d, 4 * hd), lambda b, s, h: (0, h)),
            pl.BlockSpec((8, hd), lambda b, s, h: (0, h)),
            pl.BlockSpec((1, hd), const2),
            _resident((A_HEADS * hd, d), const2),
            _resident((n_blk, 2 * SCAN_ROWS), const2),
            _resident(masks.shape, lambda b, s, h: (0, 0, 0)),
        ],
        out_specs=pl.BlockSpec((tm, d), row),
        out_shape=jax.ShapeDtypeStruct((t_total, d), F32),
        scratch_shapes=[pltpu.VMEM((tm, d), BF16), pltpu.VMEM((A_HEADS, hd, hd), F32),
                        pltpu.VMEM((tm, A_HEADS * hd), BF16)],
        compiler_params=_cparams(3),
        name="hgrn_mixer",
    )(h2d, g_norm_in.reshape(1, d), w_perm, lb_rows, g_head.reshape(1, hd), w_out.astype(BF16),
      jnp.asarray(wsum, BF16), jnp.asarray(masks, F32))


def _resident(block_shape, index_map):
    return pl.BlockSpec(block_shape, index_map, pipeline_mode=pl.Buffered(1))


MXU_TILE = 256


def _swiglu(xn, wg_ref, wu_ref, wd_ref):
    acc = None
    for c in range(wg_ref.shape[1] // MXU_TILE):
        cols = slice(c * MXU_TILE, (c + 1) * MXU_TILE)
        gate = _dot(xn, wg_ref[:, cols])
        up = _dot(xn, wu_ref[:, cols])
        part = _dot((gate * _sigmoid(gate) * up).astype(BF16), wd_ref[cols, :])
        acc = part if acc is None else acc + part
    return acc


def _ffn_kernel(*refs, with_mixer_out):
    if with_mixer_out:
        x_ref, o_ref, wo_ref, gn_ref, wg_ref, wu_ref, wd_ref, out_ref = refs
        x = x_ref[...] + _dot(o_ref[...], wo_ref[...])
    else:
        x_ref, gn_ref, wg_ref, wu_ref, wd_ref, out_ref = refs
        x = x_ref[...]
    xn = _rms(x, gn_ref[...]).astype(BF16)
    out_ref[...] = x + _swiglu(xn, wg_ref, wu_ref, wd_ref)


def ffn_swiglu(h2d, g_norm, w_gu, w_down, *, mixer_out=None, tm=512):
    t_total, d = h2d.shape
    ff = w_down.shape[0]
    w_gu = w_gu.astype(BF16)
    const2 = lambda i: (0, 0)
    rows = lambda i: (i, 0)
    mixer_specs, mixer_args = [], []
    if mixer_out is not None:
        o, w_out = mixer_out
        mixer_specs = [pl.BlockSpec((tm, o.shape[1]), rows), _resident(w_out.shape, const2)]
        mixer_args = [o, w_out.astype(BF16)]
    return pl.pallas_call(
        functools.partial(_ffn_kernel, with_mixer_out=mixer_out is not None),
        grid=(t_total // tm,),
        in_specs=[pl.BlockSpec((tm, d), rows)] + mixer_specs + [
            _resident((1, d), const2),
            _resident((d, ff), const2),
            _resident((d, ff), lambda i: (0, 1)),
            _resident((ff, d), const2),
        ],
        out_specs=pl.BlockSpec((tm, d), rows),
        out_shape=jax.ShapeDtypeStruct((t_total, d), F32),
        compiler_params=_cparams(1),
        name="ffn_swiglu",
    )(h2d, *mixer_args, g_norm.reshape(1, d), w_gu, w_gu, w_down.astype(BF16))


N_EXPERTS = 8
LANES = 128
NEG = -1e30


def _router_kernel(x_ref, gn_ref, wh_ref, wl_ref, sel_ref):
    xn = _rms(x_ref[...], gn_ref[...])
    xh, xl = _split_bf16(xn)
    logits = _dot(xh, wh_ref[...]) + (_dot(xl, wh_ref[...]) + _dot(xh, wl_ref[...]))
    lane = lax.broadcasted_iota(jnp.int32, logits.shape, 1).astype(F32)
    logits = jnp.where(lane < N_EXPERTS, logits, NEG)
    m1 = jnp.max(logits, axis=-1, keepdims=True)
    i1 = jnp.min(jnp.where(logits == m1, lane, float(LANES)), axis=-1, keepdims=True)
    rest = jnp.where(lane == i1, NEG, logits)
    m2 = jnp.max(rest, axis=-1, keepdims=True)
    i2 = jnp.min(jnp.where(rest == m2, lane, float(LANES)), axis=-1, keepdims=True)
    e2 = jnp.exp(m2 - m1)
    denom = 1.0 + e2
    sel_ref[...] = jnp.where(lane == 0, i1, jnp.where(lane == 1, i2, jnp.where(
        lane == 2, 1.0 / denom, jnp.where(lane == 3, e2 / denom, 0.0))))


def moe_route(h2d, g_norm, w_router, *, tm=512):
    t_total, d = h2d.shape
    w_pad = jnp.zeros((d, LANES), F32).at[:, :N_EXPERTS].set(w_router)
    w_hi = w_pad.astype(BF16)
    w_lo = (w_pad - w_hi.astype(F32)).astype(BF16)
    const2 = lambda i: (0, 0)
    return pl.pallas_call(
        _router_kernel,
        grid=(t_total // tm,),
        in_specs=[
            pl.BlockSpec((tm, d), lambda i: (i, 0)),
            _resident((1, d), const2),
            _resident((d, LANES), const2),
            _resident((d, LANES), const2),
        ],
        out_specs=pl.BlockSpec((tm, LANES), lambda i: (i, 0)),
        out_shape=jax.ShapeDtypeStruct((t_total, LANES), F32),
        compiler_params=_cparams(1),
        name="moe_router",
    )(h2d, g_norm.reshape(1, d), w_hi, w_lo)


MOE_TILE = 1024
GATHER_UNROLL = 8


def _moe_plan(sel, tm):
    t_total = sel.shape[0]
    e = sel[:, :2].astype(jnp.int32)
    onehot = (e[:, :, None] == jnp.arange(N_EXPERTS, dtype=jnp.int32)[None, None, :]).astype(jnp.int32).sum(1)
    csum = jnp.cumsum(onehot, axis=0)
    rank = csum - onehot
    padded = (csum[-1] + tm - 1) // tm * tm
    ends = jnp.cumsum(padded)
    pos = (ends - padded)[e] + jnp.take_along_axis(rank, e, axis=1)
    n_slots = 2 * t_total + N_EXPERTS * tm
    tile_start = jnp.arange(n_slots // tm, dtype=jnp.int32) * tm
    tile_expert = jnp.minimum(jnp.searchsorted(ends, tile_start, side="right"), N_EXPERTS - 1).astype(jnp.int32)
    tile_used = (tile_start < ends[-1]).astype(jnp.int32)
    n_used = (ends[-1:] // tm).astype(jnp.int32)
    last_tile = jnp.maximum(ends - tm, 0).astype(jnp.int32)
    return n_slots, tile_expert, tile_used, n_used, last_tile, (padded > 0).astype(jnp.int32), pos[:, 0], pos[:, 1]


def _load_indices(idx_hbm, idx_smem, sem, tile, tm):
    return pltpu.make_async_copy(idx_hbm.at[pl.ds(tile * tm, tm)], idx_smem, sem)


def _moe_dispatch_kernel(last_tile_ref, has_rows_ref, n_used_ref, pa_hbm, pb_hbm, h_ref, xs_hbm, ia_smem, ib_smem,
                         zero_scr, row_sem, idx_sem, zero_sem, *, tm):
    i = pl.program_id(0)

    @pl.when(i == 0)
    def _():
        zero_scr[...] = jnp.zeros(zero_scr.shape, F32)
        n_tiles = xs_hbm.shape[0] // tm
        for e in range(N_EXPERTS):
            spare = n_used_ref[0] + e
            for start, wanted in ((last_tile_ref[e], has_rows_ref[e] != 0), (spare * tm, spare < n_tiles)):
                fill = pltpu.make_async_copy(zero_scr, xs_hbm.at[pl.ds(pl.multiple_of(start, tm), tm), :],
                                             zero_sem.at[e])

                @pl.when(wanted)
                def _():
                    fill.start()
                    fill.wait()

    ca = _load_indices(pa_hbm, ia_smem, idx_sem.at[0], i, tm)
    cb = _load_indices(pb_hbm, ib_smem, idx_sem.at[1], i, tm)
    ca.start()
    cb.start()
    ca.wait()
    cb.wait()

    def body(r, carry):
        src = h_ref.at[pl.ds(r, 1), :]
        pltpu.make_async_copy(src, xs_hbm.at[pl.ds(ia_smem[r], 1), :], row_sem.at[0]).start()
        pltpu.make_async_copy(src, xs_hbm.at[pl.ds(ib_smem[r], 1), :], row_sem.at[1]).start()
        return carry
    lax.fori_loop(0, tm, body, 0, unroll=GATHER_UNROLL)
    pltpu.make_async_copy(h_ref, xs_hbm.at[pl.ds(0, tm), :], row_sem.at[0]).wait()
    pltpu.make_async_copy(h_ref, xs_hbm.at[pl.ds(0, tm), :], row_sem.at[1]).wait()


def _moe_group_kernel(te_ref, used_ref, xs_ref, gn_ref, wg_ref, wu_ref, wd_ref, y_ref, xn_scr):
    i = pl.program_id(0)
    j = pl.program_id(1)

    @pl.when(used_ref[i] == 0)
    def _():
        y_ref[...] = jnp.zeros(y_ref.shape, F32)

    @pl.when(used_ref[i] != 0)
    def _():
        @pl.when(j == 0)
        def _():
            xn_scr[...] = _rms(xs_ref[...], gn_ref[...]).astype(BF16)

        contrib = _swiglu(xn_scr[...], wg_ref.at[0], wu_ref.at[0], wd_ref.at[0])

        @pl.when(j == 0)
        def _():
            y_ref[...] = contrib

        @pl.when(j > 0)
        def _():
            y_ref[...] += contrib


def _moe_combine_kernel(pa_hbm, pb_hbm, y_hbm, h_ref, sel_ref, out_ref, ia_smem, ib_smem, buf_a, buf_b,
                        row_sem, idx_sem, *, tm):
    i = pl.program_id(0)
    ca = _load_indices(pa_hbm, ia_smem, idx_sem.at[0], i, tm)
    cb = _load_indices(pb_hbm, ib_smem, idx_sem.at[1], i, tm)
    ca.start()
    cb.start()
    ca.wait()
    cb.wait()

    def body(r, carry):
        pltpu.make_async_copy(y_hbm.at[pl.ds(ia_smem[r], 1), :], buf_a.at[pl.ds(r, 1), :], row_sem.at[0]).start()
        pltpu.make_async_copy(y_hbm.at[pl.ds(ib_smem[r], 1), :], buf_b.at[pl.ds(r, 1), :], row_sem.at[1]).start()
        return carry
    lax.fori_loop(0, tm, body, 0, unroll=GATHER_UNROLL)
    pltpu.make_async_copy(buf_a, buf_a, row_sem.at[0]).wait()
    pltpu.make_async_copy(buf_b, buf_b, row_sem.at[1]).wait()
    sel = sel_ref[...]
    out_ref[...] = h_ref[...] + (sel[:, 2:3] * buf_a[...] + sel[:, 3:4] * buf_b[...])


def moe_swiglu_top2(h2d, g_norm, w_router, w_gu, w_down, *, n_chunks=2):
    t_total, d = h2d.shape
    tm = MOE_TILE
    ff = w_down.shape[1]
    tf = ff // n_chunks
    sel = moe_route(h2d, g_norm, w_router)
    n_slots, tile_expert, tile_used, n_used, last_tile, has_rows, pos_a, pos_b = _moe_plan(sel, tm)
    any_spec = pl.BlockSpec(memory_space=pl.ANY)
    index_scratch = [pltpu.SMEM((tm,), jnp.int32), pltpu.SMEM((tm,), jnp.int32)]
    sems = [pltpu.SemaphoreType.DMA((2,)), pltpu.SemaphoreType.DMA((2,))]
    xs = pl.pallas_call(
        functools.partial(_moe_dispatch_kernel, tm=tm),
        grid_spec=pltpu.PrefetchScalarGridSpec(
            num_scalar_prefetch=3,
            grid=(t_total // tm,),
            in_specs=[any_spec, any_spec, pl.BlockSpec((tm, d), lambda i, lt, hr, nu: (i, 0))],
            out_specs=any_spec,
            scratch_shapes=index_scratch + [pltpu.VMEM((tm, d), F32)] + sems
            + [pltpu.SemaphoreType.DMA((N_EXPERTS,))],
        ),
        out_shape=jax.ShapeDtypeStruct((n_slots, d), F32),
        compiler_params=_cparams(1),
        name="moe_dispatch",
    )(last_tile, has_rows, n_used, pos_a, pos_b, h2d)
    w_gu = w_gu.astype(BF16)
    row = lambda i, j, te, used: (i, 0)
    y = pl.pallas_call(
        _moe_group_kernel,
        grid_spec=pltpu.PrefetchScalarGridSpec(
            num_scalar_prefetch=2,
            grid=(n_slots // tm, n_chunks),
            in_specs=[
                pl.BlockSpec((tm, d), row),
                pl.BlockSpec((1, d), lambda i, j, te, used: (0, 0)),
                pl.BlockSpec((1, d, tf), lambda i, j, te, used: (te[i], 0, j)),
                pl.BlockSpec((1, d, tf), lambda i, j, te, used: (te[i], 0, j + n_chunks)),
                pl.BlockSpec((1, tf, d), lambda i, j, te, used: (te[i], j, 0)),
            ],
            out_specs=pl.BlockSpec((tm, d), row),
            scratch_shapes=[pltpu.VMEM((tm, d), BF16)],
        ),
        out_shape=jax.ShapeDtypeStruct((n_slots, d), F32),
        compiler_params=_cparams(2),
        name="moe_group_ffn",
    )(tile_expert, tile_used, xs, g_norm.reshape(1, d), w_gu, w_gu, w_down.astype(BF16))
    return pl.pallas_call(
        functools.partial(_moe_combine_kernel, tm=tm),
        grid=(t_total // tm,),
        in_specs=[any_spec, any_spec, any_spec, pl.BlockSpec((tm, d), lambda i: (i, 0)),
                  pl.BlockSpec((tm, LANES), lambda i: (i, 0))],
        out_specs=pl.BlockSpec((tm, d), lambda i: (i, 0)),
        out_shape=jax.ShapeDtypeStruct((t_total, d), F32),
        scratch_shapes=index_scratch + [pltpu.VMEM((tm, d), F32), pltpu.VMEM((tm, d), F32)] + sems,
        compiler_params=_cparams(1),
        name="moe_combine",
    )(pos_a, pos_b, y, h2d, sel)


B_HEADS = 16
B_KV_HEADS = 4
B_GROUP = 4
B_HEAD_DIM = 64
CMP_STRIDE = 16
CMP_BLOCK = 32
SEL_BLOCK = 32
N_SELECT = 8
WINDOW = 512
SEL_BIG = 1e9
TINY = 1e-30
N_KV_KINDS = 6

LOG2E = 1.4426950408889634
POS_RADIX = 64
N_SLOPE_PARTS = 3


def _alibi_key_part(pos, idx):
    hi = (pos // POS_RADIX * POS_RADIX).astype(F32)
    lo = (pos % POS_RADIX).astype(F32)
    return jnp.where(idx < N_SLOPE_PARTS, hi, jnp.where(idx < 2 * N_SLOPE_PARTS, lo, 0.0))


def _alibi_query_part(slopes):
    parts, rest = [], slopes.astype(F32)
    for _ in range(N_SLOPE_PARTS):
        p = rest.astype(BF16)
        parts.append(p)
        rest = rest - p.astype(F32)
    return jnp.stack(parts + parts, axis=-1)


def _kv_proj_kernel(x_ref, gn_ref, wk_ref, kgain_ref, wvt_ref, wc_ref,
                    ks_ref, kw_ref, vst_ref, vwt_ref, kc_ref, vc_ref):
    g = B_KV_HEADS
    hn = _rms(x_ref[...], gn_ref[...]).astype(BF16)
    k = _dot(hn, wk_ref[...])
    tm = k.shape[0]
    pos = pl.program_id(1) * tm + lax.broadcasted_iota(jnp.int32, (tm, LANES), 0)
    lane = lax.broadcasted_iota(jnp.int32, (tm, LANES), 1)
    pos_lanes = jnp.where(lane >= B_HEAD_DIM, _alibi_key_part(pos, lane - B_HEAD_DIM), 0.0)
    for j in range(2 * g):
        kj = k[:, j * LANES:(j + 1) * LANES]
        ms = jnp.sum(kj * kj, axis=-1, keepdims=True) * (1.0 / B_HEAD_DIM)
        kj = (kj * lax.rsqrt(ms + NORM_EPS) * kgain_ref[:, j * LANES:(j + 1) * LANES] + pos_lanes).astype(BF16)
        if j < g:
            ks_ref[0, j] = kj
        else:
            kw_ref[0, j - g] = kj
    vt = _dot_nt(wvt_ref[...], hn)
    row = lax.broadcasted_iota(jnp.int32, vt.shape, 0)
    vt = jnp.where(row % LANES >= B_HEAD_DIM, 1.0, vt).astype(BF16)
    for j in range(g):
        vst_ref[0, j] = vt[j * LANES:(j + 1) * LANES]
        vwt_ref[0, j] = vt[(g + j) * LANES:(g + j + 1) * LANES]
    c = _dot(hn, wc_ref[...])
    half = g * B_HEAD_DIM
    kc_ref[...] = c[:, :half].astype(BF16)
    vc_ref[...] = c[:, half:].astype(BF16)


def nsa_kv_proj(h2d, bsz, seq, kv_norm, kv_w, k_norm, *, tm=512):
    t_total, d = h2d.shape
    g, dh = B_KV_HEADS, B_HEAD_DIM
    n_sblk = seq // tm
    w = kv_w.reshape(d, N_KV_KINDS, g, dh)
    def head_slabs(kinds):
        cols = jnp.stack([w[:, kind] for kind in kinds], axis=1)
        return jnp.pad(cols, ((0, 0), (0, 0), (0, 0), (0, LANES - dh))).reshape(d, len(kinds) * g * LANES)

    wk = head_slabs((2, 4)).astype(BF16)
    wvt = head_slabs((3, 5)).T.astype(BF16)
    kgain = jnp.pad(jnp.stack([jnp.tile(k_norm[1], (g, 1)), jnp.tile(k_norm[2], (g, 1))]),
                    ((0, 0), (0, 0), (0, LANES - dh))).reshape(1, 2 * g * LANES)
    wc = jnp.concatenate([w[:, 0].reshape(d, g * dh), w[:, 1].reshape(d, g * dh)], axis=1).astype(BF16)
    const2 = lambda b, s: (0, 0)
    row = lambda b, s: (b * n_sblk + s, 0)
    k_spec = pl.BlockSpec((1, g, tm, LANES), lambda b, s: (b, 0, s, 0))
    vt_spec = pl.BlockSpec((1, g, LANES, tm), lambda b, s: (b, 0, 0, s))
    k_shape = jax.ShapeDtypeStruct((bsz, g, seq, LANES), BF16)
    vt_shape = jax.ShapeDtypeStruct((bsz, g, LANES, seq), BF16)
    c_shape = jax.ShapeDtypeStruct((t_total, g * dh), BF16)
    return pl.pallas_call(
        _kv_proj_kernel,
        grid=(bsz, n_sblk),
        in_specs=[
            pl.BlockSpec((tm, d), row),
            _resident((1, d), const2),
            _resident((d, 2 * g * LANES), const2),
            _resident((1, 2 * g * LANES), const2),
            _resident((2 * g * LANES, d), const2),
            _resident((d, 2 * g * dh), const2),
        ],
        out_specs=[k_spec, k_spec, vt_spec, vt_spec,
                   pl.BlockSpec((tm, g * dh), row), pl.BlockSpec((tm, g * dh), row)],
        out_shape=[k_shape, k_shape, vt_shape, vt_shape, c_shape, c_shape],
        compiler_params=_cparams(2),
        name="nsa_kv_proj",
    )(h2d, kv_norm.reshape(1, d), wk, kgain, wvt, wc)


def _compress_kernel(ak_ref, av_ref, posk_ref, posv_ref, w1k_ref, w1v_ref, w2k_ref, w2vt_ref, kgain_ref,
                     kca_ref, vct_ref, *, n_half):
    half_w = ak_ref.shape[1]

    def hidden(a_ref, pos_ref, w1_ref):
        a = a_ref[...]
        top = _dot(a, w1_ref[:half_w, :])
        bot = _dot(a, w1_ref[half_w:, :])
        bias = _dot(pos_ref[...], w1_ref[...])[0:1, :]
        pre = top + pltpu.roll(bot, shift=a.shape[0] - 1, axis=0) + bias
        return (pre * _sigmoid(pre)).astype(BF16)

    act_k = hidden(ak_ref, posk_ref, w1k_ref)
    kc = _dot(act_k, w2k_ref[...])
    ms = jnp.sum(kc * kc, axis=-1, keepdims=True) * (1.0 / B_HEAD_DIM)
    kc = kc * lax.rsqrt(ms + NORM_EPS) * kgain_ref[...]
    blk = lax.broadcasted_iota(jnp.int32, kc.shape, 0) % n_half
    lane = lax.broadcasted_iota(jnp.int32, kc.shape, 1)
    c_end = blk * CMP_STRIDE + (CMP_BLOCK - 1)
    kca_ref[...] = (kc + jnp.where(lane >= B_HEAD_DIM, _alibi_key_part(c_end, lane - B_HEAD_DIM), 0.0)).astype(BF16)
    act_v = hidden(av_ref, posv_ref, w1v_ref)
    vct = _dot_nt(w2vt_ref[...], act_v).astype(BF16)
    for j in range(vct.shape[1] // n_half):
        vct_ref[j] = vct[:, j * n_half:(j + 1) * n_half]


def nsa_compress(kc_raw, vc_raw, bsz, seq, pos_k, w1_k, w2_k, pos_v, w1_v, w2_v, k_gain, *, groups_per_step=4):
    g, dh = B_KV_HEADS, B_HEAD_DIM
    n_half = seq // CMP_STRIDE
    feat = CMP_STRIDE * dh

    def to_half_blocks(a):
        a = a.reshape(bsz, n_half, CMP_STRIDE, g, dh).transpose(0, 3, 1, 2, 4)
        return a.reshape(bsz * g * n_half, feat)

    def pos_rows(pos):
        return jnp.zeros((8, 2 * feat), F32).at[0].set(pos.reshape(-1)).astype(BF16)

    hid = w1_k.shape[1]
    pad_lanes = lambda a: jnp.pad(a, ((0, 0), (0, LANES - dh)))
    rows = groups_per_step * n_half
    const2 = lambda i: (0, 0)
    return pl.pallas_call(
        functools.partial(_compress_kernel, n_half=n_half),
        grid=(bsz * g // groups_per_step,),
        in_specs=[
            pl.BlockSpec((rows, feat), lambda i: (i, 0)),
            pl.BlockSpec((rows, feat), lambda i: (i, 0)),
            _resident((8, 2 * feat), const2),
            _resident((8, 2 * feat), const2),
            _resident((2 * feat, hid), const2),
            _resident((2 * feat, hid), const2),
            _resident((hid, LANES), const2),
            _resident((LANES, hid), const2),
            _resident((1, LANES), const2),
        ],
        out_specs=[pl.BlockSpec((rows, LANES), lambda i: (i, 0)),
                   pl.BlockSpec((groups_per_step, LANES, n_half), lambda i: (i, 0, 0))],
        out_shape=[jax.ShapeDtypeStruct((bsz * g * n_half, LANES), BF16),
                   jax.ShapeDtypeStruct((bsz * g, LANES, n_half), BF16)],
        compiler_params=_cparams(1),
        name="nsa_compress",
    )(to_half_blocks(kc_raw), to_half_blocks(vc_raw), pos_rows(pos_k), pos_rows(pos_v),
      w1_k.astype(BF16), w1_v.astype(BF16), pad_lanes(w2_k).astype(BF16), pad_lanes(w2_v).T.astype(BF16),
      pad_lanes(k_gain.reshape(1, dh)))


def _q_proj_kernel(x_ref, gn_ref, w_ref, qgain_ref, qconst_ref, q_ref, gate_ref):
    d_q = q_ref.shape[1]
    xn = _rms(x_ref[...], gn_ref[...]).astype(BF16)
    proj = _dot(xn, w_ref[...])
    ones = jnp.ones((LANES, LANES), BF16)
    heads = [slice(h * LANES, (h + 1) * LANES) for h in range(d_q // LANES)]
    splits = [_split_bf16(proj[:, lanes] * proj[:, lanes]) for lanes in heads]
    sums = [_dot(hi, ones) + _dot(lo, ones) for hi, lo in splits]
    for lanes, ss in zip(heads, sums):
        q_ref[:, lanes] = (proj[:, lanes] * lax.rsqrt(ss * (1.0 / B_HEAD_DIM) + NORM_EPS) * qgain_ref[:, lanes]
                           + qconst_ref[:, lanes]).astype(BF16)
    gate_ref[...] = _sigmoid(proj[:, d_q:])


def nsa_q_proj(h2d, g_norm, w_in, q_norm, *, tm=512):
    t_total, d = h2d.shape
    g, hpg, dh = B_KV_HEADS, B_GROUP, B_HEAD_DIM
    d_q = B_HEADS * LANES
    wq = jnp.pad(w_in[:, :B_HEADS * dh].reshape(d, B_HEADS, dh), ((0, 0), (0, 0), (0, LANES - dh))).reshape(d, d_q)
    wg = w_in[:, B_HEADS * dh:].reshape(d, g, hpg * 3)
    wg = jnp.pad(wg, ((0, 0), (0, 0), (0, LANES - hpg * 3))).reshape(d, g * LANES)
    w = jnp.concatenate([wq, wg], axis=1).astype(BF16)
    qgain = jnp.tile(jnp.pad(q_norm * (dh ** -0.5 * LOG2E), (0, LANES - dh)), B_HEADS).reshape(1, d_q)
    slopes = np.array([2.0 ** (-8.0 * (h + 1) / B_HEADS) for h in range(B_HEADS)], dtype=np.float64) * LOG2E
    feats = _alibi_query_part(jnp.asarray(slopes, F32)).astype(F32)
    qconst = jnp.pad(feats, ((0, 0), (dh, LANES - dh - feats.shape[1]))).reshape(1, d_q)
    const2 = lambda i: (0, 0)
    return pl.pallas_call(
        _q_proj_kernel,
        grid=(t_total // tm,),
        in_specs=[
            pl.BlockSpec((tm, d), lambda i: (i, 0)),
            _resident((1, d), const2),
            _resident((d, d_q + g * LANES), const2),
            _resident((1, d_q), const2),
            _resident((1, d_q), const2),
        ],
        out_specs=[pl.BlockSpec((tm, d_q), lambda i: (i, 0)), pl.BlockSpec((tm, g * LANES), lambda i: (i, 0))],
        out_shape=[jax.ShapeDtypeStruct((t_total, d_q), BF16), jax.ShapeDtypeStruct((t_total, g * LANES), F32)],
        compiler_params=_cparams(1),
        name="nsa_q_proj",
    )(h2d, g_norm.reshape(1, d), w, qgain, qconst)


def _nsa_attn_kernel(q_ref, gate_ref, kca_ref, vct_ref, ks_ref, vst_ref, kw_ref, vwt_ref,
                     ovlt_ref, eselt_ref, o_ref, live_smem, m_scr, acc_scr, *, tq, tk):
    hpg, dh = B_GROUP, B_HEAD_DIM
    t0 = pl.program_id(2) * tq
    rows = hpg * tq
    qb = q_ref[...]
    q_st = jnp.concatenate([qb[:, i * LANES:(i + 1) * LANES] for i in range(hpg)], axis=0)

    t_l = t0 + lax.broadcasted_iota(jnp.int32, (1, tq), 1)

    def normalised(acc):
        return acc[:dh] / acc[dh:2 * dh]

    span = WINDOW + tq
    w0 = pl.multiple_of(jnp.maximum(t0 - WINDOW, 0), LANES)
    dist = t_l - (w0 + lax.broadcasted_iota(jnp.int32, (span, 1), 0))
    pen = jnp.where((dist >= 0) & (dist < WINDOW), 0.0, NEG)
    s = _dot_nt(kw_ref[0, 0, pl.ds(w0, span), :], q_st) + jnp.concatenate([pen] * hpg, axis=1)
    m = jnp.max(s, axis=0, keepdims=True)
    o_win = normalised(_dot(vwt_ref[0, 0, :, pl.ds(w0, span)], jnp.exp2(s - m).astype(BF16)))

    n_half = kca_ref.shape[0]
    c_end = lax.broadcasted_iota(jnp.int32, (n_half, 1), 0) * CMP_STRIDE + (CMP_BLOCK - 1)
    t_lane = t0 + lax.broadcasted_iota(jnp.int32, (1, rows), 1) % tq
    seen = t_lane >= c_end
    s = jnp.where(seen, _dot_nt(kca_ref[...], q_st), NEG)
    m = jnp.max(s, axis=0, keepdims=True)
    e = jnp.where(seen, jnp.exp2(s - m), 0.0)
    p = e / jnp.maximum(jnp.sum(e, axis=0, keepdims=True), TINY)
    o_cmp = _dot(vct_ref[0], p.astype(BF16))[:dh]

    p_grp = p[:, 0:tq]
    for i in range(1, hpg):
        p_grp = p_grp + p[:, i * tq:(i + 1) * tq]
    p_hi, p_lo = _split_bf16(p_grp)
    n_blk = LANES // 2
    imp = (_dot(ovlt_ref[...], p_hi) + _dot(ovlt_ref[...], p_lo))[:n_blk]
    blk = lax.broadcasted_iota(jnp.int32, (n_blk, tq), 0)
    valid = blk * SEL_BLOCK <= t_l
    cur = t_l // SEL_BLOCK
    forced = valid & ((blk == 0) | (blk == cur) | (blk == cur - 1))
    work = jnp.where(forced, SEL_BIG, jnp.where(valid, imp, -SEL_BIG))
    blk_f = blk.astype(F32)
    chosen = jnp.zeros((n_blk, tq), F32)
    for _ in range(N_SELECT):
        top = jnp.max(work, axis=0, keepdims=True)
        first = jnp.min(jnp.where(work == top, blk_f, float(LANES)), axis=0, keepdims=True)
        hit = blk_f == first
        chosen = jnp.where(hit & (top > -0.5 * SEL_BIG), 1.0, chosen)
        work = jnp.where(hit, -3e38, work)
    blocks_per_tile = tk // SEL_BLOCK
    any_q = jnp.max(chosen, axis=1, keepdims=True)
    for kt in range(n_blk // blocks_per_tile):
        tile_any = jnp.max(any_q[kt * blocks_per_tile:(kt + 1) * blocks_per_tile])
        live_smem[kt] = (tile_any > 0.5).astype(jnp.int32)
    chosen = jnp.concatenate([chosen, jnp.zeros((LANES - n_blk, tq), F32)], axis=0).astype(BF16)

    m_scr[...] = jnp.full(m_scr.shape, NEG, F32)
    acc_scr[...] = jnp.zeros(acc_scr.shape, F32)

    def sel_step(kt, carry):
        @pl.when(live_smem[kt] != 0)
        def _():
            k0 = pl.multiple_of(kt * tk, tk)
            pos = k0 + lax.broadcasted_iota(jnp.int32, (tk, 1), 0)
            picked = _dot(eselt_ref[pl.ds(k0, tk), :], chosen)
            pen = jnp.where((picked > 0.5) & (pos <= t_l), 0.0, NEG)
            sc = _dot_nt(ks_ref[0, 0, pl.ds(k0, tk), :], q_st) + jnp.concatenate([pen] * hpg, axis=1)
            m_run = m_scr[...]
            m_new = jnp.maximum(m_run, jnp.broadcast_to(jnp.max(sc, axis=0, keepdims=True), m_run.shape))
            pr = jnp.exp2(sc - m_new[0:1]).astype(BF16)
            acc_scr[...] = (jnp.exp2(m_run - m_new)[0:1] * acc_scr[...]
                            + _dot(vst_ref[0, 0, :, pl.ds(k0, tk)], pr))
            m_scr[...] = m_new
        return carry

    lax.fori_loop(0, (t0 + tq + tk - 1) // tk, sel_step, 0)
    o_sel = normalised(acc_scr[...])

    gate_t = gate_ref[...].T
    outs = []
    for i in range(hpg):
        c = slice(i * tq, (i + 1) * tq)
        outs.append(gate_t[3 * i:3 * i + 1] * o_cmp[:, c] + gate_t[3 * i + 1:3 * i + 2] * o_sel[:, c]
                    + gate_t[3 * i + 2:3 * i + 3] * o_win[:, c])
    o_ref[...] = jnp.concatenate(outs, axis=0).T.astype(BF16)


def nsa_attention(q, gates, kca, vct, ks, vst, kw, vwt, bsz, seq, *, tq=256, tk=512):
    g, hpg, dh = B_KV_HEADS, B_GROUP, B_HEAD_DIM
    n_q = seq // tq
    n_half = seq // CMP_STRIDE
    n_cmp = n_half - 1
    n_sel = seq // SEL_BLOCK
    assert n_sel <= LANES // 2 and tq % LANES == 0
    c_start = np.arange(n_half) * CMP_STRIDE
    j_sel = np.arange(LANES)
    ovl = ((c_start[:, None] < (j_sel[None, :] + 1) * SEL_BLOCK)
           & (c_start[:, None] + CMP_BLOCK > j_sel[None, :] * SEL_BLOCK)
           & (np.arange(n_half)[:, None] < n_cmp) & (j_sel[None, :] < n_sel))
    esel = (np.arange(seq)[None, :] // SEL_BLOCK) == j_sel[:, None]
    row = lambda b, j, i: (b * n_q + i, j)
    per_bg = lambda b, j, i: (b, j, 0, 0)
    return pl.pallas_call(
        functools.partial(_nsa_attn_kernel, tq=tq, tk=tk),
        grid=(bsz, g, n_q),
        in_specs=[
            pl.BlockSpec((tq, hpg * LANES), row),
            pl.BlockSpec((tq, LANES), row),
            pl.BlockSpec((n_half, LANES), lambda b, j, i: (b * g + j, 0)),
            pl.BlockSpec((1, LANES, n_half), lambda b, j, i: (b * g + j, 0, 0)),
            pl.BlockSpec((1, 1, seq, LANES), per_bg),
            pl.BlockSpec((1, 1, LANES, seq), per_bg),
            pl.BlockSpec((1, 1, seq, LANES), per_bg),
            pl.BlockSpec((1, 1, LANES, seq), per_bg),
            _resident((LANES, n_half), lambda b, j, i: (0, 0)),
            _resident((seq, LANES), lambda b, j, i: (0, 0)),
        ],
        out_specs=pl.BlockSpec((tq, hpg * dh), row),
        out_shape=jax.ShapeDtypeStruct((bsz * seq, B_HEADS * dh), BF16),
        scratch_shapes=[pltpu.SMEM((seq // tk,), jnp.int32), pltpu.VMEM((SUBLANES, hpg * tq), F32),
                        pltpu.VMEM((LANES, hpg * tq), F32)],
        compiler_params=_cparams(3),
        name="nsa_attention",
    )(q, gates, kca, vct, ks, vst, kw, vwt, jnp.asarray(ovl.T, BF16), jnp.asarray(esel.T, BF16))


def _out_proj_kernel(h_ref, o_ref, w_ref, out_ref):
    out_ref[...] = h_ref[...] + _dot(o_ref[...], w_ref[...])


def out_proj_residual(h2d, o, w_out, *, tm=512):
    t_total, d = h2d.shape
    k = o.shape[1]
    return pl.pallas_call(
        _out_proj_kernel,
        grid=(t_total // tm,),
        in_specs=[pl.BlockSpec((tm, d), lambda i: (i, 0)), pl.BlockSpec((tm, k), lambda i: (i, 0)),
                  _resident((k, d), lambda i: (0, 0))],
        out_specs=pl.BlockSpec((tm, d), lambda i: (i, 0)),
        out_shape=jax.ShapeDtypeStruct((t_total, d), F32),
        compiler_params=_cparams(1),
        name="out_proj_residual",
    )(h2d, o, w_out.astype(BF16))


def nsa_shared_kv(h2d, bsz, seq, kv_norm, kv_w, cmp_pos_k, cmp_w1_k, cmp_w2_k, cmp_pos_v, cmp_w1_v, cmp_w2_v, k_norm):
    ks, kw, vst, vwt, kc_raw, vc_raw = nsa_kv_proj(h2d, bsz, seq, kv_norm, kv_w, k_norm)
    kca, vct = nsa_compress(kc_raw, vc_raw, bsz, seq, cmp_pos_k, cmp_w1_k, cmp_w2_k,
                            cmp_pos_v, cmp_w1_v, cmp_w2_v, k_norm[0])
    return kca, vct, ks, vst, kw, vwt


def nsa_mixer(h2d, bsz, seq, g_norm, w_in, q_norm, kv):
    q, gates = nsa_q_proj(h2d, g_norm, w_in, q_norm)
    return nsa_attention(q, gates, *kv, bsz, seq)


def kernel(x, norm_mix, norm_ffn, hgrn_w_in, hgrn_w_out, hgrn_g_norm, hgrn_lb_logits, kv_norm, kv_w, cmp_pos_k, cmp_w1_k, cmp_w2_k, cmp_pos_v, cmp_w1_v, cmp_w2_v, k_norm, nsa_w_in, nsa_w_out, nsa_q_norm, ffn_w_gu, ffn_w_down, moe_router, moe_w_gu, moe_w_down):
    bsz, seq, d = x.shape
    lb = jnp.cumsum(jax.nn.softmax(hgrn_lb_logits.astype(F32), axis=0), axis=0)
    lb = lb - lb[0:1]
    n_a = hgrn_w_in.shape[0]
    depth = norm_mix.shape[0]
    h = x.reshape(bsz * seq, d)
    kv = None
    for layer in range(depth):
        mixer_out = None
        if layer < n_a:
            h = hgrn_mixer(h, seq, norm_mix[layer], hgrn_w_in[layer], hgrn_w_out[layer], hgrn_g_norm[layer],
                           lb[layer])
        else:
            if kv is None:
                kv = nsa_shared_kv(h, bsz, seq, kv_norm, kv_w, cmp_pos_k, cmp_w1_k, cmp_w2_k,
                                   cmp_pos_v, cmp_w1_v, cmp_w2_v, k_norm)
            j = layer - n_a
            mixer_out = (nsa_mixer(h, bsz, seq, norm_mix[layer], nsa_w_in[j], nsa_q_norm[j], kv), nsa_w_out[j])
            if layer % 2 == 1:
                h = out_proj_residual(h, *mixer_out)
                mixer_out = None
        if layer % 2 == 0:
            h = ffn_swiglu(h, norm_ffn[layer], ffn_w_gu[layer // 2], ffn_w_down[layer // 2], mixer_out=mixer_out)
        else:
            h = moe_swiglu_top2(h, norm_ffn[layer], moe_router[layer // 2], moe_w_gu[layer // 2],
                                moe_w_down[layer // 2])
    return h.reshape(bsz, seq, d)
```

```python
import functools

import numpy as np
import jax
import jax.numpy as jnp
from jax import lax
from jax.experimental import pallas as pl
from jax.experimental.pallas import tpu as pltpu

F32 = jnp.float32
BF16 = jnp.bfloat16

NORM_EPS = 1e-6
VMEM_LIMIT_BYTES = 56 * 1024 * 1024

A_HEADS = 8
A_HEAD_DIM = 128
SCAN_ROWS = 128
SUBLANES = 8


def _cparams(n_axes):
    return pltpu.CompilerParams(dimension_semantics=("arbitrary",) * n_axes,
                                vmem_limit_bytes=VMEM_LIMIT_BYTES)


def _dot(a, b):
    return jnp.dot(a, b, preferred_element_type=F32)


def _dot_nt(a, b):
    return lax.dot_general(a, b, (((1,), (1,)), ((), ())), preferred_element_type=F32)


def _dot_tn(a, b):
    return lax.dot_general(a, b, (((0,), (0,)), ((), ())), preferred_element_type=F32)


def _sigmoid(x):
    return 0.5 * jnp.tanh(0.5 * x) + 0.5


def _rms(x, g):
    return x * lax.rsqrt(jnp.mean(x * x, axis=-1, keepdims=True) + NORM_EPS) * g


def _split_bf16(x):
    hi = x.astype(BF16)
    lo = (x - hi.astype(F32)).astype(BF16)
    return hi, lo


def _hgrn_constants():
    n = SCAN_ROWS
    t = np.arange(n)
    r = np.arange(n)[None, :]
    sums, masks = [], []
    m = n // 2
    while m >= 1:
        grp = t // (2 * m)
        mid = grp * 2 * m + m
        upper = (t % (2 * m)) >= m
        if m < SUBLANES:
            sums.append(np.where(upper[:, None], (r >= mid[:, None]) & (r <= t[:, None]),
                                 (r > t[:, None]) & (r < mid[:, None])))
        masks.append((grp[:, None] == grp[None, :]) & upper[:, None] & (~upper[None, :]))
        m //= 2
    masks.append(np.eye(n, dtype=bool))
    sums.append(r <= t[:, None])
    w = np.concatenate(sums, 0).astype(np.float32)
    return np.concatenate([w, w], 1), np.stack(masks).astype(np.float32)


def _hgrn_kernel(x_ref, gn_ref, win_ref, lb_ref, ghn_ref, wout_ref, wsum_ref, msk_ref,
                 out_ref, xn_scr, st_scr, og_scr, *, tm):
    sblk = pl.program_id(1)
    h = pl.program_id(2)
    n = SCAN_ROWS
    n_lvl = msk_ref.shape[0] - 1

    @pl.when(h == 0)
    def _():
        xn_scr[...] = _rms(x_ref[...], gn_ref[...]).astype(BF16)

    @pl.when(sblk == 0)
    def _():
        st_scr[h] = jnp.zeros((A_HEAD_DIM, A_HEAD_DIM), F32)

    proj = _dot(xn_scr[...], win_ref[...])
    q = proj[:, 0:128]
    f = proj[:, 128:256]
    v = proj[:, 256:384]
    g = proj[:, 384:512]
    lb = lb_ref[0:1, :]
    log_lb = lb_ref[1:2, :]
    log_1m_lb = lb_ref[2:3, :]
    qs = q * _sigmoid(q)
    a = jnp.exp(-jnp.abs(f))
    inv = 1.0 / (1.0 + a)
    log_sig = jnp.minimum(f, 0.0) - jnp.log1p(a)
    b = log_1m_lb + log_sig
    logf = jnp.maximum(log_lb, b) + jnp.log1p(jnp.exp(-jnp.abs(log_lb - b)))
    key = (1.0 - lb) * jnp.where(f >= 0.0, a * inv, inv)

    wide = [n >> (l + 1) for l in range(n_lvl) if n >> (l + 1) >= SUBLANES]
    row_id = lax.broadcasted_iota(jnp.int32, (n, A_HEAD_DIM), 0)
    in_upper_half = [(row_id % (2 * m)) >= m for m in wide]

    blocks = [slice(r * n, (r + 1) * n) for r in range(tm // n)]
    n_narrow = n_lvl - len(wide)
    sums = []
    for rows in blocks:
        hi, lo = _split_bf16(logf[rows])
        sums.append(_dot(wsum_ref[...], jnp.concatenate([hi, lo], axis=0)))
    e_lvl, e_prefix, e_suffix = [], [], []
    for sm in sums:
        cum = sm[n_narrow * n:]
        e_blk = []
        for m, upper in zip(wide, in_upper_half):
            bound = jnp.concatenate([jnp.broadcast_to(cum[i + m - 1:i + m], (2 * m, A_HEAD_DIM))
                                     for i in range(0, n, 2 * m)], axis=0)
            e_blk.append(jnp.exp(jnp.where(upper, cum - bound, bound - cum)))
        e_sm = jnp.exp(sm)
        e_lvl.append(e_blk + [e_sm[i * n:(i + 1) * n] for i in range(n_narrow)])
        e_prefix.append(e_sm[n_narrow * n:])
        e_suffix.append(jnp.exp(cum[n - 1:n] - cum))
    intra, inter_q, kv = [], [], []
    for rows, e_blk, e_pre, e_suf in zip(blocks, e_lvl, e_prefix, e_suffix):
        qb = qs[rows]
        kb = key[rows]
        vb = v[rows].astype(BF16)
        s = msk_ref[n_lvl] * _dot_nt(qb.astype(BF16), kb.astype(BF16))
        for l in range(n_lvl):
            s = s + msk_ref[l] * _dot_nt((qb * e_blk[l]).astype(BF16), (kb * e_blk[l]).astype(BF16))
        intra.append(_dot(s.astype(BF16), vb))
        inter_q.append((qb * e_pre).astype(BF16))
        kv.append(_dot_tn(vb, (kb * e_suf).astype(BF16)))
    st = st_scr[h]
    outs = []
    for o_intra, q_pre, e_pre, kv_blk in zip(intra, inter_q, e_prefix, kv):
        outs.append(o_intra + _dot_nt(q_pre, st.astype(BF16)))
        st = st * e_pre[n - 1:n, :] + kv_blk
    st_scr[h] = st
    o = jnp.concatenate(outs, axis=0)

    og_scr[:, pl.ds(pl.multiple_of(h * A_HEAD_DIM, A_HEAD_DIM), A_HEAD_DIM)] = (
        _rms(o, ghn_ref[...]) * (g * _sigmoid(g))).astype(BF16)

    @pl.when(h == A_HEADS - 1)
    def _():
        out_ref[...] = x_ref[...] + _dot(og_scr[...], wout_ref[...])


def hgrn_mixer(h2d, seq, g_norm_in, w_in, w_out, g_head, lb, *, tm=1024):
    t_total, d = h2d.shape
    n_sblk = seq // tm
    hd = A_HEAD_DIM
    w_perm = w_in.reshape(d, 4, A_HEADS, hd).transpose(0, 2, 1, 3).reshape(d, 4 * A_HEADS * hd).astype(BF16)
    lb_rows = jnp.zeros((8, A_HEADS * hd), F32)
    lb_rows = lb_rows.at[0].set(lb).at[1].set(jnp.log(lb)).at[2].set(jnp.log1p(-lb))
    wsum, masks = _hgrn_constants()
    n_blk = wsum.shape[0]
    grid = (t_total // seq, n_sblk, A_HEADS)
    row = lambda b, s, h: (b * n_sblk + s, 0)
    const2 = lambda b, s, h: (0, 0)
    return pl.pallas_call(
        functools.partial(_hgrn_kernel, tm=tm),
        grid=grid,
        in_specs=[
            pl.BlockSpec((tm, d), row),
            pl.BlockSpec((1, d), const2),
            pl.BlockSpec((d, 4 * hd), lambda b, s, h: (0, h)),
            pl.BlockSpec((8, hd), lambda b, s, h: (0, h)),
            pl.BlockSpec((1, hd), const2),
            _resident((A_HEADS * hd, d), const2),
            _resident((n_blk, 2 * SCAN_ROWS), const2),
            _resident(masks.shape, lambda b, s, h: (0, 0, 0)),
        ],
        out_specs=pl.BlockSpec((tm, d), row),
        out_shape=jax.ShapeDtypeStruct((t_total, d), F32),
        scratch_shapes=[pltpu.VMEM((tm, d), BF16), pltpu.VMEM((A_HEADS, hd, hd), F32),
                        pltpu.VMEM((tm, A_HEADS * hd), BF16)],
        compiler_params=_cparams(3),
        name="hgrn_mixer",
    )(h2d, g_norm_in.reshape(1, d), w_perm, lb_rows, g_head.reshape(1, hd), w_out.astype(BF16),
      jnp.asarray(wsum, BF16), jnp.asarray(masks, F32))


def _resident(block_shape, index_map):
    return pl.BlockSpec(block_shape, index_map, pipeline_mode=pl.Buffered(1))


MXU_TILE = 256


def _swiglu(xn, wg_ref, wu_ref, wd_ref):
    acc = None
    for c in range(wg_ref.shape[1] // MXU_TILE):
        cols = slice(c * MXU_TILE, (c + 1) * MXU_TILE)
        gate = _dot(xn, wg_ref[:, cols])
        up = _dot(xn, wu_ref[:, cols])
        part = _dot((gate * _sigmoid(gate) * up).astype(BF16), wd_ref[cols, :])
        acc = part if acc is None else acc + part
    return acc


def _ffn_kernel(*refs, with_mixer_out):
    if with_mixer_out:
        x_ref, o_ref, wo_ref, gn_ref, wg_ref, wu_ref, wd_ref, out_ref = refs
        x = x_ref[...] + _dot(o_ref[...], wo_ref[...])
    else:
        x_ref, gn_ref, wg_ref, wu_ref, wd_ref, out_ref = refs
        x = x_ref[...]
    xn = _rms(x, gn_ref[...]).astype(BF16)
    out_ref[...] = x + _swiglu(xn, wg_ref, wu_ref, wd_ref)


def ffn_swiglu(h2d, g_norm, w_gu, w_down, *, mixer_out=None, tm=512):
    t_total, d = h2d.shape
    ff = w_down.shape[0]
    w_gu = w_gu.astype(BF16)
    const2 = lambda i: (0, 0)
    rows = lambda i: (i, 0)
    mixer_specs, mixer_args = [], []
    if mixer_out is not None:
        o, w_out = mixer_out
        mixer_specs = [pl.BlockSpec((tm, o.shape[1]), rows), _resident(w_out.shape, const2)]
        mixer_args = [o, w_out.astype(BF16)]
    return pl.pallas_call(
        functools.partial(_ffn_kernel, with_mixer_out=mixer_out is not None),
        grid=(t_total // tm,),
        in_specs=[pl.BlockSpec((tm, d), rows)] + mixer_specs + [
            _resident((1, d), const2),
            _resident((d, ff), const2),
            _resident((d, ff), lambda i: (0, 1)),
            _resident((ff, d), const2),
        ],
        out_specs=pl.BlockSpec((tm, d), rows),
        out_shape=jax.ShapeDtypeStruct((t_total, d), F32),
        compiler_params=_cparams(1),
        name="ffn_swiglu",
    )(h2d, *mixer_args, g_norm.reshape(1, d), w_gu, w_gu, w_down.astype(BF16))


N_EXPERTS = 8
LANES = 128
NEG = -1e30


def _router_kernel(x_ref, gn_ref, wh_ref, wl_ref, sel_ref):
    xn = _rms(x_ref[...], gn_ref[...])
    xh, xl = _split_bf16(xn)
    logits = _dot(xh, wh_ref[...]) + (_dot(xl, wh_ref[...]) + _dot(xh, wl_ref[...]))
    lane = lax.broadcasted_iota(jnp.int32, logits.shape, 1).astype(F32)
    logits = jnp.where(lane < N_EXPERTS, logits, NEG)
    m1 = jnp.max(logits, axis=-1, keepdims=True)
    i1 = jnp.min(jnp.where(logits == m1, lane, float(LANES)), axis=-1, keepdims=True)
    rest = jnp.where(lane == i1, NEG, logits)
    m2 = jnp.max(rest, axis=-1, keepdims=True)
    i2 = jnp.min(jnp.where(rest == m2, lane, float(LANES)), axis=-1, keepdims=True)
    e2 = jnp.exp(m2 - m1)
    denom = 1.0 + e2
    sel_ref[...] = jnp.where(lane == 0, i1, jnp.where(lane == 1, i2, jnp.where(
        lane == 2, 1.0 / denom, jnp.where(lane == 3, e2 / denom, 0.0))))


def moe_route(h2d, g_norm, w_router, *, tm=512):
    t_total, d = h2d.shape
    w_pad = jnp.zeros((d, LANES), F32).at[:, :N_EXPERTS].set(w_router)
    w_hi = w_pad.astype(BF16)
    w_lo = (w_pad - w_hi.astype(F32)).astype(BF16)
    const2 = lambda i: (0, 0)
    return pl.pallas_call(
        _router_kernel,
        grid=(t_total // tm,),
        in_specs=[
            pl.BlockSpec((tm, d), lambda i: (i, 0)),
            _resident((1, d), const2),
            _resident((d, LANES), const2),
            _resident((d, LANES), const2),
        ],
        out_specs=pl.BlockSpec((tm, LANES), lambda i: (i, 0)),
        out_shape=jax.ShapeDtypeStruct((t_total, LANES), F32),
        compiler_params=_cparams(1),
        name="moe_router",
    )(h2d, g_norm.reshape(1, d), w_hi, w_lo)


MOE_TILE = 1024
GATHER_UNROLL = 8


def _moe_plan(sel, tm):
    t_total = sel.shape[0]
    e = sel[:, :2].astype(jnp.int32)
    onehot = (e[:, :, None] == jnp.arange(N_EXPERTS, dtype=jnp.int32)[None, None, :]).astype(jnp.int32).sum(1)
    csum = jnp.cumsum(onehot, axis=0)
    rank = csum - onehot
    padded = (csum[-1] + tm - 1) // tm * tm
    ends = jnp.cumsum(padded)
    pos = (ends - padded)[e] + jnp.take_along_axis(rank, e, axis=1)
    n_slots = 2 * t_total + N_EXPERTS * tm
    tile_start = jnp.arange(n_slots // tm, dtype=jnp.int32) * tm
    tile_expert = jnp.minimum(jnp.searchsorted(ends, tile_start, side="right"), N_EXPERTS - 1).astype(jnp.int32)
    tile_used = (tile_start < ends[-1]).astype(jnp.int32)
    n_used = (ends[-1:] // tm).astype(jnp.int32)
    last_tile = jnp.maximum(ends - tm, 0).astype(jnp.int32)
    return n_slots, tile_expert, tile_used, n_used, last_tile, (padded > 0).astype(jnp.int32), pos[:, 0], pos[:, 1]


def _load_indices(idx_hbm, idx_smem, sem, tile, tm):
    return pltpu.make_async_copy(idx_hbm.at[pl.ds(tile * tm, tm)], idx_smem, sem)


def _moe_dispatch_kernel(last_tile_ref, has_rows_ref, n_used_ref, pa_hbm, pb_hbm, h_ref, xs_hbm, ia_smem, ib_smem,
                         zero_scr, row_sem, idx_sem, zero_sem, *, tm):
    i = pl.program_id(0)

    @pl.when(i == 0)
    def _():
        zero_scr[...] = jnp.zeros(zero_scr.shape, F32)
        n_tiles = xs_hbm.shape[0] // tm
        for e in range(N_EXPERTS):
            spare = n_used_ref[0] + e
            for start, wanted in ((last_tile_ref[e], has_rows_ref[e] != 0), (spare * tm, spare < n_tiles)):
                fill = pltpu.make_async_copy(zero_scr, xs_hbm.at[pl.ds(pl.multiple_of(start, tm), tm), :],
                                             zero_sem.at[e])

                @pl.when(wanted)
                def _():
                    fill.start()
                    fill.wait()

    ca = _load_indices(pa_hbm, ia_smem, idx_sem.at[0], i, tm)
    cb = _load_indices(pb_hbm, ib_smem, idx_sem.at[1], i, tm)
    ca.start()
    cb.start()
    ca.wait()
    cb.wait()

    def body(r, carry):
        src = h_ref.at[pl.ds(r, 1), :]
        pltpu.make_async_copy(src, xs_hbm.at[pl.ds(ia_smem[r], 1), :], row_sem.at[0]).start()
        pltpu.make_async_copy(src, xs_hbm.at[pl.ds(ib_smem[r], 1), :], row_sem.at[1]).start()
        return carry
    lax.fori_loop(0, tm, body, 0, unroll=GATHER_UNROLL)
    pltpu.make_async_copy(h_ref, xs_hbm.at[pl.ds(0, tm), :], row_sem.at[0]).wait()
    pltpu.make_async_copy(h_ref, xs_hbm.at[pl.ds(0, tm), :], row_sem.at[1]).wait()


def _moe_group_kernel(te_ref, used_ref, xs_ref, gn_ref, wg_ref, wu_ref, wd_ref, y_ref, xn_scr):
    i = pl.program_id(0)
    j = pl.program_id(1)

    @pl.when(used_ref[i] == 0)
    def _():
        y_ref[...] = jnp.zeros(y_ref.shape, F32)

    @pl.when(used_ref[i] != 0)
    def _():
        @pl.when(j == 0)
        def _():
            xn_scr[...] = _rms(xs_ref[...], gn_ref[...]).astype(BF16)

        contrib = _swiglu(xn_scr[...], wg_ref.at[0], wu_ref.at[0], wd_ref.at[0])

        @pl.when(j == 0)
        def _():
            y_ref[...] = contrib

        @pl.when(j > 0)
        def _():
            y_ref[...] += contrib


def _moe_combine_kernel(pa_hbm, pb_hbm, y_hbm, h_ref, sel_ref, out_ref, ia_smem, ib_smem, buf_a, buf_b,
                        row_sem, idx_sem, *, tm):
    i = pl.program_id(0)
    ca = _load_indices(pa_hbm, ia_smem, idx_sem.at[0], i, tm)
    cb = _load_indices(pb_hbm, ib_smem, idx_sem.at[1], i, tm)
    ca.start()
    cb.start()
    ca.wait()
    cb.wait()

    def body(r, carry):
        pltpu.make_async_copy(y_hbm.at[pl.ds(ia_smem[r], 1), :], buf_a.at[pl.ds(r, 1), :], row_sem.at[0]).start()
        pltpu.make_async_copy(y_hbm.at[pl.ds(ib_smem[r], 1), :], buf_b.at[pl.ds(r, 1), :], row_sem.at[1]).start()
        return carry
    lax.fori_loop(0, tm, body, 0, unroll=GATHER_UNROLL)
    pltpu.make_async_copy(buf_a, buf_a, row_sem.at[0]).wait()
    pltpu.make_async_copy(buf_b, buf_b, row_sem.at[1]).wait()
    sel = sel_ref[...]
    out_ref[...] = h_ref[...] + (sel[:, 2:3] * buf_a[...] + sel[:, 3:4] * buf_b[...])


def moe_swiglu_top2(h2d, g_norm, w_router, w_gu, w_down, *, n_chunks=2):
    t_total, d = h2d.shape
    tm = MOE_TILE
    ff = w_down.shape[1]
    tf = ff // n_chunks
    sel = moe_route(h2d, g_norm, w_router)
    n_slots, tile_expert, tile_used, n_used, last_tile, has_rows, pos_a, pos_b = _moe_plan(sel, tm)
    any_spec = pl.BlockSpec(memory_space=pl.ANY)
    index_scratch = [pltpu.SMEM((tm,), jnp.int32), pltpu.SMEM((tm,), jnp.int32)]
    sems = [pltpu.SemaphoreType.DMA((2,)), pltpu.SemaphoreType.DMA((2,))]
    xs = pl.pallas_call(
        functools.partial(_moe_dispatch_kernel, tm=tm),
        grid_spec=pltpu.PrefetchScalarGridSpec(
            num_scalar_prefetch=3,
            grid=(t_total // tm,),
            in_specs=[any_spec, any_spec, pl.BlockSpec((tm, d), lambda i, lt, hr, nu: (i, 0))],
            out_specs=any_spec,
            scratch_shapes=index_scratch + [pltpu.VMEM((tm, d), F32)] + sems
            + [pltpu.SemaphoreType.DMA((N_EXPERTS,))],
        ),
        out_shape=jax.ShapeDtypeStruct((n_slots, d), F32),
        compiler_params=_cparams(1),
        name="moe_dispatch",
    )(last_tile, has_rows, n_used, pos_a, pos_b, h2d)
    w_gu = w_gu.astype(BF16)
    row = lambda i, j, te, used: (i, 0)
    y = pl.pallas_call(
        _moe_group_kernel,
        grid_spec=pltpu.PrefetchScalarGridSpec(
            num_scalar_prefetch=2,
            grid=(n_slots // tm, n_chunks),
            in_specs=[
                pl.BlockSpec((tm, d), row),
                pl.BlockSpec((1, d), lambda i, j, te, used: (0, 0)),
                pl.BlockSpec((1, d, tf), lambda i, j, te, used: (te[i], 0, j)),
                pl.BlockSpec((1, d, tf), lambda i, j, te, used: (te[i], 0, j + n_chunks)),
                pl.BlockSpec((1, tf, d), lambda i, j, te, used: (te[i], j, 0)),
            ],
            out_specs=pl.BlockSpec((tm, d), row),
            scratch_shapes=[pltpu.VMEM((tm, d), BF16)],
        ),
        out_shape=jax.ShapeDtypeStruct((n_slots, d), F32),
        compiler_params=_cparams(2),
        name="moe_group_ffn",
    )(tile_expert, tile_used, xs, g_norm.reshape(1, d), w_gu, w_gu, w_down.astype(BF16))
    return pl.pallas_call(
        functools.partial(_moe_combine_kernel, tm=tm),
        grid=(t_total // tm,),
        in_specs=[any_spec, any_spec, any_spec, pl.BlockSpec((tm, d), lambda i: (i, 0)),
                  pl.BlockSpec((tm, LANES), lambda i: (i, 0))],
        out_specs=pl.BlockSpec((tm, d), lambda i: (i, 0)),
        out_shape=jax.ShapeDtypeStruct((t_total, d), F32),
        scratch_shapes=index_scratch + [pltpu.VMEM((tm, d), F32), pltpu.VMEM((tm, d), F32)] + sems,
        compiler_params=_cparams(1),
        name="moe_combine",
    )(pos_a, pos_b, y, h2d, sel)


B_HEADS = 16
B_KV_HEADS = 4
B_GROUP = 4
B_HEAD_DIM = 64
CMP_STRIDE = 16
CMP_BLOCK = 32
SEL_BLOCK = 32
N_SELECT = 8
WINDOW = 512
SEL_BIG = 1e9
TINY = 1e-30
N_KV_KINDS = 6

LOG2E = 1.4426950408889634
POS_RADIX = 64
N_SLOPE_PARTS = 3


def _vector_in_upper_lanes(kv_group):
    return kv_group >= B_KV_HEADS // 2


def _alibi_key_part(pos, idx):
    hi = (pos // POS_RADIX * POS_RADIX).astype(F32)
    lo = (pos % POS_RADIX).astype(F32)
    return jnp.where(idx < N_SLOPE_PARTS, hi, jnp.where(idx < 2 * N_SLOPE_PARTS, lo, 0.0))


def _alibi_query_part(slopes):
    parts, rest = [], slopes.astype(F32)
    for _ in range(N_SLOPE_PARTS):
        p = rest.astype(BF16)
        parts.append(p)
        rest = rest - p.astype(F32)
    return jnp.stack(parts + parts, axis=-1)


def _kv_proj_kernel(x_ref, gn_ref, wk_ref, kgain_ref, wvt_ref, wc_ref,
                    ks_ref, kw_ref, vst_ref, vwt_ref, kc_ref, vc_ref):
    g = B_KV_HEADS
    hn = _rms(x_ref[...], gn_ref[...]).astype(BF16)
    k = _dot(hn, wk_ref[...])
    tm = k.shape[0]
    pos = pl.program_id(1) * tm + lax.broadcasted_iota(jnp.int32, (tm, LANES), 0)
    lane = lax.broadcasted_iota(jnp.int32, (tm, LANES), 1)
    feats = _alibi_key_part(pos, lane % B_HEAD_DIM)
    for j in range(2 * g):
        kj = k[:, j * LANES:(j + 1) * LANES]
        ms = jnp.sum(kj * kj, axis=-1, keepdims=True) * (1.0 / B_HEAD_DIM)
        pos_lanes = jnp.where((lane >= B_HEAD_DIM) == _vector_in_upper_lanes(j % g), 0.0, feats)
        kj = (kj * lax.rsqrt(ms + NORM_EPS) * kgain_ref[:, j * LANES:(j + 1) * LANES] + pos_lanes).astype(BF16)
        if j < g:
            ks_ref[0, j] = kj
        else:
            kw_ref[0, j - g] = kj
    vt = _dot_nt(wvt_ref[...], hn)
    row = lax.broadcasted_iota(jnp.int32, vt.shape, 0)
    vt = jnp.where(row % LANES >= B_HEAD_DIM, 1.0, vt).astype(BF16)
    for j in range(g):
        vst_ref[0, j] = vt[j * LANES:(j + 1) * LANES]
        vwt_ref[0, j] = vt[(g + j) * LANES:(g + j + 1) * LANES]
    c = _dot(hn, wc_ref[...])
    half = g * B_HEAD_DIM
    kc_ref[...] = c[:, :half].astype(BF16)
    vc_ref[...] = c[:, half:].astype(BF16)


def nsa_kv_proj(h2d, bsz, seq, kv_norm, kv_w, k_norm, *, tm=512):
    t_total, d = h2d.shape
    g, dh = B_KV_HEADS, B_HEAD_DIM
    n_sblk = seq // tm
    w = kv_w.reshape(d, N_KV_KINDS, g, dh)
    def head_slabs(kinds):
        cols = jnp.stack([w[:, kind] for kind in kinds], axis=1)
        return jnp.pad(cols, ((0, 0), (0, 0), (0, 0), (0, LANES - dh))).reshape(d, len(kinds) * g * LANES)

    def key_slabs(a):
        lead = [(0, 0)] * (a.ndim - 2)
        first, second = a[..., :g // 2, :], a[..., g // 2:, :]
        both = jnp.concatenate([jnp.pad(first, lead + [(0, 0), (0, LANES - dh)]),
                                jnp.pad(second, lead + [(0, 0), (LANES - dh, 0)])], axis=-2)
        return both.reshape(a.shape[:-2] + (g * LANES,))

    assert [_vector_in_upper_lanes(j) for j in range(g)] == [j >= g // 2 for j in range(g)]
    wk = key_slabs(jnp.stack([w[:, 2], w[:, 4]], axis=1)).reshape(d, 2 * g * LANES).astype(BF16)
    wvt = head_slabs((3, 5)).T.astype(BF16)
    kgain = key_slabs(jnp.stack([jnp.tile(k_norm[1], (g, 1)), jnp.tile(k_norm[2], (g, 1))])).reshape(1, 2 * g * LANES)
    wc = jnp.concatenate([w[:, 0].reshape(d, g * dh), w[:, 1].reshape(d, g * dh)], axis=1).astype(BF16)
    const2 = lambda b, s: (0, 0)
    row = lambda b, s: (b * n_sblk + s, 0)
    k_spec = pl.BlockSpec((1, g, tm, LANES), lambda b, s: (b, 0, s, 0))
    vt_spec = pl.BlockSpec((1, g, LANES, tm), lambda b, s: (b, 0, 0, s))
    k_shape = jax.ShapeDtypeStruct((bsz, g, seq, LANES), BF16)
    vt_shape = jax.ShapeDtypeStruct((bsz, g, LANES, seq), BF16)
    c_shape = jax.ShapeDtypeStruct((t_total, g * dh), BF16)
    return pl.pallas_call(
        _kv_proj_kernel,
        grid=(bsz, n_sblk),
        in_specs=[
            pl.BlockSpec((tm, d), row),
            _resident((1, d), const2),
            _resident((d, 2 * g * LANES), const2),
            _resident((1, 2 * g * LANES), const2),
            _resident((2 * g * LANES, d), const2),
            _resident((d, 2 * g * dh), const2),
        ],
        out_specs=[k_spec, k_spec, vt_spec, vt_spec,
                   pl.BlockSpec((tm, g * dh), row), pl.BlockSpec((tm, g * dh), row)],
        out_shape=[k_shape, k_shape, vt_shape, vt_shape, c_shape, c_shape],
        compiler_params=_cparams(2),
        name="nsa_kv_proj",
    )(h2d, kv_norm.reshape(1, d), wk, kgain, wvt, wc)


def _compress_kernel(ak_ref, av_ref, posk_ref, posv_ref, w1k_ref, w1v_ref, w2k_ref, w2vt_ref, kgain_ref,
                     kca_ref, vct_ref, *, n_half):
    half_w = ak_ref.shape[1]

    def hidden(a_ref, pos_ref, w1_ref):
        a = a_ref[...]
        top = _dot(a, w1_ref[:half_w, :])
        bot = _dot(a, w1_ref[half_w:, :])
        bias = _dot(pos_ref[...], w1_ref[...])[0:1, :]
        pre = top + pltpu.roll(bot, shift=a.shape[0] - 1, axis=0) + bias
        return (pre * _sigmoid(pre)).astype(BF16)

    act_k = hidden(ak_ref, posk_ref, w1k_ref)
    kc = _dot(act_k, w2k_ref[...])
    row = lax.broadcasted_iota(jnp.int32, kc.shape, 0)
    lane = lax.broadcasted_iota(jnp.int32, kc.shape, 1)
    own_half = (lane >= B_HEAD_DIM) == _vector_in_upper_lanes((row // n_half) % B_KV_HEADS)
    kc = jnp.where(own_half, kc, 0.0)
    ms = jnp.sum(kc * kc, axis=-1, keepdims=True) * (1.0 / B_HEAD_DIM)
    kc = kc * lax.rsqrt(ms + NORM_EPS) * kgain_ref[...]
    c_end = row % n_half * CMP_STRIDE + (CMP_BLOCK - 1)
    kca_ref[...] = (kc + jnp.where(own_half, 0.0, _alibi_key_part(c_end, lane % B_HEAD_DIM))).astype(BF16)
    act_v = hidden(av_ref, posv_ref, w1v_ref)
    vct = _dot_nt(w2vt_ref[...], act_v).astype(BF16)
    for j in range(vct.shape[1] // n_half):
        vct_ref[j] = vct[:, j * n_half:(j + 1) * n_half]


def nsa_compress(kc_raw, vc_raw, bsz, seq, pos_k, w1_k, w2_k, pos_v, w1_v, w2_v, k_gain, *, groups_per_step=4):
    g, dh = B_KV_HEADS, B_HEAD_DIM
    n_half = seq // CMP_STRIDE
    feat = CMP_STRIDE * dh

    def to_half_blocks(a):
        a = a.reshape(bsz, n_half, CMP_STRIDE, g, dh).transpose(0, 3, 1, 2, 4)
        return a.reshape(bsz * g * n_half, feat)

    def pos_rows(pos):
        return jnp.zeros((8, 2 * feat), F32).at[0].set(pos.reshape(-1)).astype(BF16)

    hid = w1_k.shape[1]
    assert groups_per_step % g == 0
    pad_lanes = lambda a: jnp.pad(a, ((0, 0), (0, LANES - dh)))
    rows = groups_per_step * n_half
    const2 = lambda i: (0, 0)
    return pl.pallas_call(
        functools.partial(_compress_kernel, n_half=n_half),
        grid=(bsz * g // groups_per_step,),
        in_specs=[
            pl.BlockSpec((rows, feat), lambda i: (i, 0)),
            pl.BlockSpec((rows, feat), lambda i: (i, 0)),
            _resident((8, 2 * feat), const2),
            _resident((8, 2 * feat), const2),
            _resident((2 * feat, hid), const2),
            _resident((2 * feat, hid), const2),
            _resident((hid, LANES), const2),
            _resident((LANES, hid), const2),
            _resident((1, LANES), const2),
        ],
        out_specs=[pl.BlockSpec((rows, LANES), lambda i: (i, 0)),
                   pl.BlockSpec((groups_per_step, LANES, n_half), lambda i: (i, 0, 0))],
        out_shape=[jax.ShapeDtypeStruct((bsz * g * n_half, LANES), BF16),
                   jax.ShapeDtypeStruct((bsz * g, LANES, n_half), BF16)],
        compiler_params=_cparams(1),
        name="nsa_compress",
    )(to_half_blocks(kc_raw), to_half_blocks(vc_raw), pos_rows(pos_k), pos_rows(pos_v),
      w1_k.astype(BF16), w1_v.astype(BF16), jnp.tile(w2_k, (1, LANES // dh)).astype(BF16),
      pad_lanes(w2_v).T.astype(BF16), jnp.tile(k_gain, LANES // dh).reshape(1, LANES))


def _q_proj_kernel(x_ref, gn_ref, w_ref, qgain_ref, qconst_ref, q_ref, gate_ref):
    n_pairs = qgain_ref.shape[1] // LANES
    xn = _rms(x_ref[...], gn_ref[...]).astype(BF16)
    proj = _dot(xn, w_ref[...])
    half_of = lambda shape, axis: lax.broadcasted_iota(jnp.int32, shape, axis) >= B_HEAD_DIM
    same_half = jnp.where(half_of((LANES, LANES), 0) == half_of((LANES, LANES), 1), 1.0, 0.0).astype(BF16)
    upper = half_of((proj.shape[0], LANES), 1)
    pairs = [slice(c * LANES, (c + 1) * LANES) for c in range(n_pairs)]
    splits = [_split_bf16(proj[:, cols] * proj[:, cols]) for cols in pairs]
    sums = [_dot(hi, same_half) + _dot(lo, same_half) for hi, lo in splits]
    for c, (cols, ss) in enumerate(zip(pairs, sums)):
        qn = proj[:, cols] * lax.rsqrt(ss * (1.0 / B_HEAD_DIM) + NORM_EPS) * qgain_ref[:, cols]
        for h, own in ((c, jnp.where(upper, 0.0, qn)), (c + n_pairs, jnp.where(upper, qn, 0.0))):
            slab = slice(h * LANES, (h + 1) * LANES)
            q_ref[:, slab] = (own + qconst_ref[:, slab]).astype(BF16)
    gate_ref[...] = _sigmoid(proj[:, n_pairs * LANES:])


def nsa_q_proj(h2d, g_norm, w_in, q_norm, *, tm=512):
    t_total, d = h2d.shape
    g, hpg, dh = B_KV_HEADS, B_GROUP, B_HEAD_DIM
    d_q = B_HEADS * LANES
    n_pairs = B_HEADS // 2
    assert 2 * dh == LANES and all(_vector_in_upper_lanes(h // hpg) == (h >= n_pairs) for h in range(B_HEADS))
    wq = w_in[:, :B_HEADS * dh].reshape(d, 2, n_pairs, dh).transpose(0, 2, 1, 3).reshape(d, n_pairs * LANES)
    wg = w_in[:, B_HEADS * dh:].reshape(d, g, hpg * 3)
    wg = jnp.pad(wg, ((0, 0), (0, 0), (0, LANES - hpg * 3))).reshape(d, g * LANES)
    w = jnp.concatenate([wq, wg], axis=1).astype(BF16)
    qgain = jnp.tile(q_norm * (dh ** -0.5 * LOG2E), 2 * n_pairs).reshape(1, n_pairs * LANES)
    slopes = np.array([2.0 ** (-8.0 * (h + 1) / B_HEADS) for h in range(B_HEADS)], dtype=np.float64) * LOG2E
    feats = _alibi_query_part(jnp.asarray(slopes, F32)).astype(F32)
    n_feat = feats.shape[1]
    qconst = jnp.concatenate([jnp.pad(feats[:n_pairs], ((0, 0), (dh, LANES - dh - n_feat))),
                              jnp.pad(feats[n_pairs:], ((0, 0), (0, LANES - n_feat)))]).reshape(1, d_q)
    const2 = lambda i: (0, 0)
    return pl.pallas_call(
        _q_proj_kernel,
        grid=(t_total // tm,),
        in_specs=[
            pl.BlockSpec((tm, d), lambda i: (i, 0)),
            _resident((1, d), const2),
            _resident((d, (n_pairs + g) * LANES), const2),
            _resident((1, n_pairs * LANES), const2),
            _resident((1, d_q), const2),
        ],
        out_specs=[pl.BlockSpec((tm, d_q), lambda i: (i, 0)), pl.BlockSpec((tm, g * LANES), lambda i: (i, 0))],
        out_shape=[jax.ShapeDtypeStruct((t_total, d_q), BF16), jax.ShapeDtypeStruct((t_total, g * LANES), F32)],
        compiler_params=_cparams(1),
        name="nsa_q_proj",
    )(h2d, g_norm.reshape(1, d), w, qgain, qconst)


def _nsa_attn_kernel(q_ref, gate_ref, kca_ref, vct_ref, ks_ref, vst_ref, kw_ref, vwt_ref,
                     ovlt_ref, eselt_ref, o_ref, live_smem, m_scr, acc_scr, *, tq, tk):
    hpg, dh = B_GROUP, B_HEAD_DIM
    t0 = pl.program_id(2) * tq
    rows = hpg * tq
    qb = q_ref[...]
    q_st = jnp.concatenate([qb[:, i * LANES:(i + 1) * LANES] for i in range(hpg)], axis=0)

    t_l = t0 + lax.broadcasted_iota(jnp.int32, (1, tq), 1)

    def normalised(acc):
        return acc[:dh] / acc[dh:2 * dh]

    span = WINDOW + tq
    w0 = pl.multiple_of(jnp.maximum(t0 - WINDOW, 0), LANES)
    dist = t_l - (w0 + lax.broadcasted_iota(jnp.int32, (span, 1), 0))
    pen = jnp.where((dist >= 0) & (dist < WINDOW), 0.0, NEG)
    s = _dot_nt(kw_ref[0, 0, pl.ds(w0, span), :], q_st) + jnp.concatenate([pen] * hpg, axis=1)
    m = jnp.max(s, axis=0, keepdims=True)
    o_win = normalised(_dot(vwt_ref[0, 0, :, pl.ds(w0, span)], jnp.exp2(s - m).astype(BF16)))

    n_half = kca_ref.shape[0]
    c_end = lax.broadcasted_iota(jnp.int32, (n_half, 1), 0) * CMP_STRIDE + (CMP_BLOCK - 1)
    t_lane = t0 + lax.broadcasted_iota(jnp.int32, (1, rows), 1) % tq
    seen = t_lane >= c_end
    s = jnp.where(seen, _dot_nt(kca_ref[...], q_st), NEG)
    m = jnp.max(s, axis=0, keepdims=True)
    e = jnp.where(seen, jnp.exp2(s - m), 0.0)
    p = e / jnp.maximum(jnp.sum(e, axis=0, keepdims=True), TINY)
    o_cmp = _dot(vct_ref[0], p.astype(BF16))[:dh]

    p_grp = p[:, 0:tq]
    for i in range(1, hpg):
        p_grp = p_grp + p[:, i * tq:(i + 1) * tq]
    p_hi, p_lo = _split_bf16(p_grp)
    n_blk = LANES // 2
    imp = (_dot(ovlt_ref[...], p_hi) + _dot(ovlt_ref[...], p_lo))[:n_blk]
    blk = lax.broadcasted_iota(jnp.int32, (n_blk, tq), 0)
    valid = blk * SEL_BLOCK <= t_l
    cur = t_l // SEL_BLOCK
    forced = valid & ((blk == 0) | (blk == cur) | (blk == cur - 1))
    work = jnp.where(forced, SEL_BIG, jnp.where(valid, imp, -SEL_BIG))
    blk_f = blk.astype(F32)
    chosen = jnp.zeros((n_blk, tq), F32)
    for _ in range(N_SELECT):
        top = jnp.max(work, axis=0, keepdims=True)
        first = jnp.min(jnp.where(work == top, blk_f, float(LANES)), axis=0, keepdims=True)
        hit = blk_f == first
        chosen = jnp.where(hit & (top > -0.5 * SEL_BIG), 1.0, chosen)
        work = jnp.where(hit, -3e38, work)
    blocks_per_tile = tk // SEL_BLOCK
    any_q = jnp.max(chosen, axis=1, keepdims=True)
    for kt in range(n_blk // blocks_per_tile):
        tile_any = jnp.max(any_q[kt * blocks_per_tile:(kt + 1) * blocks_per_tile])
        live_smem[kt] = (tile_any > 0.5).astype(jnp.int32)
    chosen = jnp.concatenate([chosen, jnp.zeros((LANES - n_blk, tq), F32)], axis=0).astype(BF16)

    m_scr[...] = jnp.full(m_scr.shape, NEG, F32)
    acc_scr[...] = jnp.zeros(acc_scr.shape, F32)

    def sel_step(kt, carry):
        @pl.when(live_smem[kt] != 0)
        def _():
            k0 = pl.multiple_of(kt * tk, tk)
            pos = k0 + lax.broadcasted_iota(jnp.int32, (tk, 1), 0)
            picked = _dot(eselt_ref[pl.ds(k0, tk), :], chosen)
            pen = jnp.where((picked > 0.5) & (pos <= t_l), 0.0, NEG)
            sc = _dot_nt(ks_ref[0, 0, pl.ds(k0, tk), :], q_st) + jnp.concatenate([pen] * hpg, axis=1)
            m_run = m_scr[...]
            m_new = jnp.maximum(m_run, jnp.broadcast_to(jnp.max(sc, axis=0, keepdims=True), m_run.shape))
            pr = jnp.exp2(sc - m_new[0:1]).astype(BF16)
            acc_scr[...] = (jnp.exp2(m_run - m_new)[0:1] * acc_scr[...]
                            + _dot(vst_ref[0, 0, :, pl.ds(k0, tk)], pr))
            m_scr[...] = m_new
        return carry

    lax.fori_loop(0, (t0 + tq + tk - 1) // tk, sel_step, 0)
    o_sel = normalised(acc_scr[...])

    gate_t = gate_ref[...].T
    outs = []
    for i in range(hpg):
        c = slice(i * tq, (i + 1) * tq)
        outs.append(gate_t[3 * i:3 * i + 1] * o_cmp[:, c] + gate_t[3 * i + 1:3 * i + 2] * o_sel[:, c]
                    + gate_t[3 * i + 2:3 * i + 3] * o_win[:, c])
    o_ref[...] = jnp.concatenate(outs, axis=0).T.astype(BF16)


def nsa_attention(q, gates, kca, vct, ks, vst, kw, vwt, bsz, seq, *, tq=256, tk=512):
    g, hpg, dh = B_KV_HEADS, B_GROUP, B_HEAD_DIM
    n_q = seq // tq
    n_half = seq // CMP_STRIDE
    n_cmp = n_half - 1
    n_sel = seq // SEL_BLOCK
    assert n_sel <= LANES // 2 and tq % LANES == 0
    c_start = np.arange(n_half) * CMP_STRIDE
    j_sel = np.arange(LANES)
    ovl = ((c_start[:, None] < (j_sel[None, :] + 1) * SEL_BLOCK)
           & (c_start[:, None] + CMP_BLOCK > j_sel[None, :] * SEL_BLOCK)
           & (np.arange(n_half)[:, None] < n_cmp) & (j_sel[None, :] < n_sel))
    esel = (np.arange(seq)[None, :] // SEL_BLOCK) == j_sel[:, None]
    row = lambda b, j, i: (b * n_q + i, j)
    per_bg = lambda b, j, i: (b, j, 0, 0)
    return pl.pallas_call(
        functools.partial(_nsa_attn_kernel, tq=tq, tk=tk),
        grid=(bsz, g, n_q),
        in_specs=[
            pl.BlockSpec((tq, hpg * LANES), row),
            pl.BlockSpec((tq, LANES), row),
            pl.BlockSpec((n_half, LANES), lambda b, j, i: (b * g + j, 0)),
            pl.BlockSpec((1, LANES, n_half), lambda b, j, i: (b * g + j, 0, 0)),
            pl.BlockSpec((1, 1, seq, LANES), per_bg),
            pl.BlockSpec((1, 1, LANES, seq), per_bg),
            pl.BlockSpec((1, 1, seq, LANES), per_bg),
            pl.BlockSpec((1, 1, LANES, seq), per_bg),
            _resident((LANES, n_half), lambda b, j, i: (0, 0)),
            _resident((seq, LANES), lambda b, j, i: (0, 0)),
        ],
        out_specs=pl.BlockSpec((tq, hpg * dh), row),
        out_shape=jax.ShapeDtypeStruct((bsz * seq, B_HEADS * dh), BF16),
        scratch_shapes=[pltpu.SMEM((seq // tk,), jnp.int32), pltpu.VMEM((SUBLANES, hpg * tq), F32),
                        pltpu.VMEM((LANES, hpg * tq), F32)],
        compiler_params=_cparams(3),
        name="nsa_attention",
    )(q, gates, kca, vct, ks, vst, kw, vwt, jnp.asarray(ovl.T, BF16), jnp.asarray(esel.T, BF16))


def _out_proj_kernel(h_ref, o_ref, w_ref, out_ref):
    out_ref[...] = h_ref[...] + _dot(o_ref[...], w_ref[...])


def out_proj_residual(h2d, o, w_out, *, tm=512):
    t_total, d = h2d.shape
    k = o.shape[1]
    return pl.pallas_call(
        _out_proj_kernel,
        grid=(t_total // tm,),
        in_specs=[pl.BlockSpec((tm, d), lambda i: (i, 0)), pl.BlockSpec((tm, k), lambda i: (i, 0)),
                  _resident((k, d), lambda i: (0, 0))],
        out_specs=pl.BlockSpec((tm, d), lambda i: (i, 0)),
        out_shape=jax.ShapeDtypeStruct((t_total, d), F32),
        compiler_params=_cparams(1),
        name="out_proj_residual",
    )(h2d, o, w_out.astype(BF16))


def nsa_shared_kv(h2d, bsz, seq, kv_norm, kv_w, cmp_pos_k, cmp_w1_k, cmp_w2_k, cmp_pos_v, cmp_w1_v, cmp_w2_v, k_norm):
    ks, kw, vst, vwt, kc_raw, vc_raw = nsa_kv_proj(h2d, bsz, seq, kv_norm, kv_w, k_norm)
    kca, vct = nsa_compress(kc_raw, vc_raw, bsz, seq, cmp_pos_k, cmp_w1_k, cmp_w2_k,
                            cmp_pos_v, cmp_w1_v, cmp_w2_v, k_norm[0])
    return kca, vct, ks, vst, kw, vwt


def nsa_mixer(h2d, bsz, seq, g_norm, w_in, q_norm, kv):
    q, gates = nsa_q_proj(h2d, g_norm, w_in, q_norm)
    return nsa_attention(q, gates, *kv, bsz, seq)


def kernel(x, norm_mix, norm_ffn, hgrn_w_in, hgrn_w_out, hgrn_g_norm, hgrn_lb_logits, kv_norm, kv_w, cmp_pos_k, cmp_w1_k, cmp_w2_k, cmp_pos_v, cmp_w1_v, cmp_w2_v, k_norm, nsa_w_in, nsa_w_out, nsa_q_norm, ffn_w_gu, ffn_w_down, moe_router, moe_w_gu, moe_w_down):
    bsz, seq, d = x.shape
    lb = jnp.cumsum(jax.nn.softmax(hgrn_lb_logits.astype(F32), axis=0), axis=0)
    lb = lb - lb[0:1]
    n_a = hgrn_w_in.shape[0]
    depth = norm_mix.shape[0]
    h = x.reshape(bsz * seq, d)
    kv = None
    for layer in range(depth):
        mixer_out = None
        if layer < n_a:
            h = hgrn_mixer(h, seq, norm_mix[layer], hgrn_w_in[layer], hgrn_w_out[layer], hgrn_g_norm[layer],
                           lb[layer])
        else:
            if kv is None:
                kv = nsa_shared_kv(h, bsz, seq, kv_norm, kv_w, cmp_pos_k, cmp_w1_k, cmp_w2_k,
                                   cmp_pos_v, cmp_w1_v, cmp_w2_v, k_norm)
            j = layer - n_a
            mixer_out = (nsa_mixer(h, bsz, seq, norm_mix[layer], nsa_w_in[j], nsa_q_norm[j], kv), nsa_w_out[j])
            if layer % 2 == 1:
                h = out_proj_residual(h, *mixer_out)
                mixer_out = None
        if layer % 2 == 0:
            h = ffn_swiglu(h, norm_ffn[layer], ffn_w_gu[layer // 2], ffn_w_down[layer // 2], mixer_out=mixer_out)
        else:
            h = moe_swiglu_top2(h, norm_ffn[layer], moe_router[layer // 2], moe_w_gu[layer // 2],
                                moe_w_down[layer // 2])
    return h.reshape(bsz, seq, d)
```

```python
import functools

import numpy as np
import jax
import jax.numpy as jnp
from jax import lax
from jax.experimental import pallas as pl
from jax.experimental.pallas import tpu as pltpu

F32 = jnp.float32
BF16 = jnp.bfloat16

NORM_EPS = 1e-6
VMEM_LIMIT_BYTES = 56 * 1024 * 1024

A_HEADS = 8
A_HEAD_DIM = 128
SCAN_ROWS = 128
SUBLANES = 8


def _cparams(n_axes):
    return pltpu.CompilerParams(dimension_semantics=("arbitrary",) * n_axes,
                                vmem_limit_bytes=VMEM_LIMIT_BYTES)


def _dot(a, b):
    return jnp.dot(a, b, preferred_element_type=F32)


def _dot_nt(a, b):
    return lax.dot_general(a, b, (((1,), (1,)), ((), ())), preferred_element_type=F32)


def _dot_tn(a, b):
    return lax.dot_general(a, b, (((0,), (0,)), ((), ())), preferred_element_type=F32)


def _sigmoid(x):
    return 0.5 * jnp.tanh(0.5 * x) + 0.5


def _rms(x, g):
    return x * lax.rsqrt(jnp.mean(x * x, axis=-1, keepdims=True) + NORM_EPS) * g


def _split_bf16(x):
    hi = x.astype(BF16)
    lo = (x - hi.astype(F32)).astype(BF16)
    return hi, lo


def _hgrn_constants():
    n = SCAN_ROWS
    t = np.arange(n)
    r = np.arange(n)[None, :]
    sums, masks = [], []
    m = n // 2
    while m >= 1:
        grp = t // (2 * m)
        mid = grp * 2 * m + m
        upper = (t % (2 * m)) >= m
        if m < SUBLANES:
            sums.append(np.where(upper[:, None], (r >= mid[:, None]) & (r <= t[:, None]),
                                 (r > t[:, None]) & (r < mid[:, None])))
        masks.append((grp[:, None] == grp[None, :]) & upper[:, None] & (~upper[None, :]))
        m //= 2
    masks.append(np.eye(n, dtype=bool))
    sums.append(r <= t[:, None])
    w = np.concatenate(sums, 0).astype(np.float32)
    return np.concatenate([w, w], 1), np.stack(masks).astype(np.float32)


def _hgrn_kernel(x_ref, gn_ref, win_ref, lb_ref, ghn_ref, wout_ref, wsum_ref, msk_ref,
                 out_ref, xn_scr, st_scr, og_scr, *, tm):
    sblk = pl.program_id(1)
    h = pl.program_id(2)
    n = SCAN_ROWS
    n_lvl = msk_ref.shape[0] - 1

    @pl.when(h == 0)
    def _():
        xn_scr[...] = _rms(x_ref[...], gn_ref[...]).astype(BF16)

    @pl.when(sblk == 0)
    def _():
        st_scr[h] = jnp.zeros((A_HEAD_DIM, A_HEAD_DIM), F32)

    proj = _dot(xn_scr[...], win_ref[...])
    q = proj[:, 0:128]
    f = proj[:, 128:256]
    v = proj[:, 256:384]
    g = proj[:, 384:512]
    lb = lb_ref[0:1, :]
    log_lb = lb_ref[1:2, :]
    log_1m_lb = lb_ref[2:3, :]
    qs = q * _sigmoid(q)
    a = jnp.exp(-jnp.abs(f))
    inv = 1.0 / (1.0 + a)
    log_sig = jnp.minimum(f, 0.0) - jnp.log1p(a)
    b = log_1m_lb + log_sig
    logf = jnp.maximum(log_lb, b) + jnp.log1p(jnp.exp(-jnp.abs(log_lb - b)))
    key = (1.0 - lb) * jnp.where(f >= 0.0, a * inv, inv)

    wide = [n >> (l + 1) for l in range(n_lvl) if n >> (l + 1) >= SUBLANES]
    row_id = lax.broadcasted_iota(jnp.int32, (n, A_HEAD_DIM), 0)
    in_upper_half = [(row_id % (2 * m)) >= m for m in wide]

    blocks = [slice(r * n, (r + 1) * n) for r in range(tm // n)]
    n_narrow = n_lvl - len(wide)
    sums = []
    for rows in blocks:
        hi, lo = _split_bf16(logf[rows])
        sums.append(_dot(wsum_ref[...], jnp.concatenate([hi, lo], axis=0)))
    e_lvl, e_prefix, e_suffix = [], [], []
    for sm in sums:
        cum = sm[n_narrow * n:]
        e_blk = []
        for m, upper in zip(wide, in_upper_half):
            bound = jnp.concatenate([jnp.broadcast_to(cum[i + m - 1:i + m], (2 * m, A_HEAD_DIM))
                                     for i in range(0, n, 2 * m)], axis=0)
            e_blk.append(jnp.exp(jnp.where(upper, cum - bound, bound - cum)))
        e_sm = jnp.exp(sm)
        e_lvl.append(e_blk + [e_sm[i * n:(i + 1) * n] for i in range(n_narrow)])
        e_prefix.append(e_sm[n_narrow * n:])
        e_suffix.append(jnp.exp(cum[n - 1:n] - cum))
    intra, inter_q, kv = [], [], []
    for rows, e_blk, e_pre, e_suf in zip(blocks, e_lvl, e_prefix, e_suffix):
        qb = qs[rows]
        kb = key[rows]
        vb = v[rows].astype(BF16)
        s = msk_ref[n_lvl] * _dot_nt(qb.astype(BF16), kb.astype(BF16))
        for l in range(n_lvl):
            s = s + msk_ref[l] * _dot_nt((qb * e_blk[l]).astype(BF16), (kb * e_blk[l]).astype(BF16))
        intra.append(_dot(s.astype(BF16), vb))
        inter_q.append((qb * e_pre).astype(BF16))
        kv.append(_dot_tn(vb, (kb * e_suf).astype(BF16)))
    st = st_scr[h]
    outs = []
    for o_intra, q_pre, e_pre, kv_blk in zip(intra, inter_q, e_prefix, kv):
        outs.append(o_intra + _dot_nt(q_pre, st.astype(BF16)))
        st = st * e_pre[n - 1:n, :] + kv_blk
    st_scr[h] = st
    o = jnp.concatenate(outs, axis=0)

    og_scr[:, pl.ds(pl.multiple_of(h * A_HEAD_DIM, A_HEAD_DIM), A_HEAD_DIM)] = (
        _rms(o, ghn_ref[...]) * (g * _sigmoid(g))).astype(BF16)

    @pl.when(h == A_HEADS - 1)
    def _():
        out_ref[...] = x_ref[...] + _dot(og_scr[...], wout_ref[...])


def hgrn_mixer(h2d, seq, g_norm_in, w_in, w_out, g_head, lb, *, tm=1024):
    t_total, d = h2d.shape
    n_sblk = seq // tm
    hd = A_HEAD_DIM
    w_perm = w_in.reshape(d, 4, A_HEADS, hd).transpose(0, 2, 1, 3).reshape(d, 4 * A_HEADS * hd).astype(BF16)
    lb_rows = jnp.zeros((8, A_HEADS * hd), F32)
    lb_rows = lb_rows.at[0].set(lb).at[1].set(jnp.log(lb)).at[2].set(jnp.log1p(-lb))
    wsum, masks = _hgrn_constants()
    n_blk = wsum.shape[0]
    grid = (t_total // seq, n_sblk, A_HEADS)
    row = lambda b, s, h: (b * n_sblk + s, 0)
    const2 = lambda b, s, h: (0, 0)
    return pl.pallas_call(
        functools.partial(_hgrn_kernel, tm=tm),
        grid=grid,
        in_specs=[
            pl.BlockSpec((tm, d), row),
            pl.BlockSpec((1, d), const2),
            pl.BlockSpec((d, 4 * hd), lambda b, s, h: (0, h)),
            pl.BlockSpec((8, hd), lambda b, s, h: (0, h)),
            pl.BlockSpec((1, hd), const2),
            _resident((A_HEADS * hd, d), const2),
            _resident((n_blk, 2 * SCAN_ROWS), const2),
            _resident(masks.shape, lambda b, s, h: (0, 0, 0)),
        ],
        out_specs=pl.BlockSpec((tm, d), row),
        out_shape=jax.ShapeDtypeStruct((t_total, d), F32),
        scratch_shapes=[pltpu.VMEM((tm, d), BF16), pltpu.VMEM((A_HEADS, hd, hd), F32),
                        pltpu.VMEM((tm, A_HEADS * hd), BF16)],
        compiler_params=_cparams(3),
        name="hgrn_mixer",
    )(h2d, g_norm_in.reshape(1, d), w_perm, lb_rows, g_head.reshape(1, hd), w_out.astype(BF16),
      jnp.asarray(wsum, BF16), jnp.asarray(masks, F32))


def _resident(block_shape, index_map):
    return pl.BlockSpec(block_shape, index_map, pipeline_mode=pl.Buffered(1))


MXU_TILE = 256


def _swiglu(xn, wg_ref, wu_ref, wd_ref):
    acc = None
    for c in range(wg_ref.shape[1] // MXU_TILE):
        cols = slice(c * MXU_TILE, (c + 1) * MXU_TILE)
        gate = _dot(xn, wg_ref[:, cols])
        up = _dot(xn, wu_ref[:, cols])
        part = _dot((gate * _sigmoid(gate) * up).astype(BF16), wd_ref[cols, :])
        acc = part if acc is None else acc + part
    return acc


def _ffn_kernel(*refs, with_mixer_out):
    if with_mixer_out:
        x_ref, o_ref, wo_ref, gn_ref, wg_ref, wu_ref, wd_ref, out_ref = refs
        x = x_ref[...] + _dot(o_ref[...], wo_ref[...])
    else:
        x_ref, gn_ref, wg_ref, wu_ref, wd_ref, out_ref = refs
        x = x_ref[...]
    xn = _rms(x, gn_ref[...]).astype(BF16)
    out_ref[...] = x + _swiglu(xn, wg_ref, wu_ref, wd_ref)


def ffn_swiglu(h2d, g_norm, w_gu, w_down, *, mixer_out=None, tm=512):
    t_total, d = h2d.shape
    ff = w_down.shape[0]
    w_gu = w_gu.astype(BF16)
    const2 = lambda i: (0, 0)
    rows = lambda i: (i, 0)
    mixer_specs, mixer_args = [], []
    if mixer_out is not None:
        o, w_out = mixer_out
        mixer_specs = [pl.BlockSpec((tm, o.shape[1]), rows), _resident(w_out.shape, const2)]
        mixer_args = [o, w_out.astype(BF16)]
    return pl.pallas_call(
        functools.partial(_ffn_kernel, with_mixer_out=mixer_out is not None),
        grid=(t_total // tm,),
        in_specs=[pl.BlockSpec((tm, d), rows)] + mixer_specs + [
            _resident((1, d), const2),
            _resident((d, ff), const2),
            _resident((d, ff), lambda i: (0, 1)),
            _resident((ff, d), const2),
        ],
        out_specs=pl.BlockSpec((tm, d), rows),
        out_shape=jax.ShapeDtypeStruct((t_total, d), F32),
        compiler_params=_cparams(1),
        name="ffn_swiglu",
    )(h2d, *mixer_args, g_norm.reshape(1, d), w_gu, w_gu, w_down.astype(BF16))


N_EXPERTS = 8
LANES = 128
NEG = -1e30


def _router_kernel(x_ref, gn_ref, wh_ref, wl_ref, sel_ref):
    xn = _rms(x_ref[...], gn_ref[...])
    xh, xl = _split_bf16(xn)
    logits = _dot(xh, wh_ref[...]) + (_dot(xl, wh_ref[...]) + _dot(xh, wl_ref[...]))
    lane = lax.broadcasted_iota(jnp.int32, logits.shape, 1).astype(F32)
    logits = jnp.where(lane < N_EXPERTS, logits, NEG)
    m1 = jnp.max(logits, axis=-1, keepdims=True)
    i1 = jnp.min(jnp.where(logits == m1, lane, float(LANES)), axis=-1, keepdims=True)
    rest = jnp.where(lane == i1, NEG, logits)
    m2 = jnp.max(rest, axis=-1, keepdims=True)
    i2 = jnp.min(jnp.where(rest == m2, lane, float(LANES)), axis=-1, keepdims=True)
    e2 = jnp.exp(m2 - m1)
    denom = 1.0 + e2
    sel_ref[...] = jnp.where(lane == 0, i1, jnp.where(lane == 1, i2, jnp.where(
        lane == 2, 1.0 / denom, jnp.where(lane == 3, e2 / denom, 0.0))))


def moe_route(h2d, g_norm, w_router, *, tm=512):
    t_total, d = h2d.shape
    w_pad = jnp.zeros((d, LANES), F32).at[:, :N_EXPERTS].set(w_router)
    w_hi = w_pad.astype(BF16)
    w_lo = (w_pad - w_hi.astype(F32)).astype(BF16)
    const2 = lambda i: (0, 0)
    return pl.pallas_call(
        _router_kernel,
        grid=(t_total // tm,),
        in_specs=[
            pl.BlockSpec((tm, d), lambda i: (i, 0)),
            _resident((1, d), const2),
            _resident((d, LANES), const2),
            _resident((d, LANES), const2),
        ],
        out_specs=pl.BlockSpec((tm, LANES), lambda i: (i, 0)),
        out_shape=jax.ShapeDtypeStruct((t_total, LANES), F32),
        compiler_params=_cparams(1),
        name="moe_router",
    )(h2d, g_norm.reshape(1, d), w_hi, w_lo)


MOE_TILE = 1024
GATHER_UNROLL = 8


def _moe_plan(sel, tm):
    t_total = sel.shape[0]
    e = sel[:, :2].astype(jnp.int32)
    onehot = (e[:, :, None] == jnp.arange(N_EXPERTS, dtype=jnp.int32)[None, None, :]).astype(jnp.int32).sum(1)
    csum = jnp.cumsum(onehot, axis=0)
    rank = csum - onehot
    padded = (csum[-1] + tm - 1) // tm * tm
    ends = jnp.cumsum(padded)
    pos = (ends - padded)[e] + jnp.take_along_axis(rank, e, axis=1)
    n_slots = 2 * t_total + N_EXPERTS * tm
    tile_start = jnp.arange(n_slots // tm, dtype=jnp.int32) * tm
    tile_expert = jnp.minimum(jnp.searchsorted(ends, tile_start, side="right"), N_EXPERTS - 1).astype(jnp.int32)
    tile_used = (tile_start < ends[-1]).astype(jnp.int32)
    n_used = (ends[-1:] // tm).astype(jnp.int32)
    last_tile = jnp.maximum(ends - tm, 0).astype(jnp.int32)
    return n_slots, tile_expert, tile_used, n_used, last_tile, (padded > 0).astype(jnp.int32), pos[:, 0], pos[:, 1]


def _for_each_row(n_rows, copy_row):
    def group(it, carry):
        for u in range(GATHER_UNROLL):
            copy_row(it * GATHER_UNROLL + u, u % 2)
        return carry
    lax.fori_loop(0, n_rows // GATHER_UNROLL, group, 0)


def _load_indices(idx_hbm, idx_smem, sem, tile, tm):
    return pltpu.make_async_copy(idx_hbm.at[pl.ds(tile * tm, tm)], idx_smem, sem)


def _moe_dispatch_kernel(last_tile_ref, has_rows_ref, n_used_ref, pa_hbm, pb_hbm, h_ref, xs_hbm, ia_smem, ib_smem,
                         zero_scr, row_sem, idx_sem, zero_sem, *, tm):
    i = pl.program_id(0)

    @pl.when(i == 0)
    def _():
        zero_scr[...] = jnp.zeros(zero_scr.shape, F32)
        n_tiles = xs_hbm.shape[0] // tm
        for e in range(N_EXPERTS):
            spare = n_used_ref[0] + e
            for start, wanted in ((last_tile_ref[e], has_rows_ref[e] != 0), (spare * tm, spare < n_tiles)):
                fill = pltpu.make_async_copy(zero_scr, xs_hbm.at[pl.ds(pl.multiple_of(start, tm), tm), :],
                                             zero_sem.at[e])

                @pl.when(wanted)
                def _():
                    fill.start()
                    fill.wait()

    ca = _load_indices(pa_hbm, ia_smem, idx_sem.at[0], i, tm)
    cb = _load_indices(pb_hbm, ib_smem, idx_sem.at[1], i, tm)
    ca.start()
    cb.start()
    ca.wait()
    cb.wait()

    def copy_row(r, priority):
        src = h_ref.at[pl.ds(r, 1), :]
        pltpu.make_async_copy(src, xs_hbm.at[pl.ds(ia_smem[r], 1), :], row_sem.at[0]).start(priority=priority)
        pltpu.make_async_copy(src, xs_hbm.at[pl.ds(ib_smem[r], 1), :], row_sem.at[1]).start(priority=1 - priority)
    _for_each_row(tm, copy_row)
    pltpu.make_async_copy(h_ref, xs_hbm.at[pl.ds(0, tm), :], row_sem.at[0]).wait()
    pltpu.make_async_copy(h_ref, xs_hbm.at[pl.ds(0, tm), :], row_sem.at[1]).wait()


def _moe_group_kernel(te_ref, used_ref, xs_ref, gn_ref, wg_ref, wu_ref, wd_ref, y_ref, xn_scr):
    i = pl.program_id(0)
    j = pl.program_id(1)

    @pl.when(used_ref[i] == 0)
    def _():
        y_ref[...] = jnp.zeros(y_ref.shape, F32)

    @pl.when(used_ref[i] != 0)
    def _():
        @pl.when(j == 0)
        def _():
            xn_scr[...] = _rms(xs_ref[...], gn_ref[...]).astype(BF16)

        contrib = _swiglu(xn_scr[...], wg_ref.at[0], wu_ref.at[0], wd_ref.at[0])

        @pl.when(j == 0)
        def _():
            y_ref[...] = contrib

        @pl.when(j > 0)
        def _():
            y_ref[...] += contrib


def _moe_combine_kernel(pa_hbm, pb_hbm, y_hbm, h_ref, sel_ref, out_ref, ia_smem, ib_smem, buf_a, buf_b,
                        row_sem, idx_sem, *, tm):
    i = pl.program_id(0)
    ca = _load_indices(pa_hbm, ia_smem, idx_sem.at[0], i, tm)
    cb = _load_indices(pb_hbm, ib_smem, idx_sem.at[1], i, tm)
    ca.start()
    cb.start()
    ca.wait()
    cb.wait()

    def copy_row(r, priority):
        pltpu.make_async_copy(y_hbm.at[pl.ds(ia_smem[r], 1), :], buf_a.at[pl.ds(r, 1), :],
                              row_sem.at[0]).start(priority=priority)
        pltpu.make_async_copy(y_hbm.at[pl.ds(ib_smem[r], 1), :], buf_b.at[pl.ds(r, 1), :],
                              row_sem.at[1]).start(priority=1 - priority)
    _for_each_row(tm, copy_row)
    pltpu.make_async_copy(buf_a, buf_a, row_sem.at[0]).wait()
    pltpu.make_async_copy(buf_b, buf_b, row_sem.at[1]).wait()
    sel = sel_ref[...]
    out_ref[...] = h_ref[...] + (sel[:, 2:3] * buf_a[...] + sel[:, 3:4] * buf_b[...])


def moe_swiglu_top2(h2d, g_norm, w_router, w_gu, w_down, *, n_chunks=2):
    t_total, d = h2d.shape
    tm = MOE_TILE
    ff = w_down.shape[1]
    tf = ff // n_chunks
    sel = moe_route(h2d, g_norm, w_router)
    n_slots, tile_expert, tile_used, n_used, last_tile, has_rows, pos_a, pos_b = _moe_plan(sel, tm)
    any_spec = pl.BlockSpec(memory_space=pl.ANY)
    index_scratch = [pltpu.SMEM((tm,), jnp.int32), pltpu.SMEM((tm,), jnp.int32)]
    sems = [pltpu.SemaphoreType.DMA((2,)), pltpu.SemaphoreType.DMA((2,))]
    xs = pl.pallas_call(
        functools.partial(_moe_dispatch_kernel, tm=tm),
        grid_spec=pltpu.PrefetchScalarGridSpec(
            num_scalar_prefetch=3,
            grid=(t_total // tm,),
            in_specs=[any_spec, any_spec, pl.BlockSpec((tm, d), lambda i, lt, hr, nu: (i, 0))],
            out_specs=any_spec,
            scratch_shapes=index_scratch + [pltpu.VMEM((tm, d), F32)] + sems
            + [pltpu.SemaphoreType.DMA((N_EXPERTS,))],
        ),
        out_shape=jax.ShapeDtypeStruct((n_slots, d), F32),
        compiler_params=_cparams(1),
        name="moe_dispatch",
    )(last_tile, has_rows, n_used, pos_a, pos_b, h2d)
    w_gu = w_gu.astype(BF16)
    row = lambda i, j, te, used: (i, 0)
    y = pl.pallas_call(
        _moe_group_kernel,
        grid_spec=pltpu.PrefetchScalarGridSpec(
            num_scalar_prefetch=2,
            grid=(n_slots // tm, n_chunks),
            in_specs=[
                pl.BlockSpec((tm, d), row),
                pl.BlockSpec((1, d), lambda i, j, te, used: (0, 0)),
                pl.BlockSpec((1, d, tf), lambda i, j, te, used: (te[i], 0, j)),
                pl.BlockSpec((1, d, tf), lambda i, j, te, used: (te[i], 0, j + n_chunks)),
                pl.BlockSpec((1, tf, d), lambda i, j, te, used: (te[i], j, 0)),
            ],
            out_specs=pl.BlockSpec((tm, d), row),
            scratch_shapes=[pltpu.VMEM((tm, d), BF16)],
        ),
        out_shape=jax.ShapeDtypeStruct((n_slots, d), F32),
        compiler_params=_cparams(2),
        name="moe_group_ffn",
    )(tile_expert, tile_used, xs, g_norm.reshape(1, d), w_gu, w_gu, w_down.astype(BF16))
    return pl.pallas_call(
        functools.partial(_moe_combine_kernel, tm=tm),
        grid=(t_total // tm,),
        in_specs=[any_spec, any_spec, any_spec, pl.BlockSpec((tm, d), lambda i: (i, 0)),
                  pl.BlockSpec((tm, LANES), lambda i: (i, 0))],
        out_specs=pl.BlockSpec((tm, d), lambda i: (i, 0)),
        out_shape=jax.ShapeDtypeStruct((t_total, d), F32),
        scratch_shapes=index_scratch + [pltpu.VMEM((tm, d), F32), pltpu.VMEM((tm, d), F32)] + sems,
        compiler_params=_cparams(1),
        name="moe_combine",
    )(pos_a, pos_b, y, h2d, sel)


B_HEADS = 16
B_KV_HEADS = 4
B_GROUP = 4
B_HEAD_DIM = 64
CMP_STRIDE = 16
CMP_BLOCK = 32
SEL_BLOCK = 32
N_SELECT = 8
WINDOW = 512
SEL_BIG = 1e9
TINY = 1e-30
N_KV_KINDS = 6

LOG2E = 1.4426950408889634
POS_RADIX = 64
N_SLOPE_PARTS = 3


def _vector_in_upper_lanes(kv_group):
    return kv_group >= B_KV_HEADS // 2


def _alibi_key_part(pos, idx):
    hi = (pos // POS_RADIX * POS_RADIX).astype(F32)
    lo = (pos % POS_RADIX).astype(F32)
    return jnp.where(idx < N_SLOPE_PARTS, hi, jnp.where(idx < 2 * N_SLOPE_PARTS, lo, 0.0))


def _alibi_query_part(slopes):
    parts, rest = [], slopes.astype(F32)
    for _ in range(N_SLOPE_PARTS):
        p = rest.astype(BF16)
        parts.append(p)
        rest = rest - p.astype(F32)
    return jnp.stack(parts + parts, axis=-1)


def _kv_proj_kernel(x_ref, gn_ref, wk_ref, kgain_ref, wvt_ref, wc_ref,
                    ks_ref, kw_ref, vst_ref, vwt_ref, kc_ref, vc_ref):
    g = B_KV_HEADS
    hn = _rms(x_ref[...], gn_ref[...]).astype(BF16)
    k = _dot(hn, wk_ref[...])
    tm = k.shape[0]
    pos = pl.program_id(1) * tm + lax.broadcasted_iota(jnp.int32, (tm, LANES), 0)
    lane = lax.broadcasted_iota(jnp.int32, (tm, LANES), 1)
    feats = _alibi_key_part(pos, lane % B_HEAD_DIM)
    for j in range(2 * g):
        kj = k[:, j * LANES:(j + 1) * LANES]
        ms = jnp.sum(kj * kj, axis=-1, keepdims=True) * (1.0 / B_HEAD_DIM)
        pos_lanes = jnp.where((lane >= B_HEAD_DIM) == _vector_in_upper_lanes(j % g), 0.0, feats)
        kj = (kj * lax.rsqrt(ms + NORM_EPS) * kgain_ref[:, j * LANES:(j + 1) * LANES] + pos_lanes).astype(BF16)
        if j < g:
            ks_ref[0, j] = kj
        else:
            kw_ref[0, j - g] = kj
    vt = _dot_nt(wvt_ref[...], hn)
    row = lax.broadcasted_iota(jnp.int32, vt.shape, 0)
    vt = jnp.where(row % LANES >= B_HEAD_DIM, 1.0, vt).astype(BF16)
    for j in range(g):
        vst_ref[0, j] = vt[j * LANES:(j + 1) * LANES]
        vwt_ref[0, j] = vt[(g + j) * LANES:(g + j + 1) * LANES]
    c = _dot(hn, wc_ref[...])
    half = g * B_HEAD_DIM
    kc_ref[...] = c[:, :half].astype(BF16)
    vc_ref[...] = c[:, half:].astype(BF16)


def nsa_kv_proj(h2d, bsz, seq, kv_norm, kv_w, k_norm, *, tm=512):
    t_total, d = h2d.shape
    g, dh = B_KV_HEADS, B_HEAD_DIM
    n_sblk = seq // tm
    w = kv_w.reshape(d, N_KV_KINDS, g, dh)
    def head_slabs(kinds):
        cols = jnp.stack([w[:, kind] for kind in kinds], axis=1)
        return jnp.pad(cols, ((0, 0), (0, 0), (0, 0), (0, LANES - dh))).reshape(d, len(kinds) * g * LANES)

    def key_slabs(a):
        lead = [(0, 0)] * (a.ndim - 2)
        first, second = a[..., :g // 2, :], a[..., g // 2:, :]
        both = jnp.concatenate([jnp.pad(first, lead + [(0, 0), (0, LANES - dh)]),
                                jnp.pad(second, lead + [(0, 0), (LANES - dh, 0)])], axis=-2)
        return both.reshape(a.shape[:-2] + (g * LANES,))

    assert [_vector_in_upper_lanes(j) for j in range(g)] == [j >= g // 2 for j in range(g)]
    wk = key_slabs(jnp.stack([w[:, 2], w[:, 4]], axis=1)).reshape(d, 2 * g * LANES).astype(BF16)
    wvt = head_slabs((3, 5)).T.astype(BF16)
    kgain = key_slabs(jnp.stack([jnp.tile(k_norm[1], (g, 1)), jnp.tile(k_norm[2], (g, 1))])).reshape(1, 2 * g * LANES)
    wc = jnp.concatenate([w[:, 0].reshape(d, g * dh), w[:, 1].reshape(d, g * dh)], axis=1).astype(BF16)
    const2 = lambda b, s: (0, 0)
    row = lambda b, s: (b * n_sblk + s, 0)
    k_spec = pl.BlockSpec((1, g, tm, LANES), lambda b, s: (b, 0, s, 0))
    vt_spec = pl.BlockSpec((1, g, LANES, tm), lambda b, s: (b, 0, 0, s))
    k_shape = jax.ShapeDtypeStruct((bsz, g, seq, LANES), BF16)
    vt_shape = jax.ShapeDtypeStruct((bsz, g, LANES, seq), BF16)
    c_shape = jax.ShapeDtypeStruct((t_total, g * dh), BF16)
    return pl.pallas_call(
        _kv_proj_kernel,
        grid=(bsz, n_sblk),
        in_specs=[
            pl.BlockSpec((tm, d), row),
            _resident((1, d), const2),
            _resident((d, 2 * g * LANES), const2),
            _resident((1, 2 * g * LANES), const2),
            _resident((2 * g * LANES, d), const2),
            _resident((d, 2 * g * dh), const2),
        ],
        out_specs=[k_spec, k_spec, vt_spec, vt_spec,
                   pl.BlockSpec((tm, g * dh), row), pl.BlockSpec((tm, g * dh), row)],
        out_shape=[k_shape, k_shape, vt_shape, vt_shape, c_shape, c_shape],
        compiler_params=_cparams(2),
        name="nsa_kv_proj",
    )(h2d, kv_norm.reshape(1, d), wk, kgain, wvt, wc)


def _compress_kernel(ak_ref, av_ref, posk_ref, posv_ref, w1k_ref, w1v_ref, w2k_ref, w2vt_ref, kgain_ref,
                     kca_ref, vct_ref, *, n_half):
    half_w = ak_ref.shape[1]

    def hidden(a_ref, pos_ref, w1_ref):
        a = a_ref[...]
        top = _dot(a, w1_ref[:half_w, :])
        bot = _dot(a, w1_ref[half_w:, :])
        bias = _dot(pos_ref[...], w1_ref[...])[0:1, :]
        pre = top + pltpu.roll(bot, shift=a.shape[0] - 1, axis=0) + bias
        return (pre * _sigmoid(pre)).astype(BF16)

    act_k = hidden(ak_ref, posk_ref, w1k_ref)
    kc = _dot(act_k, w2k_ref[...])
    row = lax.broadcasted_iota(jnp.int32, kc.shape, 0)
    lane = lax.broadcasted_iota(jnp.int32, kc.shape, 1)
    own_half = (lane >= B_HEAD_DIM) == _vector_in_upper_lanes((row // n_half) % B_KV_HEADS)
    kc = jnp.where(own_half, kc, 0.0)
    ms = jnp.sum(kc * kc, axis=-1, keepdims=True) * (1.0 / B_HEAD_DIM)
    kc = kc * lax.rsqrt(ms + NORM_EPS) * kgain_ref[...]
    c_end = row % n_half * CMP_STRIDE + (CMP_BLOCK - 1)
    kca_ref[...] = (kc + jnp.where(own_half, 0.0, _alibi_key_part(c_end, lane % B_HEAD_DIM))).astype(BF16)
    act_v = hidden(av_ref, posv_ref, w1v_ref)
    vct = _dot_nt(w2vt_ref[...], act_v).astype(BF16)
    for j in range(vct.shape[1] // n_half):
        vct_ref[j] = vct[:, j * n_half:(j + 1) * n_half]


def nsa_compress(kc_raw, vc_raw, bsz, seq, pos_k, w1_k, w2_k, pos_v, w1_v, w2_v, k_gain, *, groups_per_step=4):
    g, dh = B_KV_HEADS, B_HEAD_DIM
    n_half = seq // CMP_STRIDE
    feat = CMP_STRIDE * dh

    def to_half_blocks(a):
        a = a.reshape(bsz, n_half, CMP_STRIDE, g, dh).transpose(0, 3, 1, 2, 4)
        return a.reshape(bsz * g * n_half, feat)

    def pos_rows(pos):
        return jnp.zeros((8, 2 * feat), F32).at[0].set(pos.reshape(-1)).astype(BF16)

    hid = w1_k.shape[1]
    assert groups_per_step % g == 0
    pad_lanes = lambda a: jnp.pad(a, ((0, 0), (0, LANES - dh)))
    rows = groups_per_step * n_half
    const2 = lambda i: (0, 0)
    return pl.pallas_call(
        functools.partial(_compress_kernel, n_half=n_half),
        grid=(bsz * g // groups_per_step,),
        in_specs=[
            pl.BlockSpec((rows, feat), lambda i: (i, 0)),
            pl.BlockSpec((rows, feat), lambda i: (i, 0)),
            _resident((8, 2 * feat), const2),
            _resident((8, 2 * feat), const2),
            _resident((2 * feat, hid), const2),
            _resident((2 * feat, hid), const2),
            _resident((hid, LANES), const2),
            _resident((LANES, hid), const2),
            _resident((1, LANES), const2),
        ],
        out_specs=[pl.BlockSpec((rows, LANES), lambda i: (i, 0)),
                   pl.BlockSpec((groups_per_step, LANES, n_half), lambda i: (i, 0, 0))],
        out_shape=[jax.ShapeDtypeStruct((bsz * g * n_half, LANES), BF16),
                   jax.ShapeDtypeStruct((bsz * g, LANES, n_half), BF16)],
        compiler_params=_cparams(1),
        name="nsa_compress",
    )(to_half_blocks(kc_raw), to_half_blocks(vc_raw), pos_rows(pos_k), pos_rows(pos_v),
      w1_k.astype(BF16), w1_v.astype(BF16), jnp.tile(w2_k, (1, LANES // dh)).astype(BF16),
      pad_lanes(w2_v).T.astype(BF16), jnp.tile(k_gain, LANES // dh).reshape(1, LANES))


def _q_proj_kernel(x_ref, gn_ref, w_ref, qgain_ref, qconst_ref, q_ref, gate_ref):
    n_pairs = qgain_ref.shape[1] // LANES
    xn = _rms(x_ref[...], gn_ref[...]).astype(BF16)
    proj = _dot(xn, w_ref[...])
    half_of = lambda shape, axis: lax.broadcasted_iota(jnp.int32, shape, axis) >= B_HEAD_DIM
    same_half = jnp.where(half_of((LANES, LANES), 0) == half_of((LANES, LANES), 1), 1.0, 0.0).astype(BF16)
    upper = half_of((proj.shape[0], LANES), 1)
    pairs = [slice(c * LANES, (c + 1) * LANES) for c in range(n_pairs)]
    splits = [_split_bf16(proj[:, cols] * proj[:, cols]) for cols in pairs]
    sums = [_dot(hi, same_half) + _dot(lo, same_half) for hi, lo in splits]
    for c, (cols, ss) in enumerate(zip(pairs, sums)):
        qn = proj[:, cols] * lax.rsqrt(ss * (1.0 / B_HEAD_DIM) + NORM_EPS) * qgain_ref[:, cols]
        for h, own in ((c, jnp.where(upper, 0.0, qn)), (c + n_pairs, jnp.where(upper, qn, 0.0))):
            slab = slice(h * LANES, (h + 1) * LANES)
            q_ref[:, slab] = (own + qconst_ref[:, slab]).astype(BF16)
    gate_ref[...] = _sigmoid(proj[:, n_pairs * LANES:])


def nsa_q_proj(h2d, g_norm, w_in, q_norm, *, tm=512):
    t_total, d = h2d.shape
    g, hpg, dh = B_KV_HEADS, B_GROUP, B_HEAD_DIM
    d_q = B_HEADS * LANES
    n_pairs = B_HEADS // 2
    assert 2 * dh == LANES and all(_vector_in_upper_lanes(h // hpg) == (h >= n_pairs) for h in range(B_HEADS))
    wq = w_in[:, :B_HEADS * dh].reshape(d, 2, n_pairs, dh).transpose(0, 2, 1, 3).reshape(d, n_pairs * LANES)
    wg = w_in[:, B_HEADS * dh:].reshape(d, g, hpg * 3)
    wg = jnp.pad(wg, ((0, 0), (0, 0), (0, LANES - hpg * 3))).reshape(d, g * LANES)
    w = jnp.concatenate([wq, wg], axis=1).astype(BF16)
    qgain = jnp.tile(q_norm * (dh ** -0.5 * LOG2E), 2 * n_pairs).reshape(1, n_pairs * LANES)
    slopes = np.array([2.0 ** (-8.0 * (h + 1) / B_HEADS) for h in range(B_HEADS)], dtype=np.float64) * LOG2E
    feats = _alibi_query_part(jnp.asarray(slopes, F32)).astype(F32)
    n_feat = feats.shape[1]
    qconst = jnp.concatenate([jnp.pad(feats[:n_pairs], ((0, 0), (dh, LANES - dh - n_feat))),
                              jnp.pad(feats[n_pairs:], ((0, 0), (0, LANES - n_feat)))]).reshape(1, d_q)
    const2 = lambda i: (0, 0)
    return pl.pallas_call(
        _q_proj_kernel,
        grid=(t_total // tm,),
        in_specs=[
            pl.BlockSpec((tm, d), lambda i: (i, 0)),
            _resident((1, d), const2),
            _resident((d, (n_pairs + g) * LANES), const2),
            _resident((1, n_pairs * LANES), const2),
            _resident((1, d_q), const2),
        ],
        out_specs=[pl.BlockSpec((tm, d_q), lambda i: (i, 0)), pl.BlockSpec((tm, g * LANES), lambda i: (i, 0))],
        out_shape=[jax.ShapeDtypeStruct((t_total, d_q), BF16), jax.ShapeDtypeStruct((t_total, g * LANES), F32)],
        compiler_params=_cparams(1),
        name="nsa_q_proj",
    )(h2d, g_norm.reshape(1, d), w, qgain, qconst)


def _nsa_attn_kernel(q_ref, gate_ref, kca_ref, vct_ref, ks_ref, vst_ref, kw_ref, vwt_ref,
                     ovlt_ref, eselt_ref, o_ref, live_smem, m_scr, acc_scr, *, tq, tk):
    hpg, dh = B_GROUP, B_HEAD_DIM
    t0 = pl.program_id(2) * tq
    rows = hpg * tq
    qb = q_ref[...]
    q_st = jnp.concatenate([qb[:, i * LANES:(i + 1) * LANES] for i in range(hpg)], axis=0)

    t_l = t0 + lax.broadcasted_iota(jnp.int32, (1, tq), 1)

    def normalised(acc):
        return acc[:dh] / acc[dh:2 * dh]

    span = WINDOW + tq
    w0 = pl.multiple_of(jnp.maximum(t0 - WINDOW, 0), LANES)
    dist = t_l - (w0 + lax.broadcasted_iota(jnp.int32, (span, 1), 0))
    pen = jnp.where((dist >= 0) & (dist < WINDOW), 0.0, NEG)
    s = _dot_nt(kw_ref[0, 0, pl.ds(w0, span), :], q_st) + jnp.concatenate([pen] * hpg, axis=1)
    m = jnp.max(s, axis=0, keepdims=True)
    o_win = normalised(_dot(vwt_ref[0, 0, :, pl.ds(w0, span)], jnp.exp2(s - m).astype(BF16)))

    n_half = kca_ref.shape[0]
    c_end = lax.broadcasted_iota(jnp.int32, (n_half, 1), 0) * CMP_STRIDE + (CMP_BLOCK - 1)
    t_lane = t0 + lax.broadcasted_iota(jnp.int32, (1, rows), 1) % tq
    seen = t_lane >= c_end
    s = jnp.where(seen, _dot_nt(kca_ref[...], q_st), NEG)
    m = jnp.max(s, axis=0, keepdims=True)
    e = jnp.where(seen, jnp.exp2(s - m), 0.0)
    p = e / jnp.maximum(jnp.sum(e, axis=0, keepdims=True), TINY)
    o_cmp = _dot(vct_ref[0], p.astype(BF16))[:dh]

    p_grp = p[:, 0:tq]
    for i in range(1, hpg):
        p_grp = p_grp + p[:, i * tq:(i + 1) * tq]
    p_hi, p_lo = _split_bf16(p_grp)
    n_blk = LANES // 2
    imp = (_dot(ovlt_ref[...], p_hi) + _dot(ovlt_ref[...], p_lo))[:n_blk]
    blk = lax.broadcasted_iota(jnp.int32, (n_blk, tq), 0)
    valid = blk * SEL_BLOCK <= t_l
    cur = t_l // SEL_BLOCK
    forced = valid & ((blk == 0) | (blk == cur) | (blk == cur - 1))
    work = jnp.where(forced, SEL_BIG, jnp.where(valid, imp, -SEL_BIG))
    blk_f = blk.astype(F32)
    chosen = jnp.zeros((n_blk, tq), F32)
    for _ in range(N_SELECT):
        top = jnp.max(work, axis=0, keepdims=True)
        first = jnp.min(jnp.where(work == top, blk_f, float(LANES)), axis=0, keepdims=True)
        hit = blk_f == first
        chosen = jnp.where(hit & (top > -0.5 * SEL_BIG), 1.0, chosen)
        work = jnp.where(hit, -3e38, work)
    blocks_per_tile = tk // SEL_BLOCK
    any_q = jnp.max(chosen, axis=1, keepdims=True)
    for kt in range(n_blk // blocks_per_tile):
        tile_any = jnp.max(any_q[kt * blocks_per_tile:(kt + 1) * blocks_per_tile])
        live_smem[kt] = (tile_any > 0.5).astype(jnp.int32)
    chosen = jnp.concatenate([chosen, jnp.zeros((LANES - n_blk, tq), F32)], axis=0).astype(BF16)

    m_scr[...] = jnp.full(m_scr.shape, NEG, F32)
    acc_scr[...] = jnp.zeros(acc_scr.shape, F32)

    def sel_step(kt, carry):
        @pl.when(live_smem[kt] != 0)
        def _():
            k0 = pl.multiple_of(kt * tk, tk)
            pos = k0 + lax.broadcasted_iota(jnp.int32, (tk, 1), 0)
            picked = _dot(eselt_ref[pl.ds(k0, tk), :], chosen)
            pen = jnp.where((picked > 0.5) & (pos <= t_l), 0.0, NEG)
            sc = _dot_nt(ks_ref[0, 0, pl.ds(k0, tk), :], q_st) + jnp.concatenate([pen] * hpg, axis=1)
            m_run = m_scr[...]
            m_new = jnp.maximum(m_run, jnp.broadcast_to(jnp.max(sc, axis=0, keepdims=True), m_run.shape))
            pr = jnp.exp2(sc - m_new[0:1]).astype(BF16)
            acc_scr[...] = (jnp.exp2(m_run - m_new)[0:1] * acc_scr[...]
                            + _dot(vst_ref[0, 0, :, pl.ds(k0, tk)], pr))
            m_scr[...] = m_new
        return carry

    lax.fori_loop(0, (t0 + tq + tk - 1) // tk, sel_step, 0)
    o_sel = normalised(acc_scr[...])

    gate_t = gate_ref[...].T
    outs = []
    for i in range(hpg):
        c = slice(i * tq, (i + 1) * tq)
        outs.append(gate_t[3 * i:3 * i + 1] * o_cmp[:, c] + gate_t[3 * i + 1:3 * i + 2] * o_sel[:, c]
                    + gate_t[3 * i + 2:3 * i + 3] * o_win[:, c])
    o_ref[...] = jnp.concatenate(outs, axis=0).T.astype(BF16)


def nsa_attention(q, gates, kca, vct, ks, vst, kw, vwt, bsz, seq, *, tq=256, tk=512):
    g, hpg, dh = B_KV_HEADS, B_GROUP, B_HEAD_DIM
    n_q = seq // tq
    n_half = seq // CMP_STRIDE
    n_cmp = n_half - 1
    n_sel = seq // SEL_BLOCK
    assert n_sel <= LANES // 2 and tq % LANES == 0
    c_start = np.arange(n_half) * CMP_STRIDE
    j_sel = np.arange(LANES)
    ovl = ((c_start[:, None] < (j_sel[None, :] + 1) * SEL_BLOCK)
           & (c_start[:, None] + CMP_BLOCK > j_sel[None, :] * SEL_BLOCK)
           & (np.arange(n_half)[:, None] < n_cmp) & (j_sel[None, :] < n_sel))
    esel = (np.arange(seq)[None, :] // SEL_BLOCK) == j_sel[:, None]
    row = lambda b, j, i: (b * n_q + i, j)
    per_bg = lambda b, j, i: (b, j, 0, 0)
    return pl.pallas_call(
        functools.partial(_nsa_attn_kernel, tq=tq, tk=tk),
        grid=(bsz, g, n_q),
        in_specs=[
            pl.BlockSpec((tq, hpg * LANES), row),
            pl.BlockSpec((tq, LANES), row),
            pl.BlockSpec((n_half, LANES), lambda b, j, i: (b * g + j, 0)),
            pl.BlockSpec((1, LANES, n_half), lambda b, j, i: (b * g + j, 0, 0)),
            pl.BlockSpec((1, 1, seq, LANES), per_bg),
            pl.BlockSpec((1, 1, LANES, seq), per_bg),
            pl.BlockSpec((1, 1, seq, LANES), per_bg),
            pl.BlockSpec((1, 1, LANES, seq), per_bg),
            _resident((LANES, n_half), lambda b, j, i: (0, 0)),
            _resident((seq, LANES), lambda b, j, i: (0, 0)),
        ],
        out_specs=pl.BlockSpec((tq, hpg * dh), row),
        out_shape=jax.ShapeDtypeStruct((bsz * seq, B_HEADS * dh), BF16),
        scratch_shapes=[pltpu.SMEM((seq // tk,), jnp.int32), pltpu.VMEM((SUBLANES, hpg * tq), F32),
                        pltpu.VMEM((LANES, hpg * tq), F32)],
        compiler_params=_cparams(3),
        name="nsa_attention",
    )(q, gates, kca, vct, ks, vst, kw, vwt, jnp.asarray(ovl.T, BF16), jnp.asarray(esel.T, BF16))


def _out_proj_kernel(h_ref, o_ref, w_ref, out_ref):
    out_ref[...] = h_ref[...] + _dot(o_ref[...], w_ref[...])


def out_proj_residual(h2d, o, w_out, *, tm=512):
    t_total, d = h2d.shape
    k = o.shape[1]
    return pl.pallas_call(
        _out_proj_kernel,
        grid=(t_total // tm,),
        in_specs=[pl.BlockSpec((tm, d), lambda i: (i, 0)), pl.BlockSpec((tm, k), lambda i: (i, 0)),
                  _resident((k, d), lambda i: (0, 0))],
        out_specs=pl.BlockSpec((tm, d), lambda i: (i, 0)),
        out_shape=jax.ShapeDtypeStruct((t_total, d), F32),
        compiler_params=_cparams(1),
        name="out_proj_residual",
    )(h2d, o, w_out.astype(BF16))


def nsa_shared_kv(h2d, bsz, seq, kv_norm, kv_w, cmp_pos_k, cmp_w1_k, cmp_w2_k, cmp_pos_v, cmp_w1_v, cmp_w2_v, k_norm):
    ks, kw, vst, vwt, kc_raw, vc_raw = nsa_kv_proj(h2d, bsz, seq, kv_norm, kv_w, k_norm)
    kca, vct = nsa_compress(kc_raw, vc_raw, bsz, seq, cmp_pos_k, cmp_w1_k, cmp_w2_k,
                            cmp_pos_v, cmp_w1_v, cmp_w2_v, k_norm[0])
    return kca, vct, ks, vst, kw, vwt


def nsa_mixer(h2d, bsz, seq, g_norm, w_in, q_norm, kv):
    q, gates = nsa_q_proj(h2d, g_norm, w_in, q_norm)
    return nsa_attention(q, gates, *kv, bsz, seq)


def kernel(x, norm_mix, norm_ffn, hgrn_w_in, hgrn_w_out, hgrn_g_norm, hgrn_lb_logits, kv_norm, kv_w, cmp_pos_k, cmp_w1_k, cmp_w2_k, cmp_pos_v, cmp_w1_v, cmp_w2_v, k_norm, nsa_w_in, nsa_w_out, nsa_q_norm, ffn_w_gu, ffn_w_down, moe_router, moe_w_gu, moe_w_down):
    bsz, seq, d = x.shape
    lb = jnp.cumsum(jax.nn.softmax(hgrn_lb_logits.astype(F32), axis=0), axis=0)
    lb = lb - lb[0:1]
    n_a = hgrn_w_in.shape[0]
    depth = norm_mix.shape[0]
    h = x.reshape(bsz * seq, d)
    kv = None
    for layer in range(depth):
        mixer_out = None
        if layer < n_a:
            h = hgrn_mixer(h, seq, norm_mix[layer], hgrn_w_in[layer], hgrn_w_out[layer], hgrn_g_norm[layer],
                           lb[layer])
        else:
            if kv is None:
                kv = nsa_shared_kv(h, bsz, seq, kv_norm, kv_w, cmp_pos_k, cmp_w1_k, cmp_w2_k,
                                   cmp_pos_v, cmp_w1_v, cmp_w2_v, k_norm)
            j = layer - n_a
            mixer_out = (nsa_mixer(h, bsz, seq, norm_mix[layer], nsa_w_in[j], nsa_q_norm[j], kv), nsa_w_out[j])
            if layer % 2 == 1:
                h = out_proj_residual(h, *mixer_out)
                mixer_out = None
        if layer % 2 == 0:
            h = ffn_swiglu(h, norm_ffn[layer], ffn_w_gu[layer // 2], ffn_w_down[layer // 2], mixer_out=mixer_out)
        else:
            h = moe_swiglu_top2(h, norm_ffn[layer], moe_router[layer // 2], moe_w_gu[layer // 2],
                                moe_w_down[layer // 2])
    return h.reshape(bsz, seq, d)
```
